```python
import math
import jax
import jax.numpy as jnp
from jax import lax
import numpy as np

D_MODEL = 1024
BATCH = 32
SEQ = 2048
DEPTH = 2
DEC_BATCH = 128
DEC_SEQ = 1
PAST_LEN = 16384
PAGE_SIZE = 128

GDN_HEADS = 4
GDN_DK = 128
GDN_DV = 128
GDN_CONV = 4
MLA_HEADS = 8
MLA_Q_LORA = 256
MLA_KV_LORA = 128
MLA_NOPE = 64
MLA_ROPE = 32
MLA_VDIM = 64
ROPE_BASE = 10000.0
ATTN_BLOCK = 128
ML_HEADS = 4
ML_DK = 64
ML_DV = 128
CHUNK = 64
D_FF = 2816
FFN_CONV = 3
NORM_EPS = 1e-6

GDN_QK = GDN_HEADS * GDN_DK
GDN_V = GDN_HEADS * GDN_DV
GDN_QKV = 2 * GDN_QK + GDN_V
MLA_QHEAD = MLA_NOPE + MLA_ROPE
MLA_V = MLA_HEADS * MLA_VDIM
MLA_SCALE = MLA_QHEAD ** -0.5
ML_QK = ML_HEADS * ML_DK
ML_V = ML_HEADS * ML_DV
ML_QKV = 2 * ML_QK + ML_V
IN_SIZES = (GDN_QKV, GDN_HEADS, GDN_HEADS, GDN_V, MLA_Q_LORA, MLA_KV_LORA, MLA_ROPE, ML_QKV, ML_HEADS, ML_HEADS, ML_V, 3 * D_MODEL)
IN_WIDTH = sum(IN_SIZES)

kernel_name = 'hybrid_gdn_mla_mlstm_convffn_step'


def rmsnorm(x, g):
    xf = x.astype(jnp.float32)
    y = xf * lax.rsqrt(jnp.mean(xf * xf, axis=-1, keepdims=True) + NORM_EPS)
    return (y * g.astype(jnp.float32)).astype(x.dtype)


def l2norm(x):
    xf = x.astype(jnp.float32)
    return xf * lax.rsqrt(jnp.sum(xf * xf, axis=-1, keepdims=True) + NORM_EPS)


def causal_dwconv(x, hist, w):
    width, length = w.shape[0], x.shape[1]
    xp = jnp.concatenate([hist.astype(x.dtype), x], axis=1)
    y = xp[:, 0:length] * w[0]
    for j in range(1, width):
        y = y + xp[:, j:j + length] * w[j]
    return y, xp[:, length:]


def rope(x, pos):
    half = x.shape[-1] // 2
    inv_freq = ROPE_BASE ** (-jnp.arange(half, dtype=jnp.float32) / half)
    ang = pos.astype(jnp.float32)[:, None] * inv_freq[None, :]
    bshape = (ang.shape[0],) + (1,) * (x.ndim - 3) + (half,)
    cos, sin = jnp.cos(ang).reshape(bshape), jnp.sin(ang).reshape(bshape)
    xf = x.astype(jnp.float32)
    x1, x2 = xf[..., :half], xf[..., half:]
    return jnp.concatenate([x1 * cos - x2 * sin, x1 * sin + x2 * cos], axis=-1).astype(x.dtype)


def to_chunks(a, chunk):
    b, t = a.shape[:2]
    return jnp.swapaxes(a.reshape((b, t // chunk, chunk) + a.shape[2:]), 2, 3)


def causal_masks(n):
    idx = jnp.arange(n)
    return idx[:, None] >= idx[None, :], idx[:, None] > idx[None, :]


def gated_delta_rule(q, k, v, beta, g, S0, chunk):
    b, t, h, dk = q.shape
    dv = v.shape[-1]
    q, k, v, beta, g = [to_chunks(a.astype(jnp.float32), chunk) for a in (q, k, v, beta, g)]
    incl, strict = causal_masks(chunk)
    G = jnp.cumsum(g, axis=-1)
    dec_incl = jnp.exp(jnp.where(incl, G[..., :, None] - G[..., None, :], -jnp.inf))
    A = beta[..., :, None] * jnp.einsum('bnhld,bnhmd->bnhlm', k, k) * jnp.where(strict, dec_incl, 0.0)
    IA = A + jnp.eye(chunk, dtype=jnp.float32)
    U = lax.linalg.triangular_solve(IA, beta[..., None] * v, left_side=True, lower=True, unit_diagonal=True)
    W = lax.linalg.triangular_solve(IA, (beta * jnp.exp(G))[..., None] * k, left_side=True, lower=True, unit_diagonal=True)
    QK = jnp.einsum('bnhld,bnhmd->bnhlm', q, k) * dec_incl
    q_dec = q * jnp.exp(G)[..., None]
    k_dec = k * jnp.exp(G[..., -1:] - G)[..., None]
    g_last = jnp.exp(G[..., -1])

    def step(S, xs):
        U_c, W_c, QK_c, qd_c, kd_c, gl_c = xs
        v_new = U_c - jnp.einsum('bhld,bhde->bhle', W_c, S)
        o = jnp.einsum('bhld,bhde->bhle', qd_c, S) + jnp.einsum('bhlm,bhme->bhle', QK_c, v_new)
        S = gl_c[..., None, None] * S + jnp.einsum('bhld,bhle->bhde', kd_c, v_new)
        return S, o

    xs = tuple(jnp.moveaxis(a, 1, 0) for a in (U, W, QK, q_dec, k_dec, g_last))
    S, o = lax.scan(step, S0.astype(jnp.float32), xs)
    o = jnp.transpose(o, (1, 0, 3, 2, 4)).reshape(b, t, h, dv)
    return o, S


def mlstm_chunked(q, k, v, i_pre, logf, C0, n0, m0, chunk):
    b, t, h, dk = q.shape
    dv = v.shape[-1]
    q, k, v, i_pre, logf = [to_chunks(a.astype(jnp.float32), chunk) for a in (q, k, v, i_pre, logf)]
    incl, _ = causal_masks(chunk)
    F = jnp.cumsum(logf, axis=-1)
    D = jnp.where(incl, F[..., :, None] - F[..., None, :] + i_pre[..., None, :], -jnp.inf)
    D_max = jnp.max(D, axis=-1)
    QK = jnp.einsum('bnhld,bnhmd->bnhlm', q, k)

    def step(carry, xs):
        C, n, m = carry
        q_c, k_c, v_c, F_c, D_c, Dm_c, QK_c = xs
        b_log = F_c + m[..., None]
        m_r = jnp.maximum(b_log, Dm_c)
        inter = jnp.exp(b_log - m_r)
        P = jnp.exp(D_c - m_r[..., None]) * QK_c
        num = inter[..., None] * jnp.einsum('bhld,bhde->bhle', q_c, C) + jnp.einsum('bhlm,bhme->bhle', P, v_c)
        den = inter * jnp.einsum('bhld,bhd->bhl', q_c, n) + jnp.sum(P, axis=-1)
        h_t = num / jnp.maximum(jnp.abs(den), jnp.exp(-m_r))[..., None]
        m_new = m_r[..., -1]
        w_key = jnp.exp(D_c[..., -1, :] - m_new[..., None])
        carry_decay = jnp.exp(b_log[..., -1] - m_new)
        C = carry_decay[..., None, None] * C + jnp.einsum('bhl,bhld,bhle->bhde', w_key, k_c, v_c)
        n = carry_decay[..., None] * n + jnp.einsum('bhl,bhld->bhd', w_key, k_c)
        return (C, n, m_new), h_t

    xs = tuple(jnp.moveaxis(a, 1, 0) for a in (q, k, v, F, D, D_max, QK))
    init = (C0.astype(jnp.float32), n0.astype(jnp.float32), m0.astype(jnp.float32))
    (C, n, m), hs = lax.scan(step, init, xs)
    hs = jnp.transpose(hs, (1, 0, 3, 2, 4)).reshape(b, t, h, dv)
    return hs, C, n, m


def mla_prompt_attention(q_nope, q_rope, c_kv, k_rope, w_uk, w_uv):
    bsz, length = c_kv.shape[:2]
    k_nope = jnp.einsum('bsc,chd->bshd', c_kv, w_uk.reshape(MLA_KV_LORA, MLA_HEADS, MLA_NOPE))
    v = jnp.einsum('bsc,chd->bshd', c_kv, w_uv.reshape(MLA_KV_LORA, MLA_HEADS, MLA_VDIM))
    n_blk = length // ATTN_BLOCK

    def blocks(a):
        return jnp.swapaxes(a.reshape((bsz, n_blk, ATTN_BLOCK) + a.shape[2:]), 0, 1)

    kpos = jnp.arange(length)

    def attend_block(args):
        qn, qr, start = args
        s = jnp.einsum('bqhd,bkhd->bhqk', qn, k_nope) + jnp.einsum('bqhr,bkr->bhqk', qr, k_rope)
        qpos = start + jnp.arange(ATTN_BLOCK)
        s = jnp.where(kpos[None, :] <= qpos[:, None], s.astype(jnp.float32) * MLA_SCALE, -jnp.inf)
        p = jax.nn.softmax(s, axis=-1).astype(v.dtype)
        return jnp.einsum('bhqk,bkhd->bqhd', p, v)

    out = lax.map(attend_block, (blocks(q_nope), blocks(q_rope), jnp.arange(n_blk) * ATTN_BLOCK))
    return jnp.swapaxes(out, 0, 1).reshape(bsz, length, MLA_HEADS, MLA_VDIM)


def mla_sample_attention(q_nope, q_rope, c_kv, k_rope, w_uk, w_uv, lat_past, kr_past):
    length = c_kv.shape[1]
    past = lat_past.shape[1]
    q_lat = jnp.einsum('bqhd,chd->bqhc', q_nope, w_uk.reshape(MLA_KV_LORA, MLA_HEADS, MLA_NOPE))
    s_past = jnp.einsum('bqhc,bkc->bhqk', q_lat, lat_past) + jnp.einsum('bqhr,bkr->bhqk', q_rope, kr_past)
    s_new = jnp.einsum('bqhc,bkc->bhqk', q_lat, c_kv) + jnp.einsum('bqhr,bkr->bhqk', q_rope, k_rope)
    incl, _ = causal_masks(length)
    s = jnp.concatenate([s_past.astype(jnp.float32) * MLA_SCALE,
                         jnp.where(incl, s_new.astype(jnp.float32) * MLA_SCALE, -jnp.inf)], axis=-1)
    p = jax.nn.softmax(s, axis=-1).astype(lat_past.dtype)
    ctx = (jnp.einsum('bhqk,bkc->bqhc', p[..., :past], lat_past)
           + jnp.einsum('bhqk,bkc->bqhc', p[..., past:], c_kv.astype(lat_past.dtype)))
    return jnp.einsum('bqhc,chd->bqhd', ctx, w_uv.reshape(MLA_KV_LORA, MLA_HEADS, MLA_VDIM))


def mixer(h, pos, gdn_conv0, gdn_S0, ml_C0, ml_n0, ml_m0, attend, chunk,
          w_in, gdn_conv_w, gdn_A_log, gdn_dt_bias, gdn_norm_g,
          mla_q_norm_g, mla_w_uq, mla_kv_norm_g, mla_w_uk, mla_w_uv,
          ml_b_i, ml_b_f, ml_norm_g, w_branch_a, w_branch_b, w_branch_c, w_out):
    bsz, length, _ = h.shape
    f32 = jnp.float32
    z = h @ w_in
    (a_qkv, a_beta, a_alpha, a_z, b_cq, b_ckv, b_kr,
     c_qkv, c_i, c_f, c_o, gates) = jnp.split(z, np.cumsum(IN_SIZES)[:-1].tolist(), axis=-1)
    a_qkv, gdn_conv = causal_dwconv(a_qkv, gdn_conv0, gdn_conv_w)
    a_qkv = jax.nn.silu(a_qkv)
    aq, ak, av = jnp.split(a_qkv, [GDN_QK, 2 * GDN_QK], axis=-1)
    aq = l2norm(aq.reshape(bsz, length, GDN_HEADS, GDN_DK)) * (GDN_DK ** -0.5)
    ak = l2norm(ak.reshape(bsz, length, GDN_HEADS, GDN_DK))
    av = av.reshape(bsz, length, GDN_HEADS, GDN_DV)
    beta = jax.nn.sigmoid(a_beta.astype(f32))
    g = -jnp.exp(gdn_A_log.astype(f32)) * jax.nn.softplus(a_alpha.astype(f32) + gdn_dt_bias.astype(f32))
    o_a, gdn_S = gated_delta_rule(aq, ak, av, beta, g, gdn_S0, chunk)
    o_a = rmsnorm(o_a, gdn_norm_g) * jax.nn.silu(a_z.astype(f32)).reshape(bsz, length, GDN_HEADS, GDN_DV)
    o_a = o_a.reshape(bsz, length, GDN_V).astype(h.dtype)
    q = (rmsnorm(b_cq, mla_q_norm_g) @ mla_w_uq).reshape(bsz, length, MLA_HEADS, MLA_QHEAD)
    q_nope, q_rope = q[..., :MLA_NOPE], rope(q[..., MLA_NOPE:], pos)
    c_kv = rmsnorm(b_ckv, mla_kv_norm_g)
    k_rope = rope(b_kr, pos)
    o_b = attend(q_nope, q_rope, c_kv, k_rope, mla_w_uk, mla_w_uv).reshape(bsz, length, MLA_V).astype(h.dtype)
    cq, ck, cv = jnp.split(c_qkv, [ML_QK, 2 * ML_QK], axis=-1)
    cq = cq.reshape(bsz, length, ML_HEADS, ML_DK)
    ck = ck.reshape(bsz, length, ML_HEADS, ML_DK) * (ML_DK ** -0.5)
    cv = cv.reshape(bsz, length, ML_HEADS, ML_DV)
    i_pre = c_i.astype(f32) + ml_b_i.astype(f32)
    logf = jax.nn.log_sigmoid(c_f.astype(f32) + ml_b_f.astype(f32))
    h_c, ml_C, ml_n, ml_m = mlstm_chunked(cq, ck, cv, i_pre, logf, ml_C0, ml_n0, ml_m0, chunk)
    o_gate = jax.nn.sigmoid(c_o.astype(f32)).reshape(bsz, length, ML_HEADS, ML_DV)
    o_c = rmsnorm(o_gate * h_c, ml_norm_g).reshape(bsz, length, ML_V).astype(h.dtype)
    g_a, g_b, g_c = jnp.split(jax.nn.sigmoid(gates), 3, axis=-1)
    merged = g_a * (o_a @ w_branch_a) + g_b * (o_b @ w_branch_b) + g_c * (o_c @ w_branch_c)
    return merged @ w_out, (c_kv, k_rope, gdn_conv, gdn_S, ml_C, ml_n, ml_m)


def conv_ffn(h, conv0, w_up, conv_w, conv_b, w_down):
    u, gt = jnp.split(h @ w_up, 2, axis=-1)
    gt_c, new_buf = causal_dwconv(gt, conv0, conv_w)
    return (jax.nn.silu(gt_c + conv_b) * u) @ w_down, new_buf


def setup_inputs(seed: int = 0) -> dict:
    key = jax.random.key(seed)
    keys = jax.random.split(key, 40)
    f32 = jnp.float32
    n_pages = PAST_LEN // PAGE_SIZE
    n_pool = (DEC_BATCH * n_pages * 5) // 4

    def nrm(i, shape, scale=1.0):
        return jax.random.normal(keys[i], shape, f32) * scale

    def gain(i, shape):
        return 1.0 + nrm(i, shape, 0.02)

    page_table = jax.random.permutation(keys[0], n_pool)[:DEC_BATCH * n_pages].reshape(DEC_BATCH, n_pages).astype(jnp.int32)
    dt = jnp.exp(jax.random.uniform(keys[1], (DEPTH, GDN_HEADS), f32, math.log(1e-3), math.log(1e-1)))
    return {
        'x_prompt': nrm(2, (BATCH, SEQ, D_MODEL)),
        'x_sample': nrm(3, (DEC_BATCH, DEC_SEQ, D_MODEL)),
        'cache_kv_latent': nrm(4, (DEPTH, n_pool, PAGE_SIZE, MLA_KV_LORA)),
        'cache_k_rope': nrm(5, (DEPTH, n_pool, PAGE_SIZE, MLA_ROPE)),
        'state_gdn_conv': nrm(6, (DEPTH, DEC_BATCH, GDN_CONV - 1, GDN_QKV)),
        'state_gdn_S': nrm(7, (DEPTH, DEC_BATCH, GDN_HEADS, GDN_DK, GDN_DV), GDN_DK ** -0.5),
        'state_mlstm_C': nrm(8, (DEPTH, DEC_BATCH, ML_HEADS, ML_DK, ML_DV)),
        'state_mlstm_n': nrm(9, (DEPTH, DEC_BATCH, ML_HEADS, ML_DK)),
        'state_mlstm_m': nrm(10, (DEPTH, DEC_BATCH, ML_HEADS)),
        'state_ffn_conv': nrm(11, (DEPTH, DEC_BATCH, FFN_CONV - 1, D_FF)),
        'page_table': page_table,
        'norm1_g': gain(12, (DEPTH, D_MODEL)),
        'w_in': nrm(13, (DEPTH, D_MODEL, IN_WIDTH), D_MODEL ** -0.5),
        'gdn_conv_w': nrm(14, (DEPTH, GDN_CONV, GDN_QKV), GDN_CONV ** -0.5),
        'gdn_A_log': jnp.log(jax.random.uniform(keys[15], (DEPTH, GDN_HEADS), f32, 1.0, 16.0)),
        'gdn_dt_bias': dt + jnp.log(-jnp.expm1(-dt)),
        'gdn_norm_g': gain(16, (DEPTH, GDN_DV)),
        'mla_q_norm_g': gain(17, (DEPTH, MLA_Q_LORA)),
        'mla_w_uq': nrm(18, (DEPTH, MLA_Q_LORA, MLA_HEADS * MLA_QHEAD), MLA_Q_LORA ** -0.5),
        'mla_kv_norm_g': gain(19, (DEPTH, MLA_KV_LORA)),
        'mla_w_uk': nrm(20, (DEPTH, MLA_KV_LORA, MLA_HEADS * MLA_NOPE), MLA_KV_LORA ** -0.5),
        'mla_w_uv': nrm(21, (DEPTH, MLA_KV_LORA, MLA_HEADS * MLA_VDIM), MLA_KV_LORA ** -0.5),
        'ml_b_i': nrm(22, (DEPTH, ML_HEADS), 0.1),
        'ml_b_f': jnp.linspace(3.0, 6.0, ML_HEADS, dtype=f32)[None, :] + nrm(23, (DEPTH, ML_HEADS), 0.1),
        'ml_norm_g': gain(24, (DEPTH, ML_DV)),
        'w_branch_a': nrm(25, (DEPTH, GDN_V, D_MODEL), GDN_V ** -0.5),
        'w_branch_b': nrm(26, (DEPTH, MLA_V, D_MODEL), MLA_V ** -0.5),
        'w_branch_c': nrm(27, (DEPTH, ML_V, D_MODEL), ML_V ** -0.5),
        'w_out': nrm(28, (DEPTH, D_MODEL, D_MODEL), D_MODEL ** -0.5),
        'norm2_g': gain(29, (DEPTH, D_MODEL)),
        'ffn_w_up': nrm(30, (DEPTH, D_MODEL, 2 * D_FF), D_MODEL ** -0.5),
        'ffn_conv_w': nrm(31, (DEPTH, FFN_CONV, D_FF), FFN_CONV ** -0.5),
        'ffn_conv_b': nrm(32, (DEPTH, D_FF), 0.02),
        'ffn_w_down': nrm(33, (DEPTH, D_FF, D_MODEL), D_FF ** -0.5),
        'final_norm_g': gain(34, (D_MODEL,)),
    }


def reference(x_prompt, x_sample, cache_kv_latent, cache_k_rope, state_gdn_conv, state_gdn_S,
              state_mlstm_C, state_mlstm_n, state_mlstm_m, state_ffn_conv, page_table,
              norm1_g, w_in, gdn_conv_w, gdn_A_log, gdn_dt_bias, gdn_norm_g,
              mla_q_norm_g, mla_w_uq, mla_kv_norm_g, mla_w_uk, mla_w_uv,
              ml_b_i, ml_b_f, ml_norm_g, w_branch_a, w_branch_b, w_branch_c, w_out,
              norm2_g, ffn_w_up, ffn_conv_w, ffn_conv_b, ffn_w_down, final_norm_g):
    f32 = jnp.float32
    bp, seq = x_prompt.shape[:2]
    bs, dec = x_sample.shape[:2]
    past_len = page_table.shape[1] * PAGE_SIZE
    pos_p = jnp.arange(seq)
    pos_s = past_len + jnp.arange(dec)
    xp, xs = x_prompt, x_sample
    out_p = [[] for _ in range(8)]
    out_s = [[] for _ in range(8)]
    for l in range(DEPTH):
        lp = (w_in[l], gdn_conv_w[l], gdn_A_log[l], gdn_dt_bias[l], gdn_norm_g[l],
              mla_q_norm_g[l], mla_w_uq[l], mla_kv_norm_g[l], mla_w_uk[l], mla_w_uv[l],
              ml_b_i[l], ml_b_f[l], ml_norm_g[l], w_branch_a[l], w_branch_b[l], w_branch_c[l], w_out[l])
        m_out, st = mixer(rmsnorm(xp, norm1_g[l]), pos_p,
                          jnp.zeros((bp, GDN_CONV - 1, GDN_QKV), xp.dtype),
                          jnp.zeros((bp, GDN_HEADS, GDN_DK, GDN_DV), f32),
                          jnp.zeros((bp, ML_HEADS, ML_DK, ML_DV), f32),
                          jnp.zeros((bp, ML_HEADS, ML_DK), f32),
                          jnp.zeros((bp, ML_HEADS), f32),
                          mla_prompt_attention, CHUNK, *lp)
        xp = xp + m_out
        f_out, f_buf = conv_ffn(rmsnorm(xp, norm2_g[l]), jnp.zeros((bp, FFN_CONV - 1, D_FF), xp.dtype),
                                ffn_w_up[l], ffn_conv_w[l], ffn_conv_b[l], ffn_w_down[l])
        xp = xp + f_out
        for lst, a in zip(out_p, st + (f_buf,)):
            lst.append(a)
        lat_past = cache_kv_latent[l, page_table].reshape(bs, past_len, MLA_KV_LORA)
        kr_past = cache_k_rope[l, page_table].reshape(bs, past_len, MLA_ROPE)

        def attend_sample(qn, qr, ckv, kr, wuk, wuv, lat_past=lat_past, kr_past=kr_past):
            return mla_sample_attention(qn, qr, ckv, kr, wuk, wuv, lat_past, kr_past)

        m_out, st = mixer(rmsnorm(xs, norm1_g[l]), pos_s,
                          state_gdn_conv[l], state_gdn_S[l], state_mlstm_C[l], state_mlstm_n[l], state_mlstm_m[l],
                          attend_sample, dec, *lp)
        xs = xs + m_out
        f_out, f_buf = conv_ffn(rmsnorm(xs, norm2_g[l]), state_ffn_conv[l],
                                ffn_w_up[l], ffn_conv_w[l], ffn_conv_b[l], ffn_w_down[l])
        xs = xs + f_out
        for lst, a in zip(out_s, st + (f_buf,)):
            lst.append(a)
    y_prompt = rmsnorm(xp, final_norm_g)
    y_sample = rmsnorm(xs, final_norm_g)
    p_ckv, p_kr, p_gconv, p_gS, p_C, p_n, p_m, p_fconv = [jnp.stack(a) for a in out_p]
    s_ckv, s_kr, s_gconv, s_gS, s_C, s_n, s_m, s_fconv = [jnp.stack(a) for a in out_s]
    return (y_prompt, y_sample, p_ckv, p_kr, p_gconv, p_gS, p_C, p_n, p_m, p_fconv,
            s_ckv, s_kr, s_gconv, s_gS, s_C, s_n, s_m, s_fconv)
```

```python
import functools
import math

import jax
import jax.numpy as jnp
import numpy as np
from jax import lax
from jax.experimental import pallas as pl
from jax.experimental.pallas import tpu as pltpu

F32 = jnp.float32
BF16 = jnp.bfloat16

D_MODEL = 1024
PAGE_SIZE = 128
GDN_HEADS, GDN_DK, GDN_DV, GDN_CONV = 4, 128, 128, 4
MLA_HEADS, MLA_Q_LORA, MLA_KV_LORA, MLA_NOPE, MLA_ROPE, MLA_VDIM = 8, 256, 128, 64, 32, 64
ROPE_BASE = 10000.0
ML_HEADS, ML_DK, ML_DV = 4, 64, 128
CHUNK = 64
D_FF = 2816
FFN_CONV = 3
NORM_EPS = 1e-6

GDN_QK = GDN_HEADS * GDN_DK
GDN_V = GDN_HEADS * GDN_DV
GDN_QKV = 2 * GDN_QK + GDN_V
MLA_QHEAD = MLA_NOPE + MLA_ROPE
MLA_V = MLA_HEADS * MLA_VDIM
MLA_SCALE = MLA_QHEAD ** -0.5
ML_QK = ML_HEADS * ML_DK
ML_V = ML_HEADS * ML_DV
ML_QKV = 2 * ML_QK + ML_V
IN_SIZES = (GDN_QKV, GDN_HEADS, GDN_HEADS, GDN_V, MLA_Q_LORA, MLA_KV_LORA, MLA_ROPE,
            ML_QKV, ML_HEADS, ML_HEADS, ML_V, 3 * D_MODEL)

LANE = 128
HEAD_PAD = 128
SM_BETA, SM_ALPHA, SM_I, SM_F, SM_KR = 0, 8, 16, 24, 64
VMEM_LIMIT = 56 * 1024 * 1024


def _cparams(sem):
    return pltpu.CompilerParams(dimension_semantics=sem, vmem_limit_bytes=VMEM_LIMIT)


def _const_spec(shape):
    nd = len(shape)
    return pl.BlockSpec(shape, lambda *_: (0,) * nd, pipeline_mode=pl.Buffered(1))


def _dot(a, b):
    return jnp.dot(a.astype(BF16), b.astype(BF16), preferred_element_type=F32)


def _dot_nt(a, b):
    return lax.dot_general(a.astype(BF16), b.astype(BF16), (((1,), (1,)), ((), ())),
                           preferred_element_type=F32)


def _dot_tn(a, b):
    return lax.dot_general(a.astype(BF16), b.astype(BF16), (((0,), (0,)), ((), ())),
                           preferred_element_type=F32)


def _dot01(mask_bf16, x):
    x1 = x.astype(BF16)
    r1 = x - x1.astype(F32)
    x2 = r1.astype(BF16)
    x3 = (r1 - x2.astype(F32)).astype(BF16)
    d = functools.partial(jnp.dot, preferred_element_type=F32)
    return d(mask_bf16, x1) + d(mask_bf16, x2) + d(mask_bf16, x3)


def _rms(x, g):
    return x * lax.rsqrt(jnp.mean(x * x, axis=-1, keepdims=True) + NORM_EPS) * g


def _l2(x):
    return x * lax.rsqrt(jnp.sum(x * x, axis=-1, keepdims=True) + NORM_EPS)


def _sigmoid(x):
    return jax.nn.sigmoid(x)


def _silu(x):
    return x * jax.nn.sigmoid(x)


def _softplus(x):
    return jnp.maximum(x, 0.0) + jnp.log1p(jnp.exp(-jnp.abs(x)))


def _tri_masks(n):
    r = lax.broadcasted_iota(jnp.int32, (n, n), 0)
    c = lax.broadcasted_iota(jnp.int32, (n, n), 1)
    return r >= c, r > c, r == c


def _neumann_inverse(a, eye_f):
    n = a.shape[0]
    p = -a
    t = eye_f + p
    covered = 2
    while covered < n:
        p = _dot(p, p)
        t = t + _dot(t, p)
        covered *= 2
    return t


def _rope_swap(x, lane0):
    w = x.shape[-1]
    half = MLA_ROPE // 2
    lane = lax.broadcasted_iota(jnp.int32, x.shape, x.ndim - 1) % LANE
    left = pltpu.roll(x, w - half, x.ndim - 1)
    right = pltpu.roll(x, half, x.ndim - 1)
    first = (lane >= lane0) & (lane < lane0 + half)
    second = (lane >= lane0 + half) & (lane < lane0 + 2 * half)
    return jnp.where(first, left, jnp.where(second, right, 0.0))


def _in_proj_kernel(x_ref, g_ref, w_ref, *out_refs):
    xn = _rms(x_ref[...], g_ref[...]).astype(BF16)
    off = 0
    for ref in out_refs:
        n = ref.shape[-1]
        ref[...] = jnp.dot(xn, w_ref[:, off:off + n], preferred_element_type=F32).astype(ref.dtype)
        off += n


def _in_proj(x, g, w_packed, groups, tm):
    m = x.shape[0]
    width = w_packed.shape[1]
    assert sum(n for n, _ in groups) == width and m % tm == 0
    return pl.pallas_call(
        _in_proj_kernel,
        grid=(m // tm,),
        in_specs=[pl.BlockSpec((tm, D_MODEL), lambda i: (i, 0)),
                  _const_spec((1, D_MODEL)),
                  _const_spec((D_MODEL, width))],
        out_specs=[pl.BlockSpec((tm, n), lambda i: (i, 0)) for n, _ in groups],
        out_shape=[jax.ShapeDtypeStruct((m, n), dt) for n, dt in groups],
        compiler_params=_cparams(("parallel",)),
        name="in_proj",
    )(x, g.reshape(1, D_MODEL), w_packed)


def _small_cols(a_beta, a_alpha, b_kr, c_i, c_f):
    small = jnp.zeros((D_MODEL, LANE), F32)
    small = small.at[:, SM_BETA:SM_BETA + GDN_HEADS].set(a_beta)
    small = small.at[:, SM_ALPHA:SM_ALPHA + GDN_HEADS].set(a_alpha)
    small = small.at[:, SM_I:SM_I + ML_HEADS].set(c_i)
    small = small.at[:, SM_F:SM_F + ML_HEADS].set(c_f)
    small = small.at[:, SM_KR:SM_KR + MLA_ROPE].set(b_kr)
    return small


PROMPT_GROUPS = ((GDN_QKV, F32), (GDN_V, BF16), (MLA_Q_LORA, BF16), (MLA_KV_LORA, F32),
                 (ML_QKV, BF16), (ML_V, BF16), (3 * D_MODEL, BF16), (LANE, F32))
SAMPLE_GROUPS = ((GDN_QKV, F32), (GDN_V, F32), (MLA_Q_LORA, F32), (MLA_KV_LORA, F32),
                 (2 * ML_HEADS * LANE + ML_V, F32), (ML_V, F32), (3 * D_MODEL, F32), (LANE, F32))


def _pack_w_in(w_in, pad_ml_heads):
    (a_qkv, a_beta, a_alpha, a_z, b_cq, b_ckv, b_kr,
     c_qkv, c_i, c_f, c_o, gates) = jnp.split(w_in, np.cumsum(IN_SIZES)[:-1].tolist(), axis=-1)
    if pad_ml_heads:
        qk = c_qkv[:, :2 * ML_QK].reshape(D_MODEL, 2 * ML_HEADS, ML_DK)
        qk = jnp.pad(qk, ((0, 0), (0, 0), (0, LANE - ML_DK))).reshape(D_MODEL, 2 * ML_HEADS * LANE)
        c_qkv = jnp.concatenate([qk, c_qkv[:, 2 * ML_QK:]], axis=1)
    small = _small_cols(a_beta, a_alpha, b_kr, c_i, c_f)
    return jnp.concatenate([a_qkv, a_z, b_cq, b_ckv, c_qkv, c_o, gates, small], axis=1).astype(BF16)


def _gdn_prompt_kernel(aqkv_ref, az_ref, small_ref, convw_ref, alog_ref, dtb_ref, ng_ref,
                       o_ref, conv_out_ref, s_out_ref,
                       xp_scr, q_scr, k_scr, v_scr, beta_scr, g_scr, s_scr, *, tb):
    t = pl.program_id(1)
    c = CHUNK

    @pl.when(t == 0)
    def _():
        s_scr[...] = jnp.zeros_like(s_scr)
        xp_scr[0:8, :] = jnp.zeros((8, GDN_QKV), F32)

    x = aqkv_ref[...]
    xp_scr[8:8 + tb, :] = x
    w = convw_ref[...]
    y = (x * w[3:4] + xp_scr[7:7 + tb, :] * w[2:3]
         + xp_scr[6:6 + tb, :] * w[1:2] + xp_scr[5:5 + tb, :] * w[0:1])
    xp_scr[0:8, :] = x[tb - 8:tb, :]
    y = _silu(y)
    for h in range(GDN_HEADS):
        q_scr[h] = _l2(y[:, h * GDN_DK:(h + 1) * GDN_DK]) * (GDN_DK ** -0.5)
        k_scr[h] = _l2(y[:, GDN_QK + h * GDN_DK:GDN_QK + (h + 1) * GDN_DK])
        v_scr[h] = y[:, 2 * GDN_QK + h * GDN_DV:2 * GDN_QK + (h + 1) * GDN_DV]
    sm = small_ref[...]
    beta_scr[...] = _sigmoid(sm[:, SM_BETA:SM_BETA + GDN_HEADS])
    g_scr[...] = -jnp.exp(alog_ref[...]) * _softplus(sm[:, SM_ALPHA:SM_ALPHA + GDN_HEADS] + dtb_ref[...])

    incl, strict, eye = _tri_masks(c)
    lmat = jnp.where(incl, 1.0, 0.0).astype(BF16)
    eye_f = jnp.where(eye, 1.0, 0.0).astype(F32)
    ng = ng_ref[...]

    def chunk_step(ci, carry):
        r0 = pl.multiple_of(ci * c, c)
        beta_c = beta_scr[pl.ds(r0, c), :]
        g_c = g_scr[pl.ds(r0, c), :]
        for h in range(GDN_HEADS):
            q = q_scr[h, pl.ds(r0, c), :]
            k = k_scr[h, pl.ds(r0, c), :]
            v = v_scr[h, pl.ds(r0, c), :]
            bcol = beta_c[:, h:h + 1]
            m1 = jnp.broadcast_to(g_c[:, h:h + 1], (c, c))
            gd = _dot01(lmat, jnp.concatenate([m1, jnp.where(strict, m1, 0.0)], axis=1))
            gcol = gd[:, 0:1]
            dec = jnp.where(incl, jnp.exp(gd[:, c:]), 0.0)
            a = bcol * _dot_nt(k, k) * jnp.where(strict, dec, 0.0)
            tinv = _neumann_inverse(a, eye_f)
            e_g = jnp.exp(gcol)
            u = _dot(tinv, bcol * v)
            wm = _dot(tinv, (bcol * e_g) * k)
            qk = _dot_nt(q, k) * dec
            s_old = s_scr[h]
            v_new = u - _dot(wm, s_old)
            o = _dot(q * e_g, s_old) + _dot(qk, v_new)
            g_last = gcol[c - 1:c, :]
            s_scr[h] = jnp.exp(g_last) * s_old + _dot_tn(k * jnp.exp(g_last - gcol), v_new)
            z = az_ref[pl.ds(r0, c), h * GDN_DV:(h + 1) * GDN_DV].astype(F32)
            o_ref[pl.ds(r0, c), h * GDN_DV:(h + 1) * GDN_DV] = (_rms(o, ng) * _silu(z)).astype(o_ref.dtype)
        return carry

    lax.fori_loop(0, tb // c, chunk_step, 0)

    @pl.when(t == pl.num_programs(1) - 1)
    def _():
        conv_out_ref[0] = x[tb - (GDN_CONV - 1):tb, :]
        s_out_ref[0] = s_scr[...]


def _gdn_prompt(a_qkv, a_z, small, conv_w, a_log, dt_bias, norm_g, bsz, seq, tb):
    nt = seq // tb
    row = lambda b, t: (b * nt + t, 0)
    return pl.pallas_call(
        functools.partial(_gdn_prompt_kernel, tb=tb),
        grid=(bsz, nt),
        in_specs=[pl.BlockSpec((tb, GDN_QKV), row),
                  pl.BlockSpec((tb, GDN_V), row),
                  pl.BlockSpec((tb, LANE), row),
                  _const_spec((GDN_CONV, GDN_QKV)),
                  _const_spec((1, GDN_HEADS)),
                  _const_spec((1, GDN_HEADS)),
                  _const_spec((1, GDN_DV))],
        out_specs=[pl.BlockSpec((tb, GDN_V), row),
                   pl.BlockSpec((1, GDN_CONV - 1, GDN_QKV), lambda b, t: (b, 0, 0)),
                   pl.BlockSpec((1, GDN_HEADS, GDN_DK, GDN_DV), lambda b, t: (b, 0, 0, 0))],
        out_shape=[jax.ShapeDtypeStruct((bsz * seq, GDN_V), BF16),
                   jax.ShapeDtypeStruct((bsz, GDN_CONV - 1, GDN_QKV), F32),
                   jax.ShapeDtypeStruct((bsz, GDN_HEADS, GDN_DK, GDN_DV), F32)],
        scratch_shapes=[pltpu.VMEM((tb + 8, GDN_QKV), F32),
                        pltpu.VMEM((GDN_HEADS, tb, GDN_DK), F32),
                        pltpu.VMEM((GDN_HEADS, tb, GDN_DK), F32),
                        pltpu.VMEM((GDN_HEADS, tb, GDN_DV), F32),
                        pltpu.VMEM((tb, GDN_HEADS), F32),
                        pltpu.VMEM((tb, GDN_HEADS), F32),
                        pltpu.VMEM((GDN_HEADS, GDN_DK, GDN_DV), F32)],
        compiler_params=_cparams(("parallel", "arbitrary")),
        name="gdn_prompt",
    )(a_qkv, a_z, small, conv_w, a_log.reshape(1, GDN_HEADS), dt_bias.reshape(1, GDN_HEADS),
      norm_g.reshape(1, GDN_DV))


def _mlstm_prompt_kernel(cqkv_ref, co_ref, small_ref, bi_ref, bf_ref, ng_ref,
                         o_ref, c_out_ref, n_out_ref, m_out_ref,
                         ip_scr, lf_scr, c_scr, n_scr, m_scr, *, tb):
    t = pl.program_id(1)
    c = CHUNK

    @pl.when(t == 0)
    def _():
        c_scr[...] = jnp.zeros_like(c_scr)
        n_scr[...] = jnp.zeros_like(n_scr)
        m_scr[...] = jnp.zeros_like(m_scr)

    sm = small_ref[...]
    ip_scr[...] = sm[:, SM_I:SM_I + ML_HEADS] + bi_ref[...]
    lf_scr[...] = -_softplus(-(sm[:, SM_F:SM_F + ML_HEADS] + bf_ref[...]))

    incl, strict, eye = _tri_masks(c)
    lmat = jnp.where(incl, 1.0, 0.0).astype(BF16)
    ng = ng_ref[...]

    def chunk_step(ci, carry):
        r0 = pl.multiple_of(ci * c, c)
        ip_c = ip_scr[pl.ds(r0, c), :]
        lf_c = lf_scr[pl.ds(r0, c), :]
        for h in range(ML_HEADS):
            q = cqkv_ref[pl.ds(r0, c), h * ML_DK:(h + 1) * ML_DK]
            k = cqkv_ref[pl.ds(r0, c), ML_QK + h * ML_DK:ML_QK + (h + 1) * ML_DK].astype(F32) * (ML_DK ** -0.5)
            v = cqkv_ref[pl.ds(r0, c), 2 * ML_QK + h * ML_DV:2 * ML_QK + (h + 1) * ML_DV]
            icol = ip_c[:, h:h + 1]
            mf = jnp.broadcast_to(lf_c[:, h:h + 1], (c, c))
            mi = jnp.broadcast_to(icol, (c, c))
            r = _dot01(lmat, jnp.concatenate(
                [mf, jnp.where(strict, mf, 0.0), jnp.where(eye, mi, 0.0)], axis=1))
            fcol = r[:, 0:1]
            d = jnp.where(incl, r[:, c:2 * c] + r[:, 2 * c:], -jnp.inf)
            d_max = jnp.max(d, axis=-1, keepdims=True)
            qk = _dot_nt(q, k)
            m_old = m_scr[0:1, h:h + 1]
            c_old = c_scr[h]
            n_old = n_scr[h:h + 1, :]
            b_log = fcol + m_old
            m_r = jnp.maximum(b_log, d_max)
            inter = jnp.exp(b_log - m_r)
            p = jnp.exp(d - m_r) * qk
            num = inter * _dot(q, c_old) + _dot(p, v)
            den = (inter * jnp.sum(q.astype(F32) * n_old, axis=-1, keepdims=True)
                   + jnp.sum(p, axis=-1, keepdims=True))
            h_t = num / jnp.maximum(jnp.abs(den), jnp.exp(-m_r))
            m_new = m_r[c - 1:c, :]
            w_key = jnp.exp(fcol[c - 1:c, :] - fcol + icol - m_new)
            decay = jnp.exp(b_log[c - 1:c, :] - m_new)
            wk = w_key * k
            c_scr[h] = decay * c_old + _dot_tn(wk, v)
            n_scr[h:h + 1, :] = decay * n_old + jnp.sum(wk, axis=0, keepdims=True)
            m_scr[0:1, h:h + 1] = m_new
            og = _sigmoid(co_ref[pl.ds(r0, c), h * ML_DV:(h + 1) * ML_DV].astype(F32))
            o_ref[pl.ds(r0, c), h * ML_DV:(h + 1) * ML_DV] = _rms(og * h_t, ng).astype(o_ref.dtype)
        return carry

    lax.fori_loop(0, tb // c, chunk_step, 0)

    @pl.when(t == pl.num_programs(1) - 1)
    def _():
        c_out_ref[0] = c_scr[...]
        n_out_ref[0] = n_scr[...]
        m_out_ref[0] = m_scr[...]


def _mlstm_prompt(c_qkv, c_o, small, b_i, b_f, norm_g, bsz, seq, tb):
    nt = seq // tb
    row = lambda b, t: (b * nt + t, 0)
    return pl.pallas_call(
        functools.partial(_mlstm_prompt_kernel, tb=tb),
        grid=(bsz, nt),
        in_specs=[pl.BlockSpec((tb, ML_QKV), row),
                  pl.BlockSpec((tb, ML_V), row),
                  pl.BlockSpec((tb, LANE), row),
                  _const_spec((1, ML_HEADS)),
                  _const_spec((1, ML_HEADS)),
                  _const_spec((1, ML_DV))],
        out_specs=[pl.BlockSpec((tb, ML_V), row),
                   pl.BlockSpec((1, ML_HEADS, ML_DK, ML_DV), lambda b, t: (b, 0, 0, 0)),
                   pl.BlockSpec((1, ML_HEADS, ML_DK), lambda b, t: (b, 0, 0)),
                   pl.BlockSpec((1, 1, ML_HEADS), lambda b, t: (b, 0, 0))],
        out_shape=[jax.ShapeDtypeStruct((bsz * seq, ML_V), BF16),
                   jax.ShapeDtypeStruct((bsz, ML_HEADS, ML_DK, ML_DV), F32),
                   jax.ShapeDtypeStruct((bsz, ML_HEADS, ML_DK), F32),
                   jax.ShapeDtypeStruct((bsz, 1, ML_HEADS), F32)],
        scratch_shapes=[pltpu.VMEM((tb, ML_HEADS), F32),
                        pltpu.VMEM((tb, ML_HEADS), F32),
                        pltpu.VMEM((ML_HEADS, ML_DK, ML_DV), F32),
                        pltpu.VMEM((ML_HEADS, ML_DK), F32),
                        pltpu.VMEM((1, ML_HEADS), F32)],
        compiler_params=_cparams(("parallel", "arbitrary")),
        name="mlstm_prompt",
    )(c_qkv, c_o, small, b_i.reshape(1, ML_HEADS), b_f.reshape(1, ML_HEADS), norm_g.reshape(1, ML_DV))


def _rope_tables(pos):
    half = MLA_ROPE // 2
    inv_freq = ROPE_BASE ** (-jnp.arange(half, dtype=F32) / half)
    ang = pos.astype(F32)[:, None] * inv_freq[None, :]
    cos, sin = jnp.cos(ang), jnp.sin(ang)
    n = pos.shape[0]
    ones = jnp.ones((n, MLA_NOPE), F32)
    zeros = jnp.zeros((n, MLA_NOPE), F32)
    pad = jnp.zeros((n, LANE - MLA_QHEAD), F32)
    ctab = jnp.concatenate([ones, cos, cos, pad], axis=1)
    stab = jnp.concatenate([zeros, -sin, sin, pad], axis=1)
    return ctab, stab


def _mla_pre_kernel(cq_ref, ckv_ref, small_ref, ctab_ref, stab_ref, qg_ref, kvg_ref,
                    wuq_ref, wuk_ref, wuv_ref,
                    q_ref, k_ref, v_ref, ckv_out_ref, kr_out_ref):
    ctab = ctab_ref[...]
    stab = stab_ref[...]
    lane = lax.broadcasted_iota(jnp.int32, ctab.shape, 1)
    in_rope = (lane >= MLA_NOPE) & (lane < MLA_QHEAD)
    sm = small_ref[...]
    kr = jnp.where(in_rope, sm * ctab + _rope_swap(sm, MLA_NOPE) * stab, 0.0)
    kr_out_ref[...] = kr[:, MLA_NOPE:MLA_QHEAD]
    c_kv = _rms(ckv_ref[...], kvg_ref[...])
    ckv_out_ref[...] = c_kv
    c_kv_b = c_kv.astype(BF16)
    k_nope = jnp.dot(c_kv_b, wuk_ref[...], preferred_element_type=F32)
    v_ref[...] = jnp.dot(c_kv_b, wuv_ref[...], preferred_element_type=F32).astype(v_ref.dtype)
    qn = _rms(cq_ref[...].astype(F32), qg_ref[...]).astype(BF16)
    q = jnp.dot(qn, wuq_ref[...], preferred_element_type=F32)
    q_sw = _rope_swap(q, MLA_NOPE)
    for h in range(MLA_HEADS):
        sl = slice(h * HEAD_PAD, (h + 1) * HEAD_PAD)
        q_ref[:, sl] = ((q[:, sl] * ctab + q_sw[:, sl] * stab) * MLA_SCALE).astype(q_ref.dtype)
        k_ref[:, sl] = (k_nope[:, sl] + kr).astype(k_ref.dtype)


def _pad_heads(w, head_dim):
    rows = w.shape[0]
    w = w.reshape(rows, MLA_HEADS, head_dim)
    return jnp.pad(w, ((0, 0), (0, 0), (0, HEAD_PAD - head_dim))).reshape(rows, MLA_HEADS * HEAD_PAD)


def _mla_pre(b_cq, b_ckv, small, ctab, stab, q_norm_g, kv_norm_g, w_uq_pad, w_uk_pad, w_uv, seq, tm):
    m = b_cq.shape[0]
    nt = seq // tm
    row = lambda i: (i, 0)
    tab = lambda i: (i % nt, 0)
    hp = MLA_HEADS * HEAD_PAD
    return pl.pallas_call(
        _mla_pre_kernel,
        grid=(m // tm,),
        in_specs=[pl.BlockSpec((tm, MLA_Q_LORA), row),
                  pl.BlockSpec((tm, MLA_KV_LORA), row),
                  pl.BlockSpec((tm, LANE), row),
                  pl.BlockSpec((tm, LANE), tab),
                  pl.BlockSpec((tm, LANE), tab),
                  _const_spec((1, MLA_Q_LORA)),
                  _const_spec((1, MLA_KV_LORA)),
                  _const_spec((MLA_Q_LORA, hp)),
                  _const_spec((MLA_KV_LORA, hp)),
                  _const_spec((MLA_KV_LORA, MLA_V))],
        out_specs=[pl.BlockSpec((tm, hp), row),
                   pl.BlockSpec((tm, hp), row),
                   pl.BlockSpec((tm, MLA_V), row),
                   pl.BlockSpec((tm, MLA_KV_LORA), row),
                   pl.BlockSpec((tm, MLA_ROPE), row)],
        out_shape=[jax.ShapeDtypeStruct((m, hp), BF16),
                   jax.ShapeDtypeStruct((m, hp), BF16),
                   jax.ShapeDtypeStruct((m, MLA_V), BF16),
                   jax.ShapeDtypeStruct((m, MLA_KV_LORA), F32),
                   jax.ShapeDtypeStruct((m, MLA_ROPE), F32)],
        compiler_params=_cparams(("parallel",)),
        name="mla_pre",
    )(b_cq, b_ckv, small, ctab, stab, q_norm_g.reshape(1, -1), kv_norm_g.reshape(1, -1),
      w_uq_pad, w_uk_pad, w_uv)


def _flash_kernel(q_ref, k_ref, v_ref, o_ref, m_scr, l_scr, acc_scr, *, bq, bk):
    i = pl.program_id(1)
    n_full = (i * bq) // bk
    n_all = ((i + 1) * bq) // bk
    lane = lax.broadcasted_iota(jnp.int32, (bq, LANE), 1)
    qpos = i * bq + lax.broadcasted_iota(jnp.int32, (bq, bk), 0)
    kcol = lax.broadcasted_iota(jnp.int32, (bq, bk), 1)

    for pair in range(MLA_HEADS // 2):
        vsl = slice(pair * LANE, (pair + 1) * LANE)
        outs = []
        for sub in range(2):
            h = 2 * pair + sub
            hsl = slice(h * HEAD_PAD, (h + 1) * HEAD_PAD)
            q = q_ref[:, hsl]
            m_scr[...] = jnp.full_like(m_scr, -jnp.inf)
            l_scr[...] = jnp.zeros_like(l_scr)
            acc_scr[...] = jnp.zeros_like(acc_scr)

            def step(j, masked):
                k0 = pl.multiple_of(j * bk, bk)
                s = lax.dot_general(q, k_ref[pl.ds(k0, bk), hsl], (((1,), (1,)), ((), ())),
                                    preferred_element_type=F32)
                if masked:
                    s = jnp.where(kcol + j * bk <= qpos, s, -jnp.inf)
                m_prev = m_scr[...]
                m_next = jnp.maximum(m_prev, jnp.max(s, axis=-1, keepdims=True))
                alpha = jnp.exp(m_prev - m_next)
                p = jnp.exp(s - m_next[:, 0:1])
                l_scr[...] = alpha * l_scr[...] + jnp.sum(p, axis=-1, keepdims=True)
                acc_scr[...] = alpha * acc_scr[...] + jnp.dot(
                    p.astype(BF16), v_ref[pl.ds(k0, bk), vsl], preferred_element_type=F32)
                m_scr[...] = m_next

            def body_full(j, carry):
                step(j, False)
                return carry

            def body_diag(j, carry):
                step(j, True)
                return carry

            lax.fori_loop(0, n_full, body_full, 0)
            lax.fori_loop(n_full, n_all, body_diag, 0)
            outs.append(acc_scr[...] / l_scr[...])
        o_ref[:, vsl] = jnp.where(lane < MLA_VDIM, outs[0], outs[1]).astype(o_ref.dtype)


def _flash(q, k, v, bsz, seq, bq, bk):
    nq = seq // bq
    hp = MLA_HEADS * HEAD_PAD
    return pl.pallas_call(
        functools.partial(_flash_kernel, bq=bq, bk=bk),
        grid=(bsz, nq),
        in_specs=[pl.BlockSpec((bq, hp), lambda b, i: (b * nq + i, 0)),
                  pl.BlockSpec((seq, hp), lambda b, i: (b, 0)),
                  pl.BlockSpec((seq, MLA_V), lambda b, i: (b, 0))],
        out_specs=pl.BlockSpec((bq, MLA_V), lambda b, i: (b * nq + i, 0)),
        out_shape=jax.ShapeDtypeStruct((bsz * seq, MLA_V), BF16),
        scratch_shapes=[pltpu.VMEM((bq, LANE), F32), pltpu.VMEM((bq, LANE), F32),
                        pltpu.VMEM((bq, LANE), F32)],
        compiler_params=_cparams(("parallel", "arbitrary")),
        name="mla_flash",
    )(q, k, v)


def _merge_kernel(x_ref, oa_ref, ob_ref, oc_ref, gates_ref, wa_ref, wb_ref, wc_ref, wo_ref, y_ref):
    g = gates_ref
    merged = (_sigmoid(g[:, 0:D_MODEL].astype(F32)) * _dot(oa_ref[...], wa_ref[...])
              + _sigmoid(g[:, D_MODEL:2 * D_MODEL].astype(F32)) * _dot(ob_ref[...], wb_ref[...])
              + _sigmoid(g[:, 2 * D_MODEL:].astype(F32)) * _dot(oc_ref[...], wc_ref[...]))
    y_ref[...] = x_ref[...] + _dot(merged, wo_ref[...])


def _merge(x, o_a, o_b, o_c, gates, w_a, w_b, w_c, w_o, tm):
    m = x.shape[0]
    row = lambda i: (i, 0)
    return pl.pallas_call(
        _merge_kernel,
        grid=(m // tm,),
        in_specs=[pl.BlockSpec((tm, D_MODEL), row),
                  pl.BlockSpec((tm, GDN_V), row),
                  pl.BlockSpec((tm, MLA_V), row),
                  pl.BlockSpec((tm, ML_V), row),
                  pl.BlockSpec((tm, 3 * D_MODEL), row),
                  _const_spec((GDN_V, D_MODEL)),
                  _const_spec((MLA_V, D_MODEL)),
                  _const_spec((ML_V, D_MODEL)),
                  _const_spec((D_MODEL, D_MODEL))],
        out_specs=pl.BlockSpec((tm, D_MODEL), row),
        out_shape=jax.ShapeDtypeStruct((m, D_MODEL), F32),
        compiler_params=_cparams(("parallel",)),
        name="merge",
    )(x, o_a, o_b, o_c, gates, w_a, w_b, w_c, w_o)


def _ffn_prompt_kernel(x_ref, g_ref, wup_ref, cw_ref, cb_ref, wdn_ref, gf_ref,
                       y_ref, conv_out_ref, gp_scr, *, tm, final):
    t = pl.program_id(1)

    @pl.when(t == 0)
    def _():
        gp_scr[0:8, :] = jnp.zeros((8, D_FF), F32)

    x = x_ref[...]
    hn = _rms(x, g_ref[...]).astype(BF16)
    u = jnp.dot(hn, wup_ref[:, 0:D_FF], preferred_element_type=F32)
    gt = jnp.dot(hn, wup_ref[:, D_FF:], preferred_element_type=F32)
    gp_scr[8:8 + tm, :] = gt
    cw = cw_ref[...]
    conv = gt * cw[2:3] + gp_scr[7:7 + tm, :] * cw[1:2] + gp_scr[6:6 + tm, :] * cw[0:1]
    gp_scr[0:8, :] = gt[tm - 8:tm, :]
    act = _silu(conv + cb_ref[...]) * u
    y = x + _dot(act, wdn_ref[...])
    if final:
        y = _rms(y, gf_ref[...])
    y_ref[...] = y

    @pl.when(t == pl.num_programs(1) - 1)
    def _():
        conv_out_ref[0] = gt[tm - (FFN_CONV - 1):tm, :]


def _ffn_prompt(x, norm_g, w_up, conv_w, conv_b, w_down, final_g, bsz, seq, tm, final):
    nt = seq // tm
    row = lambda b, t: (b * nt + t, 0)
    return pl.pallas_call(
        functools.partial(_ffn_prompt_kernel, tm=tm, final=final),
        grid=(bsz, nt),
        in_specs=[pl.BlockSpec((tm, D_MODEL), row),
                  _const_spec((1, D_MODEL)),
                  _const_spec((D_MODEL, 2 * D_FF)),
                  _const_spec((FFN_CONV, D_FF)),
                  _const_spec((1, D_FF)),
                  _const_spec((D_FF, D_MODEL)),
                  _const_spec((1, D_MODEL))],
        out_specs=[pl.BlockSpec((tm, D_MODEL), row),
                   pl.BlockSpec((1, FFN_CONV - 1, D_FF), lambda b, t: (b, 0, 0))],
        out_shape=[jax.ShapeDtypeStruct((bsz * seq, D_MODEL), F32),
                   jax.ShapeDtypeStruct((bsz, FFN_CONV - 1, D_FF), F32)],
        scratch_shapes=[pltpu.VMEM((tm + 8, D_FF), F32)],
        compiler_params=_cparams(("parallel", "arbitrary")),
        name="ffn_prompt",
    )(x, norm_g.reshape(1, -1), w_up, conv_w, conv_b.reshape(1, -1), w_down, final_g.reshape(1, -1))


def _ffn_sample_kernel(x_ref, hist_ref, g_ref, wup_ref, cw_ref, cb_ref, wdn_ref, gf_ref,
                       y_ref, hist_out_ref, *, final):
    x = x_ref[...]
    hn = _rms(x, g_ref[...]).astype(BF16)
    u = jnp.dot(hn, wup_ref[:, 0:D_FF], preferred_element_type=F32)
    gt = jnp.dot(hn, wup_ref[:, D_FF:], preferred_element_type=F32)
    cw = cw_ref[...]
    conv = hist_ref[0] * cw[0:1] + hist_ref[1] * cw[1:2] + gt * cw[2:3]
    act = _silu(conv + cb_ref[...]) * u
    y = x + _dot(act, wdn_ref[...])
    if final:
        y = _rms(y, gf_ref[...])
    y_ref[...] = y
    hist_out_ref[0] = hist_ref[1]
    hist_out_ref[1] = gt


def _ffn_sample(x, hist_t, norm_g, w_up, conv_w, conv_b, w_down, final_g, final):
    bs = x.shape[0]
    return pl.pallas_call(
        functools.partial(_ffn_sample_kernel, final=final),
        out_shape=[jax.ShapeDtypeStruct((bs, D_MODEL), F32),
                   jax.ShapeDtypeStruct((FFN_CONV - 1, bs, D_FF), F32)],
        compiler_params=pltpu.CompilerParams(vmem_limit_bytes=VMEM_LIMIT),
        name="ffn_sample",
    )(x, hist_t, norm_g.reshape(1, -1), w_up, conv_w, conv_b.reshape(1, -1), w_down, final_g.reshape(1, -1))


def _row_to_col(row, eye):
    n = eye.shape[0]
    return jnp.sum(jnp.where(eye, jnp.broadcast_to(row, (n, n)), 0.0), axis=1, keepdims=True)


def _rec_sample_kernel(aqkv_ref, hist_ref, az_ref, small_ref, convw_ref, alog_ref, dtb_ref, gng_ref,
                       s0_ref, cqkv_ref, co_ref, bi_ref, bf_ref, mng_ref, c0_ref, n0_ref, m0_ref,
                       oa_ref, hist_out_ref, s_out_ref, oc_ref, c_out_ref, n_out_ref, m_out_ref):
    _, _, eye128 = _tri_masks(LANE)
    lane = lax.broadcasted_iota(jnp.int32, (1, LANE), 1)
    sm = small_ref[0]
    x = aqkv_ref[0]
    hist = hist_ref[0]
    w = convw_ref[...]
    y = _silu(hist[0:1] * w[0:1] + hist[1:2] * w[1:2] + hist[2:3] * w[2:3] + x * w[3:4])
    hist_out_ref[0, 0:1, :] = hist[1:2]
    hist_out_ref[0, 1:2, :] = hist[2:3]
    hist_out_ref[0, 2:3, :] = x
    beta_row = _sigmoid(sm)
    decay_row = jnp.exp(-jnp.exp(alog_ref[...]) * _softplus(sm + dtb_ref[...]))
    gng = gng_ref[...]
    az = az_ref[0]
    for h in range(GDN_HEADS):
        q_row = _l2(y[:, h * GDN_DK:(h + 1) * GDN_DK]) * (GDN_DK ** -0.5)
        k_row = _l2(y[:, GDN_QK + h * GDN_DK:GDN_QK + (h + 1) * GDN_DK])
        v_row = y[:, 2 * GDN_QK + h * GDN_DV:2 * GDN_QK + (h + 1) * GDN_DV]
        beta = beta_row[:, SM_BETA + h:SM_BETA + h + 1]
        a = decay_row[:, SM_ALPHA + h:SM_ALPHA + h + 1]
        k_col = _row_to_col(k_row, eye128)
        q_col = _row_to_col(q_row, eye128)
        s_old = s0_ref[0, h]
        ks = jnp.sum(k_col * s_old, axis=0, keepdims=True)
        v_new = beta * (v_row - a * ks)
        qs = jnp.sum(q_col * s_old, axis=0, keepdims=True)
        qk = jnp.sum(q_row * k_row, axis=-1, keepdims=True)
        o = a * qs + qk * v_new
        s_out_ref[0, h] = a * s_old + k_col * v_new
        z = az[:, h * GDN_DV:(h + 1) * GDN_DV]
        oa_ref[0, :, h * GDN_DV:(h + 1) * GDN_DV] = _rms(o, gng) * _silu(z)

    cq = cqkv_ref[0]
    co = co_ref[0]
    ip_row = sm + bi_ref[...]
    lf_row = -_softplus(-(sm + bf_ref[...]))
    mng = mng_ref[...]
    m_prev = m0_ref[0]
    m_out = jnp.zeros((1, LANE), F32)
    for h in range(ML_HEADS):
        q_row = cq[:, h * LANE:(h + 1) * LANE]
        k_row = cq[:, (ML_HEADS + h) * LANE:(ML_HEADS + h + 1) * LANE] * (ML_DK ** -0.5)
        v_row = cq[:, 2 * ML_HEADS * LANE + h * ML_DV:2 * ML_HEADS * LANE + (h + 1) * ML_DV]
        ip = ip_row[:, SM_I + h:SM_I + h + 1]
        lf = lf_row[:, SM_F + h:SM_F + h + 1]
        m_old = m_prev[:, h:h + 1]
        c_old = c0_ref[0, h]
        n_old = n0_ref[0, h:h + 1, :]
        k_col = _row_to_col(k_row, eye128)[0:ML_DK]
        q_col = _row_to_col(q_row, eye128)[0:ML_DK]
        b_log = lf + m_old
        m_r = jnp.maximum(b_log, ip)
        inter = jnp.exp(b_log - m_r)
        w_key = jnp.exp(ip - m_r)
        qk = jnp.sum(q_row * k_row, axis=-1, keepdims=True)
        p = w_key * qk
        num = inter * jnp.sum(q_col * c_old, axis=0, keepdims=True) + p * v_row
        den = inter * jnp.sum(q_row * n_old, axis=-1, keepdims=True) + p
        h_t = num / jnp.maximum(jnp.abs(den), jnp.exp(-m_r))
        c_out_ref[0, h] = inter * c_old + (w_key * k_col) * v_row
        n_out_ref[0, h:h + 1, :] = inter * n_old + w_key * k_row
        m_out = jnp.where(lane == h, m_r, m_out)
        og = _sigmoid(co[:, h * ML_DV:(h + 1) * ML_DV])
        oc_ref[0, :, h * ML_DV:(h + 1) * ML_DV] = _rms(og * h_t, mng)
    m_out_ref[0] = m_out


def _rec_sample(a_qkv, hist, a_z, small, conv_w, alog_row, dtb_row, gdn_norm_g, s0,
                c_qkv, c_o, bi_row, bf_row, ml_norm_g, c0, n0_pad, m0_pad):
    bs = a_qkv.shape[0]
    r3 = lambda a: a.reshape(bs, 1, a.shape[-1])
    b3 = lambda n: pl.BlockSpec((1, 1, n), lambda b: (b, 0, 0))
    mlw = 2 * ML_HEADS * LANE + ML_V
    return pl.pallas_call(
        _rec_sample_kernel,
        grid=(bs,),
        in_specs=[b3(GDN_QKV),
                  pl.BlockSpec((1, GDN_CONV - 1, GDN_QKV), lambda b: (b, 0, 0)),
                  b3(GDN_V), b3(LANE),
                  _const_spec((GDN_CONV, GDN_QKV)),
                  _const_spec((1, LANE)), _const_spec((1, LANE)), _const_spec((1, GDN_DV)),
                  pl.BlockSpec((1, GDN_HEADS, GDN_DK, GDN_DV), lambda b: (b, 0, 0, 0)),
                  b3(mlw), b3(ML_V),
                  _const_spec((1, LANE)), _const_spec((1, LANE)), _const_spec((1, ML_DV)),
                  pl.BlockSpec((1, ML_HEADS, ML_DK, ML_DV), lambda b: (b, 0, 0, 0)),
                  pl.BlockSpec((1, ML_HEADS, LANE), lambda b: (b, 0, 0)),
                  b3(LANE)],
        out_specs=[b3(GDN_V),
                   pl.BlockSpec((1, GDN_CONV - 1, GDN_QKV), lambda b: (b, 0, 0)),
                   pl.BlockSpec((1, GDN_HEADS, GDN_DK, GDN_DV), lambda b: (b, 0, 0, 0)),
                   b3(ML_V),
                   pl.BlockSpec((1, ML_HEADS, ML_DK, ML_DV), lambda b: (b, 0, 0, 0)),
                   pl.BlockSpec((1, ML_HEADS, LANE), lambda b: (b, 0, 0)),
                   b3(LANE)],
        out_shape=[jax.ShapeDtypeStruct((bs, 1, GDN_V), F32),
                   jax.ShapeDtypeStruct((bs, GDN_CONV - 1, GDN_QKV), F32),
                   jax.ShapeDtypeStruct((bs, GDN_HEADS, GDN_DK, GDN_DV), F32),
                   jax.ShapeDtypeStruct((bs, 1, ML_V), F32),
                   jax.ShapeDtypeStruct((bs, ML_HEADS, ML_DK, ML_DV), F32),
                   jax.ShapeDtypeStruct((bs, ML_HEADS, LANE), F32),
                   jax.ShapeDtypeStruct((bs, 1, LANE), F32)],
        compiler_params=_cparams(("parallel",)),
        name="rec_sample",
    )(r3(a_qkv), hist, r3(a_z), r3(small), conv_w, alog_row, dtb_row, gdn_norm_g.reshape(1, -1), s0,
      r3(c_qkv), r3(c_o), bi_row, bf_row, ml_norm_g.reshape(1, -1), c0, n0_pad, r3(m0_pad))


def _mla_sample_pre_kernel(cq_ref, ckv_ref, small_ref, ctab_ref, stab_ref, qg_ref, kvg_ref,
                           wuq_ref, wukt_ref, qlat_ref, qrope_ref, ckv_out_ref, kr_out_ref):
    ctab = ctab_ref[...]
    stab = stab_ref[...]
    lane = lax.broadcasted_iota(jnp.int32, (1, LANE), 1)
    in_rope = (lane >= MLA_NOPE) & (lane < MLA_QHEAD)
    sm = small_ref[...]
    kr_out_ref[...] = jnp.where(in_rope, sm * ctab + _rope_swap(sm, MLA_NOPE) * stab, 0.0)
    ckv_out_ref[...] = _rms(ckv_ref[...], kvg_ref[...])
    qn = _rms(cq_ref[...], qg_ref[...])
    q = _dot(qn, wuq_ref[...])
    q_sw = _rope_swap(q, MLA_NOPE)
    for h in range(MLA_HEADS):
        sl = slice(h * HEAD_PAD, (h + 1) * HEAD_PAD)
        qh = q[:, sl]
        qrope_ref[:, sl] = jnp.where(in_rope, qh * ctab + q_sw[:, sl] * stab, 0.0)
        qlat_ref[:, sl] = _dot(jnp.where(lane < MLA_NOPE, qh, 0.0), wukt_ref[h])


def _mla_sample_pre(b_cq, b_ckv, small, ctab, stab, q_norm_g, kv_norm_g, w_uq_pad, w_uk_t):
    bs = b_cq.shape[0]
    hp = MLA_HEADS * HEAD_PAD
    return pl.pallas_call(
        _mla_sample_pre_kernel,
        out_shape=[jax.ShapeDtypeStruct((bs, hp), F32),
                   jax.ShapeDtypeStruct((bs, hp), F32),
                   jax.ShapeDtypeStruct((bs, MLA_KV_LORA), F32),
                   jax.ShapeDtypeStruct((bs, LANE), F32)],
        compiler_params=pltpu.CompilerParams(vmem_limit_bytes=VMEM_LIMIT),
        name="mla_sample_pre",
    )(b_cq, b_ckv, small, ctab, stab, q_norm_g.reshape(1, -1), kv_norm_g.reshape(1, -1), w_uq_pad, w_uk_t)


def _mla_sample_kernel(pt_ref, qlat_ref, qrope_ref, ckv_ref, kr_ref, *rest, group):
    lat_refs = rest[:group]
    kr_refs = rest[group:2 * group]
    ctx_ref = rest[2 * group]
    m_scr, l_scr, acc_scr = rest[2 * group + 1:]
    j = pl.program_id(1)

    @pl.when(j == 0)
    def _():
        m_scr[...] = jnp.full_like(m_scr, -jnp.inf)
        l_scr[...] = jnp.zeros_like(l_scr)
        acc_scr[...] = jnp.zeros_like(acc_scr)

    q_lat = qlat_ref[0].astype(BF16)
    q_rope = qrope_ref[0][:, MLA_NOPE:MLA_QHEAD].astype(BF16)
    lats = [r[...].astype(BF16) for r in lat_refs]
    s = jnp.concatenate(
        [_dot_nt(q_lat, lats[g]) + _dot_nt(q_rope, kr_refs[g][...]) for g in range(group)],
        axis=1) * MLA_SCALE
    m_prev = m_scr[...]
    m_next = jnp.maximum(m_prev, jnp.max(s, axis=-1, keepdims=True))
    alpha = jnp.exp(m_prev - m_next)
    p = jnp.exp(s - m_next[:, 0:1])
    l_scr[...] = alpha * l_scr[...] + jnp.sum(p, axis=-1, keepdims=True)
    pv = _dot(p[:, 0:PAGE_SIZE], lats[0])
    for g in range(1, group):
        pv = pv + _dot(p[:, g * PAGE_SIZE:(g + 1) * PAGE_SIZE], lats[g])
    acc_scr[...] = alpha * acc_scr[...] + pv
    m_scr[...] = m_next

    @pl.when(j == pl.num_programs(1) - 1)
    def _():
        c_new = ckv_ref[0].astype(BF16).astype(F32)
        kr_new = kr_ref[0][:, MLA_NOPE:MLA_QHEAD].astype(BF16).astype(F32)
        s_new = (jnp.sum(q_lat.astype(F32) * c_new, axis=-1, keepdims=True)
                 + jnp.sum(q_rope.astype(F32) * kr_new, axis=-1, keepdims=True)) * MLA_SCALE
        m_fin = jnp.maximum(m_scr[...], s_new)
        a2 = jnp.exp(m_scr[...] - m_fin)
        p_new = jnp.exp(s_new - m_fin[:, 0:1])
        l_fin = a2 * l_scr[...] + p_new
        acc = a2 * acc_scr[...] + p_new.astype(BF16).astype(F32) * c_new
        ctx_ref[0] = acc / l_fin


def _mla_sample(page_table, q_lat, q_rope, c_kv, kr, cache_lat, cache_kr, layer, group):
    bs, n_pages = page_table.shape
    assert n_pages % group == 0
    hp3 = lambda a: a.reshape(bs, MLA_HEADS, HEAD_PAD)

    def page_spec(width, g):
        return pl.BlockSpec((None, None, PAGE_SIZE, width),
                            lambda b, j, pt: (layer, pt[b, j * group + g], 0, 0))

    per_b = lambda n: pl.BlockSpec((1, n, LANE), lambda b, j, pt: (b, 0, 0))
    grid_spec = pltpu.PrefetchScalarGridSpec(
        num_scalar_prefetch=1,
        grid=(bs, n_pages // group),
        in_specs=([per_b(MLA_HEADS), per_b(MLA_HEADS), per_b(1), per_b(1)]
                  + [page_spec(MLA_KV_LORA, g) for g in range(group)]
                  + [page_spec(MLA_ROPE, g) for g in range(group)]),
        out_specs=per_b(MLA_HEADS),
        scratch_shapes=[pltpu.VMEM((MLA_HEADS, LANE), F32), pltpu.VMEM((MLA_HEADS, LANE), F32),
                        pltpu.VMEM((MLA_HEADS, LANE), F32)],
    )
    return pl.pallas_call(
        functools.partial(_mla_sample_kernel, group=group),
        grid_spec=grid_spec,
        out_shape=jax.ShapeDtypeStruct((bs, MLA_HEADS, LANE), F32),
        compiler_params=_cparams(("parallel", "arbitrary")),
        name="mla_sample",
    )(page_table, hp3(q_lat), hp3(q_rope), c_kv.reshape(bs, 1, LANE), kr.reshape(bs, 1, LANE),
      *([cache_lat] * group), *([cache_kr] * group))


def _ctx_up_kernel(ctx_ref, wuv_ref, o_ref):
    o_ref[...] = _dot(ctx_ref[...], wuv_ref[...])


def _ctx_up(ctx, w_uv_bd):
    return pl.pallas_call(
        _ctx_up_kernel,
        out_shape=jax.ShapeDtypeStruct((ctx.shape[0], MLA_V), F32),
        name="ctx_up",
    )(ctx, w_uv_bd)


def _lane_row(vals, lane0):
    return jnp.zeros((1, LANE), F32).at[0, lane0:lane0 + vals.shape[0]].set(vals.astype(F32))


def _pick(n, pref):
    t = min(n, pref)
    while n % t:
        t //= 2
    return t


def kernel(x_prompt, x_sample, cache_kv_latent, cache_k_rope, state_gdn_conv, state_gdn_S,
           state_mlstm_C, state_mlstm_n, state_mlstm_m, state_ffn_conv, page_table,
           norm1_g, w_in, gdn_conv_w, gdn_A_log, gdn_dt_bias, gdn_norm_g,
           mla_q_norm_g, mla_w_uq, mla_kv_norm_g, mla_w_uk, mla_w_uv,
           ml_b_i, ml_b_f, ml_norm_g, w_branch_a, w_branch_b, w_branch_c, w_out,
           norm2_g, ffn_w_up, ffn_conv_w, ffn_conv_b, ffn_w_down, final_norm_g):
    bp, seq = x_prompt.shape[:2]
    bs, dec = x_sample.shape[:2]
    assert dec == 1 and seq % CHUNK == 0
    depth = w_in.shape[0]
    n_pages = page_table.shape[1]
    past_len = n_pages * PAGE_SIZE
    mp = bp * seq

    tm = _pick(seq, 512)
    tb = _pick(seq, 256)
    bq = _pick(seq, 256)
    group = _pick(n_pages, 16)

    ctab_p, stab_p = _rope_tables(jnp.arange(seq))
    ctab_s, stab_s = _rope_tables(jnp.full((1,), past_len))

    xp = x_prompt.reshape(mp, D_MODEL)
    xs = x_sample.reshape(bs, D_MODEL)
    out_p = [[] for _ in range(8)]
    out_s = [[] for _ in range(8)]

    for l in range(depth):
        last = l == depth - 1
        w_p = _pack_w_in(w_in[l], False)
        w_s = _pack_w_in(w_in[l], True)
        w_uq_pad = _pad_heads(mla_w_uq[l], MLA_QHEAD).astype(BF16)
        w_uk_pad = _pad_heads(mla_w_uk[l], MLA_NOPE).astype(BF16)
        w_uv = mla_w_uv[l].astype(BF16)
        w_a, w_b, w_c, w_o = (w_branch_a[l].astype(BF16), w_branch_b[l].astype(BF16),
                              w_branch_c[l].astype(BF16), w_out[l].astype(BF16))
        w_up, w_dn = ffn_w_up[l].astype(BF16), ffn_w_down[l].astype(BF16)

        a_qkv, a_z, b_cq, b_ckv, c_qkv, c_o, gates, small = _in_proj(xp, norm1_g[l], w_p, PROMPT_GROUPS, tm)
        o_a, p_gconv, p_gs = _gdn_prompt(a_qkv, a_z, small, gdn_conv_w[l], gdn_A_log[l], gdn_dt_bias[l],
                                         gdn_norm_g[l], bp, seq, tb)
        q_pad, k_pad, v_all, p_ckv, p_kr = _mla_pre(b_cq, b_ckv, small, ctab_p, stab_p, mla_q_norm_g[l],
                                                    mla_kv_norm_g[l], w_uq_pad, w_uk_pad, w_uv, seq, tm)
        o_b = _flash(q_pad, k_pad, v_all, bp, seq, bq, bq)
        o_c, p_c, p_n, p_m = _mlstm_prompt(c_qkv, c_o, small, ml_b_i[l], ml_b_f[l], ml_norm_g[l], bp, seq, tb)
        xp = _merge(xp, o_a, o_b, o_c, gates, w_a, w_b, w_c, w_o, tm)
        xp, p_fconv = _ffn_prompt(xp, norm2_g[l], w_up, ffn_conv_w[l], ffn_conv_b[l], w_dn, final_norm_g,
                                  bp, seq, tm, last)
        for lst, a in zip(out_p, (p_ckv.reshape(bp, seq, MLA_KV_LORA), p_kr.reshape(bp, seq, MLA_ROPE),
                                  p_gconv, p_gs, p_c, p_n, p_m.reshape(bp, ML_HEADS), p_fconv)):
            lst.append(a)

        a_qkv, a_z, b_cq, b_ckv, c_qkv, c_o, gates, small = _in_proj(xs, norm1_g[l], w_s, SAMPLE_GROUPS, bs)
        n0_pad = jnp.pad(state_mlstm_n[l], ((0, 0), (0, 0), (0, LANE - ML_DK)))
        m0_pad = jnp.pad(state_mlstm_m[l], ((0, 0), (0, LANE - ML_HEADS)))
        o_a, s_gconv, s_gs, o_c, s_c, s_n, s_m = _rec_sample(
            a_qkv, state_gdn_conv[l], a_z, small, gdn_conv_w[l],
            _lane_row(gdn_A_log[l], SM_ALPHA), _lane_row(gdn_dt_bias[l], SM_ALPHA), gdn_norm_g[l],
            state_gdn_S[l], c_qkv, c_o, _lane_row(ml_b_i[l], SM_I), _lane_row(ml_b_f[l], SM_F),
            ml_norm_g[l], state_mlstm_C[l], n0_pad, m0_pad)
        w_uk_t = jnp.pad(jnp.transpose(mla_w_uk[l].reshape(MLA_KV_LORA, MLA_HEADS, MLA_NOPE), (1, 2, 0)),
                         ((0, 0), (0, HEAD_PAD - MLA_NOPE), (0, 0))).astype(BF16)
        q_lat, q_rope, s_ckv, kr_s = _mla_sample_pre(b_cq, b_ckv, small, ctab_s, stab_s, mla_q_norm_g[l],
                                                      mla_kv_norm_g[l], w_uq_pad, w_uk_t)
        ctx = _mla_sample(page_table, q_lat, q_rope, s_ckv, kr_s, cache_kv_latent, cache_k_rope, l, group)
        w_uv_bd = (jnp.eye(MLA_HEADS, dtype=F32)[:, None, :, None]
                   * jnp.transpose(mla_w_uv[l].reshape(MLA_KV_LORA, MLA_HEADS, MLA_VDIM), (1, 0, 2))[:, :, None, :]
                   ).reshape(MLA_HEADS * MLA_KV_LORA, MLA_V).astype(BF16)
        o_b = _ctx_up(ctx.reshape(bs, MLA_HEADS * LANE), w_uv_bd)
        xs = _merge(xs, o_a.reshape(bs, GDN_V), o_b, o_c.reshape(bs, ML_V), gates, w_a, w_b, w_c, w_o, bs)
        xs, s_fconv_t = _ffn_sample(xs, jnp.swapaxes(state_ffn_conv[l], 0, 1), norm2_g[l], w_up, ffn_conv_w[l],
                                    ffn_conv_b[l], w_dn, final_norm_g, last)
        for lst, a in zip(out_s, (s_ckv.reshape(bs, 1, MLA_KV_LORA),
                                  kr_s[:, MLA_NOPE:MLA_QHEAD].reshape(bs, 1, MLA_ROPE),
                                  s_gconv, s_gs, s_c, s_n[:, :, :ML_DK], s_m.reshape(bs, LANE)[:, :ML_HEADS],
                                  jnp.swapaxes(s_fconv_t, 0, 1))):
            lst.append(a)

    y_prompt = xp.reshape(bp, seq, D_MODEL)
    y_sample = xs.reshape(bs, 1, D_MODEL)
    return (y_prompt, y_sample) + tuple(jnp.stack(a) for a in out_p) + tuple(jnp.stack(a) for a in out_s)
```

```python
import functools
import math

import jax
import jax.numpy as jnp
import numpy as np
from jax import lax
from jax.experimental import pallas as pl
from jax.experimental.pallas import tpu as pltpu

F32 = jnp.float32
BF16 = jnp.bfloat16

D_MODEL = 1024
PAGE_SIZE = 128
GDN_HEADS, GDN_DK, GDN_DV, GDN_CONV = 4, 128, 128, 4
MLA_HEADS, MLA_Q_LORA, MLA_KV_LORA, MLA_NOPE, MLA_ROPE, MLA_VDIM = 8, 256, 128, 64, 32, 64
ROPE_BASE = 10000.0
ML_HEADS, ML_DK, ML_DV = 4, 64, 128
CHUNK = 64
REC_CHUNK = 64
D_FF = 2816
FFN_CONV = 3
NORM_EPS = 1e-6

GDN_QK = GDN_HEADS * GDN_DK
GDN_V = GDN_HEADS * GDN_DV
GDN_QKV = 2 * GDN_QK + GDN_V
MLA_QHEAD = MLA_NOPE + MLA_ROPE
MLA_V = MLA_HEADS * MLA_VDIM
MLA_SCALE = MLA_QHEAD ** -0.5
ML_QK = ML_HEADS * ML_DK
ML_V = ML_HEADS * ML_DV
ML_QKV = 2 * ML_QK + ML_V
IN_SIZES = (GDN_QKV, GDN_HEADS, GDN_HEADS, GDN_V, MLA_Q_LORA, MLA_KV_LORA, MLA_ROPE,
            ML_QKV, ML_HEADS, ML_HEADS, ML_V, 3 * D_MODEL)

LANE = 128
HEAD_PAD = 128
SM_BETA, SM_ALPHA, SM_I, SM_F, SM_KR = 0, 8, 16, 24, 64
VMEM_LIMIT = 56 * 1024 * 1024
LOG2E = math.log2(math.e)


def _cparams(sem):
    return pltpu.CompilerParams(dimension_semantics=sem, vmem_limit_bytes=VMEM_LIMIT)


def _const_spec(shape):
    nd = len(shape)
    return pl.BlockSpec(shape, lambda *_: (0,) * nd, pipeline_mode=pl.Buffered(1))


def _dot(a, b):
    return jnp.dot(a.astype(BF16), b.astype(BF16), preferred_element_type=F32)


def _dot_nt(a, b):
    return lax.dot_general(a.astype(BF16), b.astype(BF16), (((1,), (1,)), ((), ())),
                           preferred_element_type=F32)


def _dot_tn(a, b):
    return lax.dot_general(a.astype(BF16), b.astype(BF16), (((0,), (0,)), ((), ())),
                           preferred_element_type=F32)


def _dot01(mask_bf16, x):
    x1 = x.astype(BF16)
    r1 = x - x1.astype(F32)
    x2 = r1.astype(BF16)
    x3 = (r1 - x2.astype(F32)).astype(BF16)
    d = functools.partial(jnp.dot, preferred_element_type=F32)
    return d(mask_bf16, x1) + d(mask_bf16, x2) + d(mask_bf16, x3)


def _dot01_r(x, mask_bf16):
    x1 = x.astype(BF16)
    r1 = x - x1.astype(F32)
    x2 = r1.astype(BF16)
    x3 = (r1 - x2.astype(F32)).astype(BF16)
    d = functools.partial(jnp.dot, preferred_element_type=F32)
    return d(x1, mask_bf16) + d(x2, mask_bf16) + d(x3, mask_bf16)


def _rms(x, g):
    return x * lax.rsqrt(jnp.mean(x * x, axis=-1, keepdims=True) + NORM_EPS) * g


def _l2(x):
    return x * lax.rsqrt(jnp.sum(x * x, axis=-1, keepdims=True) + NORM_EPS)


def _sigmoid(x):
    return jax.nn.sigmoid(x)


def _silu(x):
    return x * jax.nn.sigmoid(x)


def _softplus(x):
    return jnp.maximum(x, 0.0) + jnp.log1p(jnp.exp(-jnp.abs(x)))


def _tri_masks(n):
    r = lax.broadcasted_iota(jnp.int32, (n, n), 0)
    c = lax.broadcasted_iota(jnp.int32, (n, n), 1)
    return r >= c, r > c, r == c


def _chunk_cumsum_mask(tb, c):
    r = lax.broadcasted_iota(jnp.int32, (tb, tb), 0)
    q = lax.broadcasted_iota(jnp.int32, (tb, tb), 1)
    return jnp.where((r // c == q // c) & (r >= q), 1.0, 0.0).astype(BF16)


def _neumann_inverse_many(a_list, eye_f):
    n = eye_f.shape[0]
    ps = [-a for a in a_list]
    ts = [eye_f + p for p in ps]
    qs = [_dot(p, p) for p in ps]
    covered = 2
    while 2 * covered < n:
        rs = [_dot(q, jnp.concatenate([t, q], axis=1)) for t, q in zip(ts, qs)]
        ts = [t + r[:, :n] for t, r in zip(ts, rs)]
        qs = [r[:, n:] for r in rs]
        covered *= 2
    return [t + _dot(q, t) for t, q in zip(ts, qs)]


def _rope_swap(x, lane0):
    w = x.shape[-1]
    half = MLA_ROPE // 2
    lane = lax.broadcasted_iota(jnp.int32, x.shape, x.ndim - 1) % LANE
    left = pltpu.roll(x, w - half, x.ndim - 1)
    right = pltpu.roll(x, half, x.ndim - 1)
    first = (lane >= lane0) & (lane < lane0 + half)
    second = (lane >= lane0 + half) & (lane < lane0 + 2 * half)
    return jnp.where(first, left, jnp.where(second, right, 0.0))


def _in_proj_kernel(x_ref, g_ref, w_ref, *rest, with_small_t):
    xn = _rms(x_ref[...], g_ref[...]).astype(BF16)
    out_refs = rest
    if with_small_t:
        wst_ref, out_refs, small_t_ref = rest[0], rest[1:-1], rest[-1]
        small_t_ref[...] = lax.dot_general(wst_ref[...], xn, (((1,), (1,)), ((), ())), preferred_element_type=F32)
    off = 0
    for ref in out_refs:
        n = ref.shape[-1]
        ref[...] = jnp.dot(xn, w_ref[:, off:off + n], preferred_element_type=F32).astype(ref.dtype)
        off += n


def _in_proj(x, g, w_packed, groups, tm, w_small_t=None):
    m = x.shape[0]
    width = w_packed.shape[1]
    assert sum(n for n, _ in groups) == width and m % tm == 0
    with_t = w_small_t is not None
    in_specs = [pl.BlockSpec((tm, D_MODEL), lambda i: (i, 0)),
                _const_spec((1, D_MODEL)),
                _const_spec((D_MODEL, width))]
    out_specs = [pl.BlockSpec((tm, n), lambda i: (i, 0)) for n, _ in groups]
    out_shape = [jax.ShapeDtypeStruct((m, n), dt) for n, dt in groups]
    args = [x, g.reshape(1, D_MODEL), w_packed]
    if with_t:
        in_specs.append(_const_spec((LANE, D_MODEL)))
        out_specs.append(pl.BlockSpec((LANE, tm), lambda i: (0, i)))
        out_shape.append(jax.ShapeDtypeStruct((LANE, m), F32))
        args.append(w_small_t)
    return pl.pallas_call(
        functools.partial(_in_proj_kernel, with_small_t=with_t),
        grid=(m // tm,),
        in_specs=in_specs,
        out_specs=out_specs,
        out_shape=out_shape,
        compiler_params=_cparams(("parallel",)),
        name="in_proj",
    )(*args)


def _small_cols(a_beta, a_alpha, b_kr, c_i, c_f):
    small = jnp.zeros((D_MODEL, LANE), F32)
    small = small.at[:, SM_BETA:SM_BETA + GDN_HEADS].set(a_beta)
    small = small.at[:, SM_ALPHA:SM_ALPHA + GDN_HEADS].set(a_alpha)
    small = small.at[:, SM_I:SM_I + ML_HEADS].set(c_i)
    small = small.at[:, SM_F:SM_F + ML_HEADS].set(c_f)
    small = small.at[:, SM_KR:SM_KR + MLA_ROPE].set(b_kr)
    return small


PROMPT_GROUPS = ((GDN_QKV, F32), (GDN_V, BF16), (MLA_Q_LORA, BF16), (MLA_KV_LORA, F32),
                 (ML_QKV, BF16), (ML_V, BF16), (3 * D_MODEL, BF16), (LANE, F32))
SAMPLE_GROUPS = ((GDN_QKV, F32), (GDN_V, F32), (MLA_Q_LORA, F32), (MLA_KV_LORA, F32),
                 (2 * ML_HEADS * LANE + ML_V, F32), (ML_V, F32), (3 * D_MODEL, F32), (LANE, F32))


def _pack_w_in(w_in, pad_ml_heads):
    (a_qkv, a_beta, a_alpha, a_z, b_cq, b_ckv, b_kr,
     c_qkv, c_i, c_f, c_o, gates) = jnp.split(w_in, np.cumsum(IN_SIZES)[:-1].tolist(), axis=-1)
    if pad_ml_heads:
        qk = c_qkv[:, :2 * ML_QK].reshape(D_MODEL, 2 * ML_HEADS, ML_DK)
        qk = jnp.pad(qk, ((0, 0), (0, 0), (0, LANE - ML_DK))).reshape(D_MODEL, 2 * ML_HEADS * LANE)
        c_qkv = jnp.concatenate([qk, c_qkv[:, 2 * ML_QK:]], axis=1)
    small = _small_cols(a_beta, a_alpha, b_kr, c_i, c_f)
    packed = jnp.concatenate([a_qkv, a_z, b_cq, b_ckv, c_qkv, c_o, gates, small], axis=1).astype(BF16)
    return packed, small.T.astype(BF16)


def _gdn_prompt_kernel(aqkv_ref, az_ref, small_ref, small_t_ref, convw_ref, alog_ref, dtb_ref,
                       alog_c_ref, dtb_c_ref, ng_ref,
                       o_ref, conv_out_ref, s_out_ref, xp_scr, s_scr, *, tb):
    t = pl.program_id(1)
    c = REC_CHUNK
    nc = tb // c
    heads = range(GDN_HEADS)
    pairs = [(ci, h) for ci in range(nc) for h in heads]

    @pl.when(t == 0)
    def _():
        s_scr[...] = jnp.zeros_like(s_scr)
        xp_scr[0:8, :] = jnp.zeros((8, GDN_QKV), F32)

    x = aqkv_ref[...]
    xp_scr[8:8 + tb, :] = x
    w = convw_ref[...]
    y = (x * w[3:4] + xp_scr[7:7 + tb, :] * w[2:3]
         + xp_scr[6:6 + tb, :] * w[1:2] + xp_scr[5:5 + tb, :] * w[0:1])
    xp_scr[0:8, :] = x[tb - 8:tb, :]
    y = _silu(y)
    sm = small_ref[...]
    beta = _sigmoid(sm[:, SM_BETA:SM_BETA + GDN_HEADS])
    g_c = -jnp.exp(alog_ref[...]) * _softplus(sm[:, SM_ALPHA:SM_ALPHA + GDN_HEADS] + dtb_ref[...])
    g_r = -jnp.exp(alog_c_ref[...]) * _softplus(small_t_ref[SM_ALPHA:SM_ALPHA + GDN_HEADS, :] + dtb_c_ref[...])
    cmask = _chunk_cumsum_mask(tb, c)
    big_c = _dot01(cmask, g_c)
    big_r = _dot01_r(g_r, cmask.T)

    incl, strict, eye = _tri_masks(c)
    eye_f = jnp.where(eye, 1.0, 0.0).astype(F32)
    ng = ng_ref[...]
    rows = lambda ci: slice(ci * c, (ci + 1) * c)

    qs = [_l2(y[rows(ci), h * GDN_DK:(h + 1) * GDN_DK]) * (GDN_DK ** -0.5) for ci, h in pairs]
    ks = [_l2(y[rows(ci), GDN_QK + h * GDN_DK:GDN_QK + (h + 1) * GDN_DK]) for ci, h in pairs]
    vs = [y[rows(ci), 2 * GDN_QK + h * GDN_DV:2 * GDN_QK + (h + 1) * GDN_DV] for ci, h in pairs]
    bcols = [beta[rows(ci), h:h + 1] for ci, h in pairs]
    gcols = [big_c[rows(ci), h:h + 1] for ci, h in pairs]
    decs = [jnp.exp(jnp.where(incl, big_c[rows(ci), h:h + 1] - big_r[h:h + 1, rows(ci)], -jnp.inf))
            for ci, h in pairs]
    qkk = [_dot_nt(jnp.concatenate([q, k], axis=0), k) for q, k in zip(qs, ks)]
    a_list = [b * m[c:] * jnp.where(strict, d, 0.0) for b, m, d in zip(bcols, qkk, decs)]
    tinvs = _neumann_inverse_many(a_list, eye_f)
    e_gs = [jnp.exp(g) for g in gcols]
    uws = [_dot(ti, jnp.concatenate([b * v, (b * e) * k], axis=1))
           for ti, b, v, e, k in zip(tinvs, bcols, vs, e_gs, ks)]
    wqs = [jnp.concatenate([uw[:, GDN_DV:], q * e], axis=0).astype(BF16) for uw, q, e in zip(uws, qs, e_gs)]
    qkd = [(m[:c] * d).astype(BF16) for m, d in zip(qkk, decs)]
    kds = [(k * jnp.exp(g[c - 1:c, :] - g)).astype(BF16) for k, g in zip(ks, gcols)]
    gls = [jnp.exp(jnp.broadcast_to(g[c - 1:c, :], (1, GDN_DV))) for g in gcols]

    s_st = [s_scr[h] for h in heads]
    for ci in range(nc):
        idx = [ci * GDN_HEADS + h for h in heads]
        ws_qs = [jnp.dot(wqs[i], s_st[h].astype(BF16), preferred_element_type=F32)
                 for h, i in zip(heads, idx)]
        v_new = [(uws[i][:, :GDN_DV] - r[:c]).astype(BF16) for i, r in zip(idx, ws_qs)]
        o2 = [jnp.dot(qkd[i], vn, preferred_element_type=F32) for i, vn in zip(idx, v_new)]
        ds = [_dot_tn(kds[i], vn) for i, vn in zip(idx, v_new)]
        for h, i in zip(heads, idx):
            s_st[h] = gls[i] * s_st[h] + ds[h]
            o = ws_qs[h][c:] + o2[h]
            z = az_ref[rows(ci), h * GDN_DV:(h + 1) * GDN_DV].astype(F32)
            o_ref[rows(ci), h * GDN_DV:(h + 1) * GDN_DV] = (_rms(o, ng) * _silu(z)).astype(o_ref.dtype)
    for h in heads:
        s_scr[h] = s_st[h]

    @pl.when(t == pl.num_programs(1) - 1)
    def _():
        conv_out_ref[0] = x[tb - (GDN_CONV - 1):tb, :]
        s_out_ref[0] = s_scr[...]


def _gdn_prompt(a_qkv, a_z, small, small_t, conv_w, a_log, dt_bias, norm_g, bsz, seq, tb):
    nt = seq // tb
    row = lambda b, t: (b * nt + t, 0)
    col = lambda b, t: (0, b * nt + t)
    return pl.pallas_call(
        functools.partial(_gdn_prompt_kernel, tb=tb),
        grid=(bsz, nt),
        in_specs=[pl.BlockSpec((tb, GDN_QKV), row),
                  pl.BlockSpec((tb, GDN_V), row),
                  pl.BlockSpec((tb, LANE), row),
                  pl.BlockSpec((LANE, tb), col),
                  _const_spec((GDN_CONV, GDN_QKV)),
                  _const_spec((1, GDN_HEADS)),
                  _const_spec((1, GDN_HEADS)),
                  _const_spec((GDN_HEADS, 1)),
                  _const_spec((GDN_HEADS, 1)),
                  _const_spec((1, GDN_DV))],
        out_specs=[pl.BlockSpec((tb, GDN_V), row),
                   pl.BlockSpec((1, GDN_CONV - 1, GDN_QKV), lambda b, t: (b, 0, 0)),
                   pl.BlockSpec((1, GDN_HEADS, GDN_DK, GDN_DV), lambda b, t: (b, 0, 0, 0))],
        out_shape=[jax.ShapeDtypeStruct((bsz * seq, GDN_V), BF16),
                   jax.ShapeDtypeStruct((bsz, GDN_CONV - 1, GDN_QKV), F32),
                   jax.ShapeDtypeStruct((bsz, GDN_HEADS, GDN_DK, GDN_DV), F32)],
        scratch_shapes=[pltpu.VMEM((tb + 8, GDN_QKV), F32),
                        pltpu.VMEM((GDN_HEADS, GDN_DK, GDN_DV), F32)],
        compiler_params=_cparams(("parallel", "arbitrary")),
        name="gdn_prompt",
    )(a_qkv, a_z, small, small_t, conv_w, a_log.reshape(1, GDN_HEADS), dt_bias.reshape(1, GDN_HEADS),
      a_log.reshape(GDN_HEADS, 1), dt_bias.reshape(GDN_HEADS, 1), norm_g.reshape(1, GDN_DV))


def _mlstm_prompt_kernel(cqkv_ref, co_ref, small_ref, small_t_ref, bi_ref, bf_ref, bi_c_ref, bf_c_ref, ng_ref,
                         o_ref, c_out_ref, n_out_ref, m_out_ref, cn_scr, m_scr, *, tb):
    t = pl.program_id(1)
    c = REC_CHUNK
    nc = tb // c
    heads = range(ML_HEADS)
    pairs = [(ci, h) for ci in range(nc) for h in heads]

    @pl.when(t == 0)
    def _():
        cn_scr[...] = jnp.zeros_like(cn_scr)
        m_scr[...] = jnp.zeros_like(m_scr)

    sm = small_ref[...]
    ip_c = sm[:, SM_I:SM_I + ML_HEADS] + bi_ref[...]
    lf_c = -_softplus(-(sm[:, SM_F:SM_F + ML_HEADS] + bf_ref[...]))
    ip_r = small_t_ref[SM_I:SM_I + ML_HEADS, :] + bi_c_ref[...]
    lf_r = -_softplus(-(small_t_ref[SM_F:SM_F + ML_HEADS, :] + bf_c_ref[...]))
    cmask = _chunk_cumsum_mask(tb, c)
    f_c = _dot01(cmask, lf_c)
    f_r = _dot01_r(lf_r, cmask.T)

    incl, _, _ = _tri_masks(c)
    ng = ng_ref[...]
    rows = lambda ci: slice(ci * c, (ci + 1) * c)
    ones = jnp.ones((c, ML_DV), BF16)
    rep = lambda col: jnp.broadcast_to(col, (c, ML_DV))

    qs = [cqkv_ref[rows(ci), h * ML_DK:(h + 1) * ML_DK] for ci, h in pairs]
    ks = [cqkv_ref[rows(ci), ML_QK + h * ML_DK:ML_QK + (h + 1) * ML_DK].astype(F32) * (ML_DK ** -0.5)
          for ci, h in pairs]
    v1s = [jnp.concatenate([cqkv_ref[rows(ci), 2 * ML_QK + h * ML_DV:2 * ML_QK + (h + 1) * ML_DV], ones], axis=1)
           for ci, h in pairs]
    fcols = [f_c[rows(ci), h:h + 1] for ci, h in pairs]
    ds = [jnp.where(incl, f_c[rows(ci), h:h + 1] - f_r[h:h + 1, rows(ci)] + ip_r[h:h + 1, rows(ci)], -jnp.inf)
          for ci, h in pairs]
    dmaxs = [jnp.max(d, axis=-1, keepdims=True) for d in ds]
    qks = [_dot_nt(q, k) for q, k in zip(qs, ks)]
    p_locs = [jnp.exp(d - dm) * qk for d, dm, qk in zip(ds, dmaxs, qks)]
    pvs = [_dot(p, v1) for p, v1 in zip(p_locs, v1s)]
    wks = [jnp.exp(f[c - 1:c, :] - f + ip_c[rows(ci), h:h + 1] - dm[c - 1:c, :]) * k
           for (ci, h), f, dm, k in zip(pairs, fcols, dmaxs, ks)]
    kvs = [_dot_tn(wk, v1) for wk, v1 in zip(wks, v1s)]
    fbs = [rep(f) for f in fcols]
    dmbs = [rep(dm) for dm in dmaxs]

    cn_st = [cn_scr[h] for h in heads]
    m_st = [m_scr[h:h + 1, :] for h in heads]
    for ci in range(nc):
        idx = [ci * ML_HEADS + h for h in heads]
        qcs = [_dot(qs[i], cn_st[h]) for h, i in zip(heads, idx)]
        for h, i in zip(heads, idx):
            b_log = fbs[i] + m_st[h]
            m_r = jnp.maximum(b_log, dmbs[i])
            inter = jnp.exp(b_log - m_r)
            sc = jnp.exp(dmbs[i] - m_r)
            num = inter * qcs[h][:, :ML_DV] + sc * pvs[i][:, :ML_DV]
            den = inter * qcs[h][:, ML_DV:] + sc * pvs[i][:, ML_DV:]
            h_t = num / jnp.maximum(jnp.abs(den), jnp.exp(-m_r))
            m_new = m_r[c - 1:c, :]
            decay = jnp.exp(b_log[c - 1:c, :] - m_new)
            ksc = jnp.exp(dmbs[i][c - 1:c, :] - m_new)
            cn_st[h] = (jnp.concatenate([decay, decay], axis=1) * cn_st[h]
                        + jnp.concatenate([ksc, ksc], axis=1) * kvs[i])
            m_st[h] = m_new
            og = _sigmoid(co_ref[rows(ci), h * ML_DV:(h + 1) * ML_DV].astype(F32))
            o_ref[rows(ci), h * ML_DV:(h + 1) * ML_DV] = _rms(og * h_t, ng).astype(o_ref.dtype)
    for h in heads:
        cn_scr[h] = cn_st[h]
        m_scr[h:h + 1, :] = m_st[h]

    @pl.when(t == pl.num_programs(1) - 1)
    def _():
        c_out_ref[0] = cn_scr[:, :, 0:ML_DV]
        n_out_ref[0] = cn_scr[:, :, ML_DV:]
        m_out_ref[0] = m_scr[...]


def _mlstm_prompt(c_qkv, c_o, small, small_t, b_i, b_f, norm_g, bsz, seq, tb):
    nt = seq // tb
    row = lambda b, t: (b * nt + t, 0)
    col = lambda b, t: (0, b * nt + t)
    return pl.pallas_call(
        functools.partial(_mlstm_prompt_kernel, tb=tb),
        grid=(bsz, nt),
        in_specs=[pl.BlockSpec((tb, ML_QKV), row),
                  pl.BlockSpec((tb, ML_V), row),
                  pl.BlockSpec((tb, LANE), row),
                  pl.BlockSpec((LANE, tb), col),
                  _const_spec((1, ML_HEADS)),
                  _const_spec((1, ML_HEADS)),
                  _const_spec((ML_HEADS, 1)),
                  _const_spec((ML_HEADS, 1)),
                  _const_spec((1, ML_DV))],
        out_specs=[pl.BlockSpec((tb, ML_V), row),
                   pl.BlockSpec((1, ML_HEADS, ML_DK, ML_DV), lambda b, t: (b, 0, 0, 0)),
                   pl.BlockSpec((1, ML_HEADS, ML_DK, ML_DV), lambda b, t: (b, 0, 0, 0)),
                   pl.BlockSpec((1, ML_HEADS, LANE), lambda b, t: (b, 0, 0))],
        out_shape=[jax.ShapeDtypeStruct((bsz * seq, ML_V), BF16),
                   jax.ShapeDtypeStruct((bsz, ML_HEADS, ML_DK, ML_DV), F32),
                   jax.ShapeDtypeStruct((bsz, ML_HEADS, ML_DK, ML_DV), F32),
                   jax.ShapeDtypeStruct((bsz, ML_HEADS, LANE), F32)],
        scratch_shapes=[pltpu.VMEM((ML_HEADS, ML_DK, 2 * ML_DV), F32),
                        pltpu.VMEM((ML_HEADS, LANE), F32)],
        compiler_params=_cparams(("parallel", "arbitrary")),
        name="mlstm_prompt",
    )(c_qkv, c_o, small, small_t, b_i.reshape(1, ML_HEADS), b_f.reshape(1, ML_HEADS),
      b_i.reshape(ML_HEADS, 1), b_f.reshape(ML_HEADS, 1), norm_g.reshape(1, ML_DV))


def _rope_tables(pos):
    half = MLA_ROPE // 2
    inv_freq = ROPE_BASE ** (-jnp.arange(half, dtype=F32) / half)
    ang = pos.astype(F32)[:, None] * inv_freq[None, :]
    cos, sin = jnp.cos(ang), jnp.sin(ang)
    n = pos.shape[0]
    ones = jnp.ones((n, MLA_NOPE), F32)
    zeros = jnp.zeros((n, MLA_NOPE), F32)
    pad = jnp.zeros((n, LANE - MLA_QHEAD), F32)
    ctab = jnp.concatenate([ones, cos, cos, pad], axis=1)
    stab = jnp.concatenate([zeros, -sin, sin, pad], axis=1)
    return ctab, stab


def _mla_pre_kernel(cq_ref, ckv_ref, small_ref, ctab_ref, stab_ref, qg_ref, kvg_ref,
                    wuq_ref, wuk_ref, wuv_ref,
                    q_ref, k_ref, v_ref, ckv_out_ref, kr_out_ref):
    ctab = ctab_ref[...]
    stab = stab_ref[...]
    lane = lax.broadcasted_iota(jnp.int32, ctab.shape, 1)
    in_rope = (lane >= MLA_NOPE) & (lane < MLA_QHEAD)
    sm = small_ref[...]
    kr = jnp.where(in_rope, sm * ctab + _rope_swap(sm, MLA_NOPE) * stab, 0.0)
    kr_out_ref[0] = kr.T[MLA_NOPE:MLA_QHEAD, :]
    c_kv = _rms(ckv_ref[...], kvg_ref[...])
    ckv_out_ref[...] = c_kv
    c_kv_b = c_kv.astype(BF16)
    k_nope = jnp.dot(c_kv_b, wuk_ref[...], preferred_element_type=F32)
    v = jnp.dot(c_kv_b, wuv_ref[...], preferred_element_type=F32)
    for pair in range(MLA_HEADS // 2):
        slab = v[:, pair * LANE:(pair + 1) * LANE]
        v_ref[:, (2 * pair) * LANE:(2 * pair + 1) * LANE] = jnp.where(lane < MLA_VDIM, slab, 1.0).astype(v_ref.dtype)
        v_ref[:, (2 * pair + 1) * LANE:(2 * pair + 2) * LANE] = jnp.where(lane < MLA_VDIM, 1.0, slab).astype(v_ref.dtype)
    qn = _rms(cq_ref[...].astype(F32), qg_ref[...]).astype(BF16)
    q = jnp.dot(qn, wuq_ref[...], preferred_element_type=F32)
    q_sw = _rope_swap(q, MLA_NOPE)
    for h in range(MLA_HEADS):
        sl = slice(h * HEAD_PAD, (h + 1) * HEAD_PAD)
        q_ref[:, sl] = ((q[:, sl] * ctab + q_sw[:, sl] * stab) * (MLA_SCALE * LOG2E)).astype(q_ref.dtype)
        k_ref[:, sl] = (k_nope[:, sl] + kr).astype(k_ref.dtype)


def _pad_heads(w, head_dim):
    rows = w.shape[0]
    w = w.reshape(rows, MLA_HEADS, head_dim)
    return jnp.pad(w, ((0, 0), (0, 0), (0, HEAD_PAD - head_dim))).reshape(rows, MLA_HEADS * HEAD_PAD)


def _mla_pre(b_cq, b_ckv, small, ctab, stab, q_norm_g, kv_norm_g, w_uq_pad, w_uk_pad, w_uv, seq, tm):
    m = b_cq.shape[0]
    nt = seq // tm
    row = lambda i: (i, 0)
    tab = lambda i: (i % nt, 0)
    hp = MLA_HEADS * HEAD_PAD
    return pl.pallas_call(
        _mla_pre_kernel,
        grid=(m // tm,),
        in_specs=[pl.BlockSpec((tm, MLA_Q_LORA), row),
                  pl.BlockSpec((tm, MLA_KV_LORA), row),
                  pl.BlockSpec((tm, LANE), row),
                  pl.BlockSpec((tm, LANE), tab),
                  pl.BlockSpec((tm, LANE), tab),
                  _const_spec((1, MLA_Q_LORA)),
                  _const_spec((1, MLA_KV_LORA)),
                  _const_spec((MLA_Q_LORA, hp)),
                  _const_spec((MLA_KV_LORA, hp)),
                  _const_spec((MLA_KV_LORA, MLA_V))],
        out_specs=[pl.BlockSpec((tm, hp), row),
                   pl.BlockSpec((tm, hp), row),
                   pl.BlockSpec((tm, hp), row),
                   pl.BlockSpec((tm, MLA_KV_LORA), row),
                   pl.BlockSpec((1, MLA_ROPE, tm), lambda i: (i // nt, 0, i % nt))],
        out_shape=[jax.ShapeDtypeStruct((m, hp), BF16),
                   jax.ShapeDtypeStruct((m, hp), BF16),
                   jax.ShapeDtypeStruct((m, hp), BF16),
                   jax.ShapeDtypeStruct((m, MLA_KV_LORA), F32),
                   jax.ShapeDtypeStruct((m // seq, MLA_ROPE, seq), F32)],
        compiler_params=_cparams(("parallel",)),
        name="mla_pre",
    )(b_cq, b_ckv, small, ctab, stab, q_norm_g.reshape(1, -1), kv_norm_g.reshape(1, -1),
      w_uq_pad, w_uk_pad, w_uv)


def _flash_kernel(q_ref, k_ref, v_ref, o_ref, m_scr, acc_scr, *, bq, bk):
    i = pl.program_id(1)
    n_full = (i * bq) // bk
    n_all = ((i + 1) * bq) // bk
    lane = lax.broadcasted_iota(jnp.int32, (bq, LANE), 1)
    qpos = i * bq + lax.broadcasted_iota(jnp.int32, (bq, bk), 0)
    kcol = lax.broadcasted_iota(jnp.int32, (bq, bk), 1)
    m_scr[...] = jnp.full_like(m_scr, -jnp.inf)
    acc_scr[...] = jnp.zeros_like(acc_scr)

    def step(j, masked):
        k0 = pl.multiple_of(j * bk, bk)
        keep = (kcol + j * bk <= qpos) if masked else None
        hsl = lambda h: slice(h * HEAD_PAD, (h + 1) * HEAD_PAD)
        ss = [lax.dot_general(q_ref[:, hsl(h)], k_ref[pl.ds(k0, bk), hsl(h)], (((1,), (1,)), ((), ())),
                              preferred_element_type=F32) for h in range(MLA_HEADS)]
        for h in range(MLA_HEADS):
            s = jnp.where(keep, ss[h], -jnp.inf) if masked else ss[h]
            m_prev = m_scr[h]
            m_next = jnp.maximum(m_prev, jnp.max(s, axis=-1, keepdims=True))
            p = jnp.exp2(s - m_next[:, 0:1])
            acc_scr[h] = jnp.exp2(m_prev - m_next) * acc_scr[h] + jnp.dot(
                p.astype(BF16), v_ref[pl.ds(k0, bk), hsl(h)], preferred_element_type=F32)
            m_scr[h] = m_next

    def body_full(j, carry):
        step(j, False)
        return carry

    def body_diag(j, carry):
        step(j, True)
        return carry

    lax.fori_loop(0, n_full, body_full, 0)
    lax.fori_loop(n_full, n_all, body_diag, 0)
    for pair in range(MLA_HEADS // 2):
        a0 = acc_scr[2 * pair]
        a1 = acc_scr[2 * pair + 1]
        o_ref[:, pair * LANE:(pair + 1) * LANE] = jnp.where(
            lane < MLA_VDIM, a0 / pltpu.roll(a0, MLA_VDIM, 1), a1 / pltpu.roll(a1, MLA_VDIM, 1)).astype(o_ref.dtype)


def _flash(q, k, v, bsz, seq, bq, bk):
    nq = seq // bq
    hp = MLA_HEADS * HEAD_PAD
    return pl.pallas_call(
        functools.partial(_flash_kernel, bq=bq, bk=bk),
        grid=(bsz, nq),
        in_specs=[pl.BlockSpec((bq, hp), lambda b, i: (b * nq + i, 0)),
                  pl.BlockSpec((seq, hp), lambda b, i: (b, 0)),
                  pl.BlockSpec((seq, hp), lambda b, i: (b, 0))],
        out_specs=pl.BlockSpec((bq, MLA_V), lambda b, i: (b * nq + i, 0)),
        out_shape=jax.ShapeDtypeStruct((bsz * seq, MLA_V), BF16),
        scratch_shapes=[pltpu.VMEM((MLA_HEADS, bq, LANE), F32), pltpu.VMEM((MLA_HEADS, bq, LANE), F32)],
        compiler_params=_cparams(("parallel", "arbitrary")),
        name="mla_flash",
    )(q, k, v)


def _merge_kernel(x_ref, oa_ref, ob_ref, oc_ref, gates_ref, wa_ref, wb_ref, wc_ref, wo_ref, y_ref):
    g = gates_ref
    merged = (_sigmoid(g[:, 0:D_MODEL].astype(F32)) * _dot(oa_ref[...], wa_ref[...])
              + _sigmoid(g[:, D_MODEL:2 * D_MODEL].astype(F32)) * _dot(ob_ref[...], wb_ref[...])
              + _sigmoid(g[:, 2 * D_MODEL:].astype(F32)) * _dot(oc_ref[...], wc_ref[...]))
    y_ref[...] = x_ref[...] + _dot(merged, wo_ref[...])


def _merge(x, o_a, o_b, o_c, gates, w_a, w_b, w_c, w_o, tm):
    m = x.shape[0]
    row = lambda i: (i, 0)
    return pl.pallas_call(
        _merge_kernel,
        grid=(m // tm,),
        in_specs=[pl.BlockSpec((tm, D_MODEL), row),
                  pl.BlockSpec((tm, GDN_V), row),
                  pl.BlockSpec((tm, MLA_V), row),
                  pl.BlockSpec((tm, ML_V), row),
                  pl.BlockSpec((tm, 3 * D_MODEL), row),
                  _const_spec((GDN_V, D_MODEL)),
                  _const_spec((MLA_V, D_MODEL)),
                  _const_spec((ML_V, D_MODEL)),
                  _const_spec((D_MODEL, D_MODEL))],
        out_specs=pl.BlockSpec((tm, D_MODEL), row),
        out_shape=jax.ShapeDtypeStruct((m, D_MODEL), F32),
        compiler_params=_cparams(("parallel",)),
        name="merge",
    )(x, o_a, o_b, o_c, gates, w_a, w_b, w_c, w_o)


def _ffn_prompt_kernel(x_ref, g_ref, wup_ref, cw_ref, cb_ref, wdn_ref, gf_ref,
                       y_ref, conv_out_ref, gp_scr, *, tm, final):
    t = pl.program_id(1)

    @pl.when(t == 0)
    def _():
        gp_scr[0:8, :] = jnp.zeros((8, D_FF), F32)

    x = x_ref[...]
    hn = _rms(x, g_ref[...]).astype(BF16)
    u = jnp.dot(hn, wup_ref[:, 0:D_FF], preferred_element_type=F32)
    gt = jnp.dot(hn, wup_ref[:, D_FF:], preferred_element_type=F32)
    gp_scr[8:8 + tm, :] = gt
    cw = cw_ref[...]
    conv = gt * cw[2:3] + gp_scr[7:7 + tm, :] * cw[1:2] + gp_scr[6:6 + tm, :] * cw[0:1]
    gp_scr[0:8, :] = gt[tm - 8:tm, :]
    act = _silu(conv + cb_ref[...]) * u
    y = x + _dot(act, wdn_ref[...])
    if final:
        y = _rms(y, gf_ref[...])
    y_ref[...] = y

    @pl.when(t == pl.num_programs(1) - 1)
    def _():
        conv_out_ref[0] = gt[tm - (FFN_CONV - 1):tm, :]


def _ffn_prompt(x, norm_g, w_up, conv_w, conv_b, w_down, final_g, bsz, seq, tm, final):
    nt = seq // tm
    row = lambda b, t: (b * nt + t, 0)
    return pl.pallas_call(
        functools.partial(_ffn_prompt_kernel, tm=tm, final=final),
        grid=(bsz, nt),
        in_specs=[pl.BlockSpec((tm, D_MODEL), row),
                  _const_spec((1, D_MODEL)),
                  _const_spec((D_MODEL, 2 * D_FF)),
                  _const_spec((FFN_CONV, D_FF)),
                  _const_spec((1, D_FF)),
                  _const_spec((D_FF, D_MODEL)),
                  _const_spec((1, D_MODEL))],
        out_specs=[pl.BlockSpec((tm, D_MODEL), row),
                   pl.BlockSpec((1, FFN_CONV - 1, D_FF), lambda b, t: (b, 0, 0))],
        out_shape=[jax.ShapeDtypeStruct((bsz * seq, D_MODEL), F32),
                   jax.ShapeDtypeStruct((bsz, FFN_CONV - 1, D_FF), F32)],
        scratch_shapes=[pltpu.VMEM((tm + 8, D_FF), F32)],
        compiler_params=_cparams(("parallel", "arbitrary")),
        name="ffn_prompt",
    )(x, norm_g.reshape(1, -1), w_up, conv_w, conv_b.reshape(1, -1), w_down, final_g.reshape(1, -1))


def _ffn_sample_kernel(x_ref, hist_ref, g_ref, wup_ref, cw_ref, cb_ref, wdn_ref, gf_ref,
                       y_ref, hist_out_ref, *, final):
    x = x_ref[...]
    hn = _rms(x, g_ref[...]).astype(BF16)
    u = jnp.dot(hn, wup_ref[:, 0:D_FF], preferred_element_type=F32)
    gt = jnp.dot(hn, wup_ref[:, D_FF:], preferred_element_type=F32)
    cw = cw_ref[...]
    conv = hist_ref[0] * cw[0:1] + hist_ref[1] * cw[1:2] + gt * cw[2:3]
    act = _silu(conv + cb_ref[...]) * u
    y = x + _dot(act, wdn_ref[...])
    if final:
        y = _rms(y, gf_ref[...])
    y_ref[...] = y
    hist_out_ref[0] = hist_ref[1]
    hist_out_ref[1] = gt


def _ffn_sample(x, hist_t, norm_g, w_up, conv_w, conv_b, w_down, final_g, final):
    bs = x.shape[0]
    return pl.pallas_call(
        functools.partial(_ffn_sample_kernel, final=final),
        out_shape=[jax.ShapeDtypeStruct((bs, D_MODEL), F32),
                   jax.ShapeDtypeStruct((FFN_CONV - 1, bs, D_FF), F32)],
        compiler_params=pltpu.CompilerParams(vmem_limit_bytes=VMEM_LIMIT),
        name="ffn_sample",
    )(x, hist_t, norm_g.reshape(1, -1), w_up, conv_w, conv_b.reshape(1, -1), w_down, final_g.reshape(1, -1))


def _row_to_col(row, eye):
    n = eye.shape[0]
    return jnp.sum(jnp.where(eye, jnp.broadcast_to(row, (n, n)), 0.0), axis=1, keepdims=True)


def _rec_sample_kernel(aqkv_ref, hist_ref, az_ref, small_ref, convw_ref, alog_ref, dtb_ref, gng_ref,
                       s0_ref, cqkv_ref, co_ref, bi_ref, bf_ref, mng_ref, c0_ref, n0_ref, m0_ref,
                       oa_ref, hist_out_ref, s_out_ref, oc_ref, c_out_ref, n_out_ref, m_out_ref):
    _, _, eye128 = _tri_masks(LANE)
    lane = lax.broadcasted_iota(jnp.int32, (1, LANE), 1)
    sm = small_ref[0]
    x = aqkv_ref[0]
    hist = hist_ref[0]
    w = convw_ref[...]
    y = _silu(hist[0:1] * w[0:1] + hist[1:2] * w[1:2] + hist[2:3] * w[2:3] + x * w[3:4])
    hist_out_ref[0, 0:1, :] = hist[1:2]
    hist_out_ref[0, 1:2, :] = hist[2:3]
    hist_out_ref[0, 2:3, :] = x
    beta_row = _sigmoid(sm)
    decay_row = jnp.exp(-jnp.exp(alog_ref[...]) * _softplus(sm + dtb_ref[...]))
    gng = gng_ref[...]
    az = az_ref[0]
    for h in range(GDN_HEADS):
        q_row = _l2(y[:, h * GDN_DK:(h + 1) * GDN_DK]) * (GDN_DK ** -0.5)
        k_row = _l2(y[:, GDN_QK + h * GDN_DK:GDN_QK + (h + 1) * GDN_DK])
        v_row = y[:, 2 * GDN_QK + h * GDN_DV:2 * GDN_QK + (h + 1) * GDN_DV]
        beta = beta_row[:, SM_BETA + h:SM_BETA + h + 1]
        a = decay_row[:, SM_ALPHA + h:SM_ALPHA + h + 1]
        k_col = _row_to_col(k_row, eye128)
        q_col = _row_to_col(q_row, eye128)
        s_old = s0_ref[0, h]
        ks = jnp.sum(k_col * s_old, axis=0, keepdims=True)
        v_new = beta * (v_row - a * ks)
        qs = jnp.sum(q_col * s_old, axis=0, keepdims=True)
        qk = jnp.sum(q_row * k_row, axis=-1, keepdims=True)
        o = a * qs + qk * v_new
        s_out_ref[0, h] = a * s_old + k_col * v_new
        z = az[:, h * GDN_DV:(h + 1) * GDN_DV]
        oa_ref[0, :, h * GDN_DV:(h + 1) * GDN_DV] = _rms(o, gng) * _silu(z)

    cq = cqkv_ref[0]
    co = co_ref[0]
    ip_row = sm + bi_ref[...]
    lf_row = -_softplus(-(sm + bf_ref[...]))
    mng = mng_ref[...]
    m_prev = m0_ref[0]
    m_out = jnp.zeros((1, LANE), F32)
    for h in range(ML_HEADS):
        q_row = cq[:, h * LANE:(h + 1) * LANE]
        k_row = cq[:, (ML_HEADS + h) * LANE:(ML_HEADS + h + 1) * LANE] * (ML_DK ** -0.5)
        v_row = cq[:, 2 * ML_HEADS * LANE + h * ML_DV:2 * ML_HEADS * LANE + (h + 1) * ML_DV]
        ip = ip_row[:, SM_I + h:SM_I + h + 1]
        lf = lf_row[:, SM_F + h:SM_F + h + 1]
        m_old = m_prev[:, h:h + 1]
        c_old = c0_ref[0, h]
        n_old = n0_ref[0, h:h + 1, :]
        k_col = _row_to_col(k_row, eye128)[0:ML_DK]
        q_col = _row_to_col(q_row, eye128)[0:ML_DK]
        b_log = lf + m_old
        m_r = jnp.maximum(b_log, ip)
        inter = jnp.exp(b_log - m_r)
        w_key = jnp.exp(ip - m_r)
        qk = jnp.sum(q_row * k_row, axis=-1, keepdims=True)
        p = w_key * qk
        num = inter * jnp.sum(q_col * c_old, axis=0, keepdims=True) + p * v_row
        den = inter * jnp.sum(q_row * n_old, axis=-1, keepdims=True) + p
        h_t = num / jnp.maximum(jnp.abs(den), jnp.exp(-m_r))
        c_out_ref[0, h] = inter * c_old + (w_key * k_col) * v_row
        n_out_ref[0, h:h + 1, :] = inter * n_old + w_key * k_row
        m_out = jnp.where(lane == h, m_r, m_out)
        og = _sigmoid(co[:, h * ML_DV:(h + 1) * ML_DV])
        oc_ref[0, :, h * ML_DV:(h + 1) * ML_DV] = _rms(og * h_t, mng)
    m_out_ref[0] = m_out


def _rec_sample(a_qkv, hist, a_z, small, conv_w, alog_row, dtb_row, gdn_norm_g, s0,
                c_qkv, c_o, bi_row, bf_row, ml_norm_g, c0, n0_pad, m0_pad):
    bs = a_qkv.shape[0]
    r3 = lambda a: a.reshape(bs, 1, a.shape[-1])
    b3 = lambda n: pl.BlockSpec((1, 1, n), lambda b: (b, 0, 0))
    mlw = 2 * ML_HEADS * LANE + ML_V
    return pl.pallas_call(
        _rec_sample_kernel,
        grid=(bs,),
        in_specs=[b3(GDN_QKV),
                  pl.BlockSpec((1, GDN_CONV - 1, GDN_QKV), lambda b: (b, 0, 0)),
                  b3(GDN_V), b3(LANE),
                  _const_spec((GDN_CONV, GDN_QKV)),
                  _const_spec((1, LANE)), _const_spec((1, LANE)), _const_spec((1, GDN_DV)),
                  pl.BlockSpec((1, GDN_HEADS, GDN_DK, GDN_DV), lambda b: (b, 0, 0, 0)),
                  b3(mlw), b3(ML_V),
                  _const_spec((1, LANE)), _const_spec((1, LANE)), _const_spec((1, ML_DV)),
                  pl.BlockSpec((1, ML_HEADS, ML_DK, ML_DV), lambda b: (b, 0, 0, 0)),
                  pl.BlockSpec((1, ML_HEADS, LANE), lambda b: (b, 0, 0)),
                  b3(LANE)],
        out_specs=[b3(GDN_V),
                   pl.BlockSpec((1, GDN_CONV - 1, GDN_QKV), lambda b: (b, 0, 0)),
                   pl.BlockSpec((1, GDN_HEADS, GDN_DK, GDN_DV), lambda b: (b, 0, 0, 0)),
                   b3(ML_V),
                   pl.BlockSpec((1, ML_HEADS, ML_DK, ML_DV), lambda b: (b, 0, 0, 0)),
                   pl.BlockSpec((1, ML_HEADS, LANE), lambda b: (b, 0, 0)),
                   b3(LANE)],
        out_shape=[jax.ShapeDtypeStruct((bs, 1, GDN_V), F32),
                   jax.ShapeDtypeStruct((bs, GDN_CONV - 1, GDN_QKV), F32),
                   jax.ShapeDtypeStruct((bs, GDN_HEADS, GDN_DK, GDN_DV), F32),
                   jax.ShapeDtypeStruct((bs, 1, ML_V), F32),
                   jax.ShapeDtypeStruct((bs, ML_HEADS, ML_DK, ML_DV), F32),
                   jax.ShapeDtypeStruct((bs, ML_HEADS, LANE), F32),
                   jax.ShapeDtypeStruct((bs, 1, LANE), F32)],
        compiler_params=_cparams(("parallel",)),
        name="rec_sample",
    )(r3(a_qkv), hist, r3(a_z), r3(small), conv_w, alog_row, dtb_row, gdn_norm_g.reshape(1, -1), s0,
      r3(c_qkv), r3(c_o), bi_row, bf_row, ml_norm_g.reshape(1, -1), c0, n0_pad, r3(m0_pad))


def _mla_sample_pre_kernel(cq_ref, ckv_ref, small_ref, ctab_ref, stab_ref, qg_ref, kvg_ref,
                           wuq_ref, wukt_ref, qlat_ref, qrope_ref, ckv_out_ref, kr_out_ref):
    ctab = ctab_ref[...]
    stab = stab_ref[...]
    lane = lax.broadcasted_iota(jnp.int32, (1, LANE), 1)
    in_rope = (lane >= MLA_NOPE) & (lane < MLA_QHEAD)
    sm = small_ref[...]
    kr_out_ref[...] = jnp.where(in_rope, sm * ctab + _rope_swap(sm, MLA_NOPE) * stab, 0.0)
    ckv_out_ref[...] = _rms(ckv_ref[...], kvg_ref[...])
    qn = _rms(cq_ref[...], qg_ref[...])
    q = _dot(qn, wuq_ref[...])
    q_sw = _rope_swap(q, MLA_NOPE)
    for h in range(MLA_HEADS):
        sl = slice(h * HEAD_PAD, (h + 1) * HEAD_PAD)
        qh = q[:, sl]
        qrope_ref[:, sl] = jnp.where(in_rope, qh * ctab + q_sw[:, sl] * stab, 0.0)
        qlat_ref[:, sl] = _dot(jnp.where(lane < MLA_NOPE, qh, 0.0), wukt_ref[h])


def _mla_sample_pre(b_cq, b_ckv, small, ctab, stab, q_norm_g, kv_norm_g, w_uq_pad, w_uk_t):
    bs = b_cq.shape[0]
    hp = MLA_HEADS * HEAD_PAD
    return pl.pallas_call(
        _mla_sample_pre_kernel,
        out_shape=[jax.ShapeDtypeStruct((bs, hp), F32),
                   jax.ShapeDtypeStruct((bs, hp), F32),
                   jax.ShapeDtypeStruct((bs, MLA_KV_LORA), F32),
                   jax.ShapeDtypeStruct((bs, LANE), F32)],
        compiler_params=pltpu.CompilerParams(vmem_limit_bytes=VMEM_LIMIT),
        name="mla_sample_pre",
    )(b_cq, b_ckv, small, ctab, stab, q_norm_g.reshape(1, -1), kv_norm_g.reshape(1, -1), w_uq_pad, w_uk_t)


def _mla_sample_kernel(pt_ref, qlat_ref, qrope_ref, ckv_ref, kr_ref, lat_hbm, krt_hbm, ctx_ref,
                       lat_buf, krt_buf, sem, *, layer, n_pages):
    b = pl.program_id(0)
    slot = b % 2

    def page_copies(row, buf_slot, g):
        page = pt_ref[row, g]
        off = pl.multiple_of(g * PAGE_SIZE, PAGE_SIZE)
        return (pltpu.make_async_copy(lat_hbm.at[layer, page], lat_buf.at[buf_slot, pl.ds(off, PAGE_SIZE), :],
                                      sem.at[0, buf_slot]),
                pltpu.make_async_copy(krt_hbm.at[layer, page], krt_buf.at[buf_slot, :, pl.ds(off, PAGE_SIZE)],
                                      sem.at[1, buf_slot]))

    def start_row(row, buf_slot):
        def body(g, carry):
            for cp in page_copies(row, buf_slot, g):
                cp.start()
            return carry
        lax.fori_loop(0, n_pages, body, 0)

    def wait_row(row, buf_slot):
        def body(g, carry):
            for cp in page_copies(row, buf_slot, g):
                cp.wait()
            return carry
        lax.fori_loop(0, n_pages, body, 0)

    @pl.when(b == 0)
    def _():
        start_row(0, 0)

    @pl.when(b + 1 < pl.num_programs(0))
    def _():
        start_row(b + 1, 1 - slot)

    wait_row(b, slot)

    q_lat = qlat_ref[0].astype(BF16)
    q_rope = qrope_ref[0][:, MLA_NOPE:MLA_QHEAD].astype(BF16)
    lat = lat_buf[slot].astype(BF16)
    s = (_dot_nt(q_lat, lat) + _dot(q_rope, krt_buf[slot])) * MLA_SCALE
    c_new = ckv_ref[0].astype(BF16).astype(F32)
    kr_new = kr_ref[0][:, MLA_NOPE:MLA_QHEAD].astype(BF16).astype(F32)
    s_new = (jnp.sum(q_lat.astype(F32) * c_new, axis=-1, keepdims=True)
             + jnp.sum(q_rope.astype(F32) * kr_new, axis=-1, keepdims=True)) * MLA_SCALE
    m = jnp.maximum(jnp.max(s, axis=-1, keepdims=True), s_new)
    p = jnp.exp(s - m)
    p_new = jnp.exp(s_new - m)
    denom = jnp.sum(p, axis=-1, keepdims=True) + p_new
    ctx_ref[0] = (_dot(p, lat) + p_new.astype(BF16).astype(F32) * c_new) / denom


def _mla_sample(page_table, q_lat, q_rope, c_kv, kr, cache_lat, cache_krt, layer):
    bs, n_pages = page_table.shape
    past = n_pages * PAGE_SIZE
    hp3 = lambda a: a.reshape(bs, MLA_HEADS, HEAD_PAD)
    per_b = lambda n: pl.BlockSpec((1, n, LANE), lambda b, pt: (b, 0, 0))
    grid_spec = pltpu.PrefetchScalarGridSpec(
        num_scalar_prefetch=1,
        grid=(bs,),
        in_specs=[per_b(MLA_HEADS), per_b(MLA_HEADS), per_b(1), per_b(1),
                  pl.BlockSpec(memory_space=pl.ANY), pl.BlockSpec(memory_space=pl.ANY)],
        out_specs=per_b(MLA_HEADS),
        scratch_shapes=[pltpu.VMEM((2, past, MLA_KV_LORA), F32),
                        pltpu.VMEM((2, MLA_ROPE, past), F32),
                        pltpu.SemaphoreType.DMA((2, 2))],
    )
    return pl.pallas_call(
        functools.partial(_mla_sample_kernel, layer=layer, n_pages=n_pages),
        grid_spec=grid_spec,
        out_shape=jax.ShapeDtypeStruct((bs, MLA_HEADS, LANE), F32),
        compiler_params=_cparams(("arbitrary",)),
        name="mla_sample",
    )(page_table, hp3(q_lat), hp3(q_rope), c_kv.reshape(bs, 1, LANE), kr.reshape(bs, 1, LANE),
      cache_lat, cache_krt)


def _ctx_up_kernel(ctx_ref, wuv_ref, o_ref):
    o_ref[...] = _dot(ctx_ref[...], wuv_ref[...])


def _ctx_up(ctx, w_uv_bd):
    return pl.pallas_call(
        _ctx_up_kernel,
        out_shape=jax.ShapeDtypeStruct((ctx.shape[0], MLA_V), F32),
        name="ctx_up",
    )(ctx, w_uv_bd)


def _lane_row(vals, lane0):
    return jnp.zeros((1, LANE), F32).at[0, lane0:lane0 + vals.shape[0]].set(vals.astype(F32))


def _pick(n, pref):
    t = min(n, pref)
    while n % t:
        t //= 2
    return t


def kernel(x_prompt, x_sample, cache_kv_latent, cache_k_rope, state_gdn_conv, state_gdn_S,
           state_mlstm_C, state_mlstm_n, state_mlstm_m, state_ffn_conv, page_table,
           norm1_g, w_in, gdn_conv_w, gdn_A_log, gdn_dt_bias, gdn_norm_g,
           mla_q_norm_g, mla_w_uq, mla_kv_norm_g, mla_w_uk, mla_w_uv,
           ml_b_i, ml_b_f, ml_norm_g, w_branch_a, w_branch_b, w_branch_c, w_out,
           norm2_g, ffn_w_up, ffn_conv_w, ffn_conv_b, ffn_w_down, final_norm_g):
    bp, seq = x_prompt.shape[:2]
    bs, dec = x_sample.shape[:2]
    assert dec == 1 and seq % REC_CHUNK == 0
    depth = w_in.shape[0]
    n_pages = page_table.shape[1]
    past_len = n_pages * PAGE_SIZE
    mp = bp * seq

    tm = _pick(seq, 512)
    tb = _pick(seq, 256)
    bq = _pick(seq, 256)

    ctab_p, stab_p = _rope_tables(jnp.arange(seq))
    ctab_s, stab_s = _rope_tables(jnp.full((1,), past_len))
    cache_krt = jnp.swapaxes(cache_k_rope, 2, 3)

    xp = x_prompt.reshape(mp, D_MODEL)
    xs = x_sample.reshape(bs, D_MODEL)
    out_p = [[] for _ in range(8)]
    out_s = [[] for _ in range(8)]

    for l in range(depth):
        last = l == depth - 1
        w_p, w_small_t = _pack_w_in(w_in[l], False)
        w_s, _ = _pack_w_in(w_in[l], True)
        w_uq_pad = _pad_heads(mla_w_uq[l], MLA_QHEAD).astype(BF16)
        w_uk_pad = _pad_heads(mla_w_uk[l], MLA_NOPE).astype(BF16)
        w_uv = mla_w_uv[l].astype(BF16)
        w_a, w_b, w_c, w_o = (w_branch_a[l].astype(BF16), w_branch_b[l].astype(BF16),
                              w_branch_c[l].astype(BF16), w_out[l].astype(BF16))
        w_up, w_dn = ffn_w_up[l].astype(BF16), ffn_w_down[l].astype(BF16)

        a_qkv, a_z, b_cq, b_ckv, c_qkv, c_o, gates, small, small_t = _in_proj(
            xp, norm1_g[l], w_p, PROMPT_GROUPS, tm, w_small_t)
        o_a, p_gconv, p_gs = _gdn_prompt(a_qkv, a_z, small, small_t, gdn_conv_w[l], gdn_A_log[l], gdn_dt_bias[l],
                                         gdn_norm_g[l], bp, seq, tb)
        q_pad, k_pad, v_all, p_ckv, p_kr = _mla_pre(b_cq, b_ckv, small, ctab_p, stab_p, mla_q_norm_g[l],
                                                    mla_kv_norm_g[l], w_uq_pad, w_uk_pad, w_uv, seq, tm)
        o_b = _flash(q_pad, k_pad, v_all, bp, seq, bq, bq)
        o_c, p_c, p_n, p_m = _mlstm_prompt(c_qkv, c_o, small, small_t, ml_b_i[l], ml_b_f[l], ml_norm_g[l], bp, seq, tb)
        xp = _merge(xp, o_a, o_b, o_c, gates, w_a, w_b, w_c, w_o, tm)
        xp, p_fconv = _ffn_prompt(xp, norm2_g[l], w_up, ffn_conv_w[l], ffn_conv_b[l], w_dn, final_norm_g,
                                  bp, seq, tm, last)
        for lst, a in zip(out_p, (p_ckv.reshape(bp, seq, MLA_KV_LORA), jnp.swapaxes(p_kr, 1, 2),
                                  p_gconv, p_gs, p_c, p_n[..., 0], p_m[..., 0], p_fconv)):
            lst.append(a)

        a_qkv, a_z, b_cq, b_ckv, c_qkv, c_o, gates, small = _in_proj(xs, norm1_g[l], w_s, SAMPLE_GROUPS, bs)
        n0_pad = jnp.pad(state_mlstm_n[l], ((0, 0), (0, 0), (0, LANE - ML_DK)))
        m0_pad = jnp.pad(state_mlstm_m[l], ((0, 0), (0, LANE - ML_HEADS)))
        o_a, s_gconv, s_gs, o_c, s_c, s_n, s_m = _rec_sample(
            a_qkv, state_gdn_conv[l], a_z, small, gdn_conv_w[l],
            _lane_row(gdn_A_log[l], SM_ALPHA), _lane_row(gdn_dt_bias[l], SM_ALPHA), gdn_norm_g[l],
            state_gdn_S[l], c_qkv, c_o, _lane_row(ml_b_i[l], SM_I), _lane_row(ml_b_f[l], SM_F),
            ml_norm_g[l], state_mlstm_C[l], n0_pad, m0_pad)
        w_uk_t = jnp.pad(jnp.transpose(mla_w_uk[l].reshape(MLA_KV_LORA, MLA_HEADS, MLA_NOPE), (1, 2, 0)),
                         ((0, 0), (0, HEAD_PAD - MLA_NOPE), (0, 0))).astype(BF16)
        q_lat, q_rope, s_ckv, kr_s = _mla_sample_pre(b_cq, b_ckv, small, ctab_s, stab_s, mla_q_norm_g[l],
                                                      mla_kv_norm_g[l], w_uq_pad, w_uk_t)
        ctx = _mla_sample(page_table, q_lat, q_rope, s_ckv, kr_s, cache_kv_latent, cache_krt, l)
        w_uv_bd = (jnp.eye(MLA_HEADS, dtype=F32)[:, None, :, None]
                   * jnp.transpose(mla_w_uv[l].reshape(MLA_KV_LORA, MLA_HEADS, MLA_VDIM), (1, 0, 2))[:, :, None, :]
                   ).reshape(MLA_HEADS * MLA_KV_LORA, MLA_V).astype(BF16)
        o_b = _ctx_up(ctx.reshape(bs, MLA_HEADS * LANE), w_uv_bd)
        xs = _merge(xs, o_a.reshape(bs, GDN_V), o_b, o_c.reshape(bs, ML_V), gates, w_a, w_b, w_c, w_o, bs)
        xs, s_fconv_t = _ffn_sample(xs, jnp.swapaxes(state_ffn_conv[l], 0, 1), norm2_g[l], w_up, ffn_conv_w[l],
                                    ffn_conv_b[l], w_dn, final_norm_g, last)
        for lst, a in zip(out_s, (s_ckv.reshape(bs, 1, MLA_KV_LORA),
                                  kr_s[:, MLA_NOPE:MLA_QHEAD].reshape(bs, 1, MLA_ROPE),
                                  s_gconv, s_gs, s_c, s_n[:, :, :ML_DK], s_m.reshape(bs, LANE)[:, :ML_HEADS],
                                  jnp.swapaxes(s_fconv_t, 0, 1))):
            lst.append(a)

    y_prompt = xp.reshape(bp, seq, D_MODEL)
    y_sample = xs.reshape(bs, 1, D_MODEL)
    return (y_prompt, y_sample) + tuple(jnp.stack(a) for a in out_p) + tuple(jnp.stack(a) for a in out_s)
```

```python
import functools
import math

import jax
import jax.numpy as jnp
import numpy as np
from jax import lax
from jax.experimental import pallas as pl
from jax.experimental.pallas import tpu as pltpu

F32 = jnp.float32
BF16 = jnp.bfloat16

D_MODEL = 1024
PAGE_SIZE = 128
GDN_HEADS, GDN_DK, GDN_DV, GDN_CONV = 4, 128, 128, 4
MLA_HEADS, MLA_Q_LORA, MLA_KV_LORA, MLA_NOPE, MLA_ROPE, MLA_VDIM = 8, 256, 128, 64, 32, 64
ROPE_BASE = 10000.0
ML_HEADS, ML_DK, ML_DV = 4, 64, 128
CHUNK = 64
REC_CHUNK = 64
D_FF = 2816
FFN_CONV = 3
NORM_EPS = 1e-6

GDN_QK = GDN_HEADS * GDN_DK
GDN_V = GDN_HEADS * GDN_DV
GDN_QKV = 2 * GDN_QK + GDN_V
MLA_QHEAD = MLA_NOPE + MLA_ROPE
MLA_V = MLA_HEADS * MLA_VDIM
MLA_SCALE = MLA_QHEAD ** -0.5
ML_QK = ML_HEADS * ML_DK
ML_V = ML_HEADS * ML_DV
ML_QKV = 2 * ML_QK + ML_V
IN_SIZES = (GDN_QKV, GDN_HEADS, GDN_HEADS, GDN_V, MLA_Q_LORA, MLA_KV_LORA, MLA_ROPE,
            ML_QKV, ML_HEADS, ML_HEADS, ML_V, 3 * D_MODEL)

LANE = 128
HEAD_PAD = 128
SM_BETA, SM_ALPHA, SM_I, SM_F, SM_KR = 0, 8, 16, 24, 64
VMEM_LIMIT = 56 * 1024 * 1024
LOG2E = math.log2(math.e)


def _cparams(sem):
    return pltpu.CompilerParams(dimension_semantics=sem, vmem_limit_bytes=VMEM_LIMIT)


def _const_spec(shape):
    nd = len(shape)
    return pl.BlockSpec(shape, lambda *_: (0,) * nd, pipeline_mode=pl.Buffered(1))


def _dot(a, b):
    return jnp.dot(a.astype(BF16), b.astype(BF16), preferred_element_type=F32)


def _dot_nt(a, b):
    return lax.dot_general(a.astype(BF16), b.astype(BF16), (((1,), (1,)), ((), ())),
                           preferred_element_type=F32)


def _dot_tn(a, b):
    return lax.dot_general(a.astype(BF16), b.astype(BF16), (((0,), (0,)), ((), ())),
                           preferred_element_type=F32)


def _dot01(mask_bf16, x):
    x1 = x.astype(BF16)
    r1 = x - x1.astype(F32)
    x2 = r1.astype(BF16)
    x3 = (r1 - x2.astype(F32)).astype(BF16)
    d = functools.partial(jnp.dot, preferred_element_type=F32)
    return d(mask_bf16, x1) + d(mask_bf16, x2) + d(mask_bf16, x3)


def _dot01_r(x, mask_bf16):
    x1 = x.astype(BF16)
    r1 = x - x1.astype(F32)
    x2 = r1.astype(BF16)
    x3 = (r1 - x2.astype(F32)).astype(BF16)
    d = functools.partial(jnp.dot, preferred_element_type=F32)
    return d(x1, mask_bf16) + d(x2, mask_bf16) + d(x3, mask_bf16)


def _rms(x, g):
    return x * lax.rsqrt(jnp.mean(x * x, axis=-1, keepdims=True) + NORM_EPS) * g


def _l2(x):
    return x * lax.rsqrt(jnp.sum(x * x, axis=-1, keepdims=True) + NORM_EPS)


def _sigmoid(x):
    return jax.nn.sigmoid(x)


def _silu(x):
    return x * jax.nn.sigmoid(x)


def _softplus(x):
    return jnp.maximum(x, 0.0) + jnp.log1p(jnp.exp(-jnp.abs(x)))


def _tri_masks(n):
    r = lax.broadcasted_iota(jnp.int32, (n, n), 0)
    c = lax.broadcasted_iota(jnp.int32, (n, n), 1)
    return r >= c, r > c, r == c


def _chunk_cumsum_mask(tb, c):
    r = lax.broadcasted_iota(jnp.int32, (tb, tb), 0)
    q = lax.broadcasted_iota(jnp.int32, (tb, tb), 1)
    return jnp.where((r // c == q // c) & (r >= q), 1.0, 0.0).astype(BF16)


def _neumann_inverse_many(a_list, eye_f):
    n = eye_f.shape[0]
    ps = [-a for a in a_list]
    ts = [eye_f + p for p in ps]
    qs = [_dot(p, p) for p in ps]
    covered = 2
    while 2 * covered < n:
        rs = [_dot(q, jnp.concatenate([t, q], axis=1)) for t, q in zip(ts, qs)]
        ts = [t + r[:, :n] for t, r in zip(ts, rs)]
        qs = [r[:, n:] for r in rs]
        covered *= 2
    return [t + _dot(q, t) for t, q in zip(ts, qs)]


def _rope_swap(x, lane0):
    w = x.shape[-1]
    half = MLA_ROPE // 2
    lane = lax.broadcasted_iota(jnp.int32, x.shape, x.ndim - 1) % LANE
    left = pltpu.roll(x, w - half, x.ndim - 1)
    right = pltpu.roll(x, half, x.ndim - 1)
    first = (lane >= lane0) & (lane < lane0 + half)
    second = (lane >= lane0 + half) & (lane < lane0 + 2 * half)
    return jnp.where(first, left, jnp.where(second, right, 0.0))


def _in_proj_kernel(x_ref, g_ref, w_ref, *rest, with_small_t):
    xn = _rms(x_ref[...], g_ref[...]).astype(BF16)
    out_refs = rest
    if with_small_t:
        wst_ref, out_refs, small_t_ref = rest[0], rest[1:-1], rest[-1]
        small_t_ref[...] = lax.dot_general(wst_ref[...], xn, (((1,), (1,)), ((), ())), preferred_element_type=F32)
    off = 0
    for ref in out_refs:
        n = ref.shape[-1]
        ref[...] = jnp.dot(xn, w_ref[:, off:off + n], preferred_element_type=F32).astype(ref.dtype)
        off += n


def _in_proj(x, g, w_packed, groups, tm, w_small_t=None):
    m = x.shape[0]
    width = w_packed.shape[1]
    assert sum(n for n, _ in groups) == width and m % tm == 0
    with_t = w_small_t is not None
    in_specs = [pl.BlockSpec((tm, D_MODEL), lambda i: (i, 0)),
                _const_spec((1, D_MODEL)),
                _const_spec((D_MODEL, width))]
    out_specs = [pl.BlockSpec((tm, n), lambda i: (i, 0)) for n, _ in groups]
    out_shape = [jax.ShapeDtypeStruct((m, n), dt) for n, dt in groups]
    args = [x, g.reshape(1, D_MODEL), w_packed]
    if with_t:
        in_specs.append(_const_spec((LANE, D_MODEL)))
        out_specs.append(pl.BlockSpec((LANE, tm), lambda i: (0, i)))
        out_shape.append(jax.ShapeDtypeStruct((LANE, m), F32))
        args.append(w_small_t)
    return pl.pallas_call(
        functools.partial(_in_proj_kernel, with_small_t=with_t),
        grid=(m // tm,),
        in_specs=in_specs,
        out_specs=out_specs,
        out_shape=out_shape,
        compiler_params=_cparams(("parallel",)),
        name="in_proj",
    )(*args)


def _small_cols(a_beta, a_alpha, b_kr, c_i, c_f):
    small = jnp.zeros((D_MODEL, LANE), F32)
    small = small.at[:, SM_BETA:SM_BETA + GDN_HEADS].set(a_beta)
    small = small.at[:, SM_ALPHA:SM_ALPHA + GDN_HEADS].set(a_alpha)
    small = small.at[:, SM_I:SM_I + ML_HEADS].set(c_i)
    small = small.at[:, SM_F:SM_F + ML_HEADS].set(c_f)
    small = small.at[:, SM_KR:SM_KR + MLA_ROPE].set(b_kr)
    return small


PROMPT_GROUPS = ((GDN_QKV, F32), (GDN_V, BF16), (MLA_Q_LORA, BF16), (MLA_KV_LORA, F32),
                 (ML_QKV, BF16), (ML_V, BF16), (3 * D_MODEL, BF16), (LANE, F32))
SAMPLE_GROUPS = ((GDN_QKV, F32), (GDN_V, F32), (MLA_Q_LORA, F32), (MLA_KV_LORA, F32),
                 (2 * ML_HEADS * LANE + ML_V, F32), (ML_V, F32), (3 * D_MODEL, F32), (LANE, F32))


def _pack_w_in(w_in, pad_ml_heads):
    (a_qkv, a_beta, a_alpha, a_z, b_cq, b_ckv, b_kr,
     c_qkv, c_i, c_f, c_o, gates) = jnp.split(w_in, np.cumsum(IN_SIZES)[:-1].tolist(), axis=-1)
    if pad_ml_heads:
        qk = c_qkv[:, :2 * ML_QK].reshape(D_MODEL, 2 * ML_HEADS, ML_DK)
        qk = jnp.pad(qk, ((0, 0), (0, 0), (0, LANE - ML_DK))).reshape(D_MODEL, 2 * ML_HEADS * LANE)
        c_qkv = jnp.concatenate([qk, c_qkv[:, 2 * ML_QK:]], axis=1)
    small = _small_cols(a_beta, a_alpha, b_kr, c_i, c_f)
    packed = jnp.concatenate([a_qkv, a_z, b_cq, b_ckv, c_qkv, c_o, gates, small], axis=1).astype(BF16)
    return packed, small.T.astype(BF16)


def _gdn_prompt_kernel(aqkv_ref, az_ref, small_ref, small_t_ref, convw_ref, alog_ref, dtb_ref,
                       alog_c_ref, dtb_c_ref, ng_ref,
                       o_ref, conv_out_ref, s_out_ref, xp_scr, s_scr, *, tb):
    t = pl.program_id(1)
    c = REC_CHUNK
    nc = tb // c
    heads = range(GDN_HEADS)
    pairs = [(ci, h) for ci in range(nc) for h in heads]

    @pl.when(t == 0)
    def _():
        s_scr[...] = jnp.zeros_like(s_scr)
        xp_scr[0:8, :] = jnp.zeros((8, GDN_QKV), F32)

    x = aqkv_ref[...]
    xp_scr[8:8 + tb, :] = x
    w = convw_ref[...]
    y = (x * w[3:4] + xp_scr[7:7 + tb, :] * w[2:3]
         + xp_scr[6:6 + tb, :] * w[1:2] + xp_scr[5:5 + tb, :] * w[0:1])
    xp_scr[0:8, :] = x[tb - 8:tb, :]
    y = _silu(y)
    sm = small_ref[...]
    beta = _sigmoid(sm[:, SM_BETA:SM_BETA + GDN_HEADS])
    g_c = -jnp.exp(alog_ref[...]) * _softplus(sm[:, SM_ALPHA:SM_ALPHA + GDN_HEADS] + dtb_ref[...])
    g_r = -jnp.exp(alog_c_ref[...]) * _softplus(small_t_ref[SM_ALPHA:SM_ALPHA + GDN_HEADS, :] + dtb_c_ref[...])
    cmask = _chunk_cumsum_mask(tb, c)
    big_c = _dot01(cmask, g_c)
    big_r = _dot01_r(g_r, cmask.T)

    incl, strict, eye = _tri_masks(c)
    eye_f = jnp.where(eye, 1.0, 0.0).astype(F32)
    ng = ng_ref[...]
    rows = lambda ci: slice(ci * c, (ci + 1) * c)

    qs = [_l2(y[rows(ci), h * GDN_DK:(h + 1) * GDN_DK]) * (GDN_DK ** -0.5) for ci, h in pairs]
    ks = [_l2(y[rows(ci), GDN_QK + h * GDN_DK:GDN_QK + (h + 1) * GDN_DK]) for ci, h in pairs]
    vs = [y[rows(ci), 2 * GDN_QK + h * GDN_DV:2 * GDN_QK + (h + 1) * GDN_DV] for ci, h in pairs]
    bcols = [beta[rows(ci), h:h + 1] for ci, h in pairs]
    gcols = [big_c[rows(ci), h:h + 1] for ci, h in pairs]
    decs = [jnp.exp(jnp.where(incl, big_c[rows(ci), h:h + 1] - big_r[h:h + 1, rows(ci)], -jnp.inf))
            for ci, h in pairs]
    qkk = [_dot_nt(jnp.concatenate([q, k], axis=0), k) for q, k in zip(qs, ks)]
    a_list = [b * m[c:] * jnp.where(strict, d, 0.0) for b, m, d in zip(bcols, qkk, decs)]
    tinvs = _neumann_inverse_many(a_list, eye_f)
    e_gs = [jnp.exp(g) for g in gcols]
    uws = [_dot(ti, jnp.concatenate([b * v, (b * e) * k], axis=1))
           for ti, b, v, e, k in zip(tinvs, bcols, vs, e_gs, ks)]
    wqs = [jnp.concatenate([uw[:, GDN_DV:], q * e], axis=0).astype(BF16) for uw, q, e in zip(uws, qs, e_gs)]
    qkd = [(m[:c] * d).astype(BF16) for m, d in zip(qkk, decs)]
    kds = [(k * jnp.exp(g[c - 1:c, :] - g)).astype(BF16) for k, g in zip(ks, gcols)]
    gls = [jnp.exp(jnp.broadcast_to(g[c - 1:c, :], (1, GDN_DV))) for g in gcols]

    s_st = [s_scr[h] for h in heads]
    for ci in range(nc):
        idx = [ci * GDN_HEADS + h for h in heads]
        ws_qs = [jnp.dot(wqs[i], s_st[h].astype(BF16), preferred_element_type=F32)
                 for h, i in zip(heads, idx)]
        v_new = [(uws[i][:, :GDN_DV] - r[:c]).astype(BF16) for i, r in zip(idx, ws_qs)]
        o2 = [jnp.dot(qkd[i], vn, preferred_element_type=F32) for i, vn in zip(idx, v_new)]
        ds = [_dot_tn(kds[i], vn) for i, vn in zip(idx, v_new)]
        for h, i in zip(heads, idx):
            s_st[h] = gls[i] * s_st[h] + ds[h]
            o = ws_qs[h][c:] + o2[h]
            z = az_ref[rows(ci), h * GDN_DV:(h + 1) * GDN_DV].astype(F32)
            o_ref[rows(ci), h * GDN_DV:(h + 1) * GDN_DV] = (_rms(o, ng) * _silu(z)).astype(o_ref.dtype)
    for h in heads:
        s_scr[h] = s_st[h]

    @pl.when(t == pl.num_programs(1) - 1)
    def _():
        conv_out_ref[0] = x[tb - (GDN_CONV - 1):tb, :]
        s_out_ref[0] = s_scr[...]


def _gdn_prompt(a_qkv, a_z, small, small_t, conv_w, a_log, dt_bias, norm_g, bsz, seq, tb):
    nt = seq // tb
    row = lambda b, t: (b * nt + t, 0)
    col = lambda b, t: (0, b * nt + t)
    return pl.pallas_call(
        functools.partial(_gdn_prompt_kernel, tb=tb),
        grid=(bsz, nt),
        in_specs=[pl.BlockSpec((tb, GDN_QKV), row),
                  pl.BlockSpec((tb, GDN_V), row),
                  pl.BlockSpec((tb, LANE), row),
                  pl.BlockSpec((LANE, tb), col),
                  _const_spec((GDN_CONV, GDN_QKV)),
                  _const_spec((1, GDN_HEADS)),
                  _const_spec((1, GDN_HEADS)),
                  _const_spec((GDN_HEADS, 1)),
                  _const_spec((GDN_HEADS, 1)),
                  _const_spec((1, GDN_DV))],
        out_specs=[pl.BlockSpec((tb, GDN_V), row),
                   pl.BlockSpec((1, GDN_CONV - 1, GDN_QKV), lambda b, t: (b, 0, 0)),
                   pl.BlockSpec((1, GDN_HEADS, GDN_DK, GDN_DV), lambda b, t: (b, 0, 0, 0))],
        out_shape=[jax.ShapeDtypeStruct((bsz * seq, GDN_V), BF16),
                   jax.ShapeDtypeStruct((bsz, GDN_CONV - 1, GDN_QKV), F32),
                   jax.ShapeDtypeStruct((bsz, GDN_HEADS, GDN_DK, GDN_DV), F32)],
        scratch_shapes=[pltpu.VMEM((tb + 8, GDN_QKV), F32),
                        pltpu.VMEM((GDN_HEADS, GDN_DK, GDN_DV), F32)],
        compiler_params=_cparams(("parallel", "arbitrary")),
        name="gdn_prompt",
    )(a_qkv, a_z, small, small_t, conv_w, a_log.reshape(1, GDN_HEADS), dt_bias.reshape(1, GDN_HEADS),
      a_log.reshape(GDN_HEADS, 1), dt_bias.reshape(GDN_HEADS, 1), norm_g.reshape(1, GDN_DV))


def _mlstm_prompt_kernel(cqkv_ref, co_ref, small_ref, small_t_ref, bi_ref, bf_ref, bi_c_ref, bf_c_ref, ng_ref,
                         o_ref, c_out_ref, n_out_ref, m_out_ref, cn_scr, m_scr, *, tb):
    t = pl.program_id(1)
    c = REC_CHUNK
    nc = tb // c
    heads = range(ML_HEADS)
    pairs = [(ci, h) for ci in range(nc) for h in heads]

    @pl.when(t == 0)
    def _():
        cn_scr[...] = jnp.zeros_like(cn_scr)
        m_scr[...] = jnp.zeros_like(m_scr)

    sm = small_ref[...]
    ip_c = sm[:, SM_I:SM_I + ML_HEADS] + bi_ref[...]
    lf_c = -_softplus(-(sm[:, SM_F:SM_F + ML_HEADS] + bf_ref[...]))
    ip_r = small_t_ref[SM_I:SM_I + ML_HEADS, :] + bi_c_ref[...]
    lf_r = -_softplus(-(small_t_ref[SM_F:SM_F + ML_HEADS, :] + bf_c_ref[...]))
    cmask = _chunk_cumsum_mask(tb, c)
    f_c = _dot01(cmask, lf_c)
    f_r = _dot01_r(lf_r, cmask.T)

    incl, _, _ = _tri_masks(c)
    ng = ng_ref[...]
    rows = lambda ci: slice(ci * c, (ci + 1) * c)
    ones = jnp.ones((c, ML_DV), BF16)
    rep = lambda col: jnp.broadcast_to(col, (c, ML_DV))

    qs = [cqkv_ref[rows(ci), h * ML_DK:(h + 1) * ML_DK] for ci, h in pairs]
    ks = [cqkv_ref[rows(ci), ML_QK + h * ML_DK:ML_QK + (h + 1) * ML_DK].astype(F32) * (ML_DK ** -0.5)
          for ci, h in pairs]
    v1s = [jnp.concatenate([cqkv_ref[rows(ci), 2 * ML_QK + h * ML_DV:2 * ML_QK + (h + 1) * ML_DV], ones], axis=1)
           for ci, h in pairs]
    fcols = [f_c[rows(ci), h:h + 1] for ci, h in pairs]
    ds = [jnp.where(incl, f_c[rows(ci), h:h + 1] - f_r[h:h + 1, rows(ci)] + ip_r[h:h + 1, rows(ci)], -jnp.inf)
          for ci, h in pairs]
    dmaxs = [jnp.max(d, axis=-1, keepdims=True) for d in ds]
    qks = [_dot_nt(q, k) for q, k in zip(qs, ks)]
    p_locs = [jnp.exp(d - dm) * qk for d, dm, qk in zip(ds, dmaxs, qks)]
    pvs = [_dot(p, v1) for p, v1 in zip(p_locs, v1s)]
    wks = [jnp.exp(f[c - 1:c, :] - f + ip_c[rows(ci), h:h + 1] - dm[c - 1:c, :]) * k
           for (ci, h), f, dm, k in zip(pairs, fcols, dmaxs, ks)]
    kvs = [_dot_tn(wk, v1) for wk, v1 in zip(wks, v1s)]
    fbs = [rep(f) for f in fcols]
    dmbs = [rep(dm) for dm in dmaxs]

    cn_st = [cn_scr[h] for h in heads]
    m_st = [m_scr[h:h + 1, :] for h in heads]
    for ci in range(nc):
        idx = [ci * ML_HEADS + h for h in heads]
        qcs = [_dot(qs[i], cn_st[h]) for h, i in zip(heads, idx)]
        for h, i in zip(heads, idx):
            b_log = fbs[i] + m_st[h]
            m_r = jnp.maximum(b_log, dmbs[i])
            inter = jnp.exp(b_log - m_r)
            sc = jnp.exp(dmbs[i] - m_r)
            num = inter * qcs[h][:, :ML_DV] + sc * pvs[i][:, :ML_DV]
            den = inter * qcs[h][:, ML_DV:] + sc * pvs[i][:, ML_DV:]
            h_t = num / jnp.maximum(jnp.abs(den), jnp.exp(-m_r))
            m_new = m_r[c - 1:c, :]
            decay = jnp.exp(b_log[c - 1:c, :] - m_new)
            ksc = jnp.exp(dmbs[i][c - 1:c, :] - m_new)
            cn_st[h] = (jnp.concatenate([decay, decay], axis=1) * cn_st[h]
                        + jnp.concatenate([ksc, ksc], axis=1) * kvs[i])
            m_st[h] = m_new
            og = _sigmoid(co_ref[rows(ci), h * ML_DV:(h + 1) * ML_DV].astype(F32))
            o_ref[rows(ci), h * ML_DV:(h + 1) * ML_DV] = _rms(og * h_t, ng).astype(o_ref.dtype)
    for h in heads:
        cn_scr[h] = cn_st[h]
        m_scr[h:h + 1, :] = m_st[h]

    @pl.when(t == pl.num_programs(1) - 1)
    def _():
        c_out_ref[0] = cn_scr[:, :, 0:ML_DV]
        n_out_ref[0] = cn_scr[:, :, ML_DV:]
        m_out_ref[0] = m_scr[...]


def _mlstm_prompt(c_qkv, c_o, small, small_t, b_i, b_f, norm_g, bsz, seq, tb):
    nt = seq // tb
    row = lambda b, t: (b * nt + t, 0)
    col = lambda b, t: (0, b * nt + t)
    return pl.pallas_call(
        functools.partial(_mlstm_prompt_kernel, tb=tb),
        grid=(bsz, nt),
        in_specs=[pl.BlockSpec((tb, ML_QKV), row),
                  pl.BlockSpec((tb, ML_V), row),
                  pl.BlockSpec((tb, LANE), row),
                  pl.BlockSpec((LANE, tb), col),
                  _const_spec((1, ML_HEADS)),
                  _const_spec((1, ML_HEADS)),
                  _const_spec((ML_HEADS, 1)),
                  _const_spec((ML_HEADS, 1)),
                  _const_spec((1, ML_DV))],
        out_specs=[pl.BlockSpec((tb, ML_V), row),
                   pl.BlockSpec((1, ML_HEADS, ML_DK, ML_DV), lambda b, t: (b, 0, 0, 0)),
                   pl.BlockSpec((1, ML_HEADS, ML_DK, ML_DV), lambda b, t: (b, 0, 0, 0)),
                   pl.BlockSpec((1, ML_HEADS, LANE), lambda b, t: (b, 0, 0))],
        out_shape=[jax.ShapeDtypeStruct((bsz * seq, ML_V), BF16),
                   jax.ShapeDtypeStruct((bsz, ML_HEADS, ML_DK, ML_DV), F32),
                   jax.ShapeDtypeStruct((bsz, ML_HEADS, ML_DK, ML_DV), F32),
                   jax.ShapeDtypeStruct((bsz, ML_HEADS, LANE), F32)],
        scratch_shapes=[pltpu.VMEM((ML_HEADS, ML_DK, 2 * ML_DV), F32),
                        pltpu.VMEM((ML_HEADS, LANE), F32)],
        compiler_params=_cparams(("parallel", "arbitrary")),
        name="mlstm_prompt",
    )(c_qkv, c_o, small, small_t, b_i.reshape(1, ML_HEADS), b_f.reshape(1, ML_HEADS),
      b_i.reshape(ML_HEADS, 1), b_f.reshape(ML_HEADS, 1), norm_g.reshape(1, ML_DV))


def _rope_tables(pos):
    half = MLA_ROPE // 2
    inv_freq = ROPE_BASE ** (-jnp.arange(half, dtype=F32) / half)
    ang = pos.astype(F32)[:, None] * inv_freq[None, :]
    cos, sin = jnp.cos(ang), jnp.sin(ang)
    n = pos.shape[0]
    ones = jnp.ones((n, MLA_NOPE), F32)
    zeros = jnp.zeros((n, MLA_NOPE), F32)
    pad = jnp.zeros((n, LANE - MLA_QHEAD), F32)
    ctab = jnp.concatenate([ones, cos, cos, pad], axis=1)
    stab = jnp.concatenate([zeros, -sin, sin, pad], axis=1)
    return ctab, stab


def _mla_pre_kernel(cq_ref, ckv_ref, small_ref, ctab_ref, stab_ref, qg_ref, kvg_ref,
                    wuq_ref, wuk_ref, wuv_ref,
                    q_ref, k_ref, v_ref, ckv_out_ref, kr_out_ref):
    ctab = ctab_ref[...]
    stab = stab_ref[...]
    lane = lax.broadcasted_iota(jnp.int32, ctab.shape, 1)
    in_rope = (lane >= MLA_NOPE) & (lane < MLA_QHEAD)
    sm = small_ref[...]
    kr = jnp.where(in_rope, sm * ctab + _rope_swap(sm, MLA_NOPE) * stab, 0.0)
    kr_out_ref[0] = kr.T[MLA_NOPE:MLA_QHEAD, :]
    c_kv = _rms(ckv_ref[...], kvg_ref[...])
    ckv_out_ref[...] = c_kv
    c_kv_b = c_kv.astype(BF16)
    k_nope = jnp.dot(c_kv_b, wuk_ref[...], preferred_element_type=F32)
    v = jnp.dot(c_kv_b, wuv_ref[...], preferred_element_type=F32)
    v_t = v.T
    ones = jnp.ones((MLA_VDIM, v_t.shape[1]), v_ref.dtype)
    for h in range(MLA_HEADS):
        v_ref[0, h * HEAD_PAD:h * HEAD_PAD + MLA_VDIM, :] = v_t[h * MLA_VDIM:(h + 1) * MLA_VDIM, :].astype(v_ref.dtype)
        v_ref[0, h * HEAD_PAD + MLA_VDIM:(h + 1) * HEAD_PAD, :] = ones
    qn = _rms(cq_ref[...].astype(F32), qg_ref[...]).astype(BF16)
    q = jnp.dot(qn, wuq_ref[...], preferred_element_type=F32)
    q_sw = _rope_swap(q, MLA_NOPE)
    for h in range(MLA_HEADS):
        sl = slice(h * HEAD_PAD, (h + 1) * HEAD_PAD)
        q_ref[:, sl] = ((q[:, sl] * ctab + q_sw[:, sl] * stab) * (MLA_SCALE * LOG2E)).astype(q_ref.dtype)
        k_ref[:, sl] = (k_nope[:, sl] + kr).astype(k_ref.dtype)


def _pad_heads(w, head_dim):
    rows = w.shape[0]
    w = w.reshape(rows, MLA_HEADS, head_dim)
    return jnp.pad(w, ((0, 0), (0, 0), (0, HEAD_PAD - head_dim))).reshape(rows, MLA_HEADS * HEAD_PAD)


def _mla_pre(b_cq, b_ckv, small, ctab, stab, q_norm_g, kv_norm_g, w_uq_pad, w_uk_pad, w_uv, seq, tm):
    m = b_cq.shape[0]
    nt = seq // tm
    row = lambda i: (i, 0)
    tab = lambda i: (i % nt, 0)
    hp = MLA_HEADS * HEAD_PAD
    return pl.pallas_call(
        _mla_pre_kernel,
        grid=(m // tm,),
        in_specs=[pl.BlockSpec((tm, MLA_Q_LORA), row),
                  pl.BlockSpec((tm, MLA_KV_LORA), row),
                  pl.BlockSpec((tm, LANE), row),
                  pl.BlockSpec((tm, LANE), tab),
                  pl.BlockSpec((tm, LANE), tab),
                  _const_spec((1, MLA_Q_LORA)),
                  _const_spec((1, MLA_KV_LORA)),
                  _const_spec((MLA_Q_LORA, hp)),
                  _const_spec((MLA_KV_LORA, hp)),
                  _const_spec((MLA_KV_LORA, MLA_V))],
        out_specs=[pl.BlockSpec((tm, hp), row),
                   pl.BlockSpec((tm, hp), row),
                   pl.BlockSpec((1, hp, tm), lambda i: (i // nt, 0, i % nt)),
                   pl.BlockSpec((tm, MLA_KV_LORA), row),
                   pl.BlockSpec((1, MLA_ROPE, tm), lambda i: (i // nt, 0, i % nt))],
        out_shape=[jax.ShapeDtypeStruct((m, hp), BF16),
                   jax.ShapeDtypeStruct((m, hp), BF16),
                   jax.ShapeDtypeStruct((m // seq, hp, seq), BF16),
                   jax.ShapeDtypeStruct((m, MLA_KV_LORA), F32),
                   jax.ShapeDtypeStruct((m // seq, MLA_ROPE, seq), F32)],
        compiler_params=_cparams(("parallel",)),
        name="mla_pre",
    )(b_cq, b_ckv, small, ctab, stab, q_norm_g.reshape(1, -1), kv_norm_g.reshape(1, -1),
      w_uq_pad, w_uk_pad, w_uv)


def _flash_kernel(q_ref, k_ref, vt_ref, o_ref, m_scr, acc_scr, *, bq, bk):
    i = pl.program_id(1)
    n_full = (i * bq) // bk
    n_all = ((i + 1) * bq) // bk
    kpos = lax.broadcasted_iota(jnp.int32, (bk, bq), 0)
    qpos = i * bq + lax.broadcasted_iota(jnp.int32, (bk, bq), 1)
    hsl = lambda h: slice(h * HEAD_PAD, (h + 1) * HEAD_PAD)
    m_scr[...] = jnp.full_like(m_scr, -jnp.inf)
    acc_scr[...] = jnp.zeros_like(acc_scr)

    def step(j, masked):
        k0 = pl.multiple_of(j * bk, bk)
        keep = (kpos + j * bk <= qpos) if masked else None
        sts = [lax.dot_general(k_ref[pl.ds(k0, bk), hsl(h)], q_ref[:, hsl(h)], (((1,), (1,)), ((), ())),
                               preferred_element_type=F32) for h in range(MLA_HEADS)]
        for h in range(MLA_HEADS):
            st = jnp.where(keep, sts[h], -jnp.inf) if masked else sts[h]
            m_prev = m_scr[h]
            m_next = jnp.maximum(m_prev, jnp.max(st, axis=0, keepdims=True))
            p_t = jnp.exp2(st - m_next[0:1, :])
            alpha = jnp.exp2(m_prev - m_next)[0:1, :]
            acc_scr[h] = alpha * acc_scr[h] + jnp.dot(
                vt_ref[0, hsl(h), pl.ds(k0, bk)], p_t.astype(BF16), preferred_element_type=F32)
            m_scr[h] = m_next

    def body_full(j, carry):
        step(j, False)
        return carry

    def body_diag(j, carry):
        step(j, True)
        return carry

    lax.fori_loop(0, n_full, body_full, 0)
    lax.fori_loop(n_full, n_all, body_diag, 0)
    outs = []
    for h in range(MLA_HEADS):
        a = acc_scr[h]
        outs.append(a[0:MLA_VDIM, :] / a[MLA_VDIM:MLA_VDIM + 1, :])
    o_ref[...] = jnp.concatenate(outs, axis=0).T.astype(o_ref.dtype)


def _flash(q, k, v_t, bsz, seq, bq, bk):
    nq = seq // bq
    hp = MLA_HEADS * HEAD_PAD
    return pl.pallas_call(
        functools.partial(_flash_kernel, bq=bq, bk=bk),
        grid=(bsz, nq),
        in_specs=[pl.BlockSpec((bq, hp), lambda b, i: (b * nq + i, 0)),
                  pl.BlockSpec((seq, hp), lambda b, i: (b, 0)),
                  pl.BlockSpec((1, hp, seq), lambda b, i: (b, 0, 0))],
        out_specs=pl.BlockSpec((bq, MLA_V), lambda b, i: (b * nq + i, 0)),
        out_shape=jax.ShapeDtypeStruct((bsz * seq, MLA_V), BF16),
        scratch_shapes=[pltpu.VMEM((MLA_HEADS, 8, bq), F32), pltpu.VMEM((MLA_HEADS, HEAD_PAD, bq), F32)],
        compiler_params=_cparams(("parallel", "arbitrary")),
        name="mla_flash",
    )(q, k, v_t)


def _merge_kernel(x_ref, oa_ref, ob_ref, oc_ref, gates_ref, wa_ref, wb_ref, wc_ref, wo_ref, y_ref):
    g = gates_ref
    merged = (_sigmoid(g[:, 0:D_MODEL].astype(F32)) * _dot(oa_ref[...], wa_ref[...])
              + _sigmoid(g[:, D_MODEL:2 * D_MODEL].astype(F32)) * _dot(ob_ref[...], wb_ref[...])
              + _sigmoid(g[:, 2 * D_MODEL:].astype(F32)) * _dot(oc_ref[...], wc_ref[...]))
    y_ref[...] = x_ref[...] + _dot(merged, wo_ref[...])


def _merge(x, o_a, o_b, o_c, gates, w_a, w_b, w_c, w_o, tm):
    m = x.shape[0]
    row = lambda i: (i, 0)
    return pl.pallas_call(
        _merge_kernel,
        grid=(m // tm,),
        in_specs=[pl.BlockSpec((tm, D_MODEL), row),
                  pl.BlockSpec((tm, GDN_V), row),
                  pl.BlockSpec((tm, MLA_V), row),
                  pl.BlockSpec((tm, ML_V), row),
                  pl.BlockSpec((tm, 3 * D_MODEL), row),
                  _const_spec((GDN_V, D_MODEL)),
                  _const_spec((MLA_V, D_MODEL)),
                  _const_spec((ML_V, D_MODEL)),
                  _const_spec((D_MODEL, D_MODEL))],
        out_specs=pl.BlockSpec((tm, D_MODEL), row),
        out_shape=jax.ShapeDtypeStruct((m, D_MODEL), F32),
        compiler_params=_cparams(("parallel",)),
        name="merge",
    )(x, o_a, o_b, o_c, gates, w_a, w_b, w_c, w_o)


def _ffn_prompt_kernel(x_ref, g_ref, wup_ref, cw_ref, cb_ref, wdn_ref, gf_ref,
                       y_ref, conv_out_ref, gp_scr, *, tm, final):
    t = pl.program_id(1)

    @pl.when(t == 0)
    def _():
        gp_scr[0:8, :] = jnp.zeros((8, D_FF), F32)

    x = x_ref[...]
    hn = _rms(x, g_ref[...]).astype(BF16)
    u = jnp.dot(hn, wup_ref[:, 0:D_FF], preferred_element_type=F32)
    gt = jnp.dot(hn, wup_ref[:, D_FF:], preferred_element_type=F32)
    gp_scr[8:8 + tm, :] = gt
    cw = cw_ref[...]
    conv = gt * cw[2:3] + gp_scr[7:7 + tm, :] * cw[1:2] + gp_scr[6:6 + tm, :] * cw[0:1]
    gp_scr[0:8, :] = gt[tm - 8:tm, :]
    act = _silu(conv + cb_ref[...]) * u
    y = x + _dot(act, wdn_ref[...])
    if final:
        y = _rms(y, gf_ref[...])
    y_ref[...] = y

    @pl.when(t == pl.num_programs(1) - 1)
    def _():
        conv_out_ref[0] = gt[tm - (FFN_CONV - 1):tm, :]


def _ffn_prompt(x, norm_g, w_up, conv_w, conv_b, w_down, final_g, bsz, seq, tm, final):
    nt = seq // tm
    row = lambda b, t: (b * nt + t, 0)
    return pl.pallas_call(
        functools.partial(_ffn_prompt_kernel, tm=tm, final=final),
        grid=(bsz, nt),
        in_specs=[pl.BlockSpec((tm, D_MODEL), row),
                  _const_spec((1, D_MODEL)),
                  _const_spec((D_MODEL, 2 * D_FF)),
                  _const_spec((FFN_CONV, D_FF)),
                  _const_spec((1, D_FF)),
                  _const_spec((D_FF, D_MODEL)),
                  _const_spec((1, D_MODEL))],
        out_specs=[pl.BlockSpec((tm, D_MODEL), row),
                   pl.BlockSpec((1, FFN_CONV - 1, D_FF), lambda b, t: (b, 0, 0))],
        out_shape=[jax.ShapeDtypeStruct((bsz * seq, D_MODEL), F32),
                   jax.ShapeDtypeStruct((bsz, FFN_CONV - 1, D_FF), F32)],
        scratch_shapes=[pltpu.VMEM((tm + 8, D_FF), F32)],
        compiler_params=_cparams(("parallel", "arbitrary")),
        name="ffn_prompt",
    )(x, norm_g.reshape(1, -1), w_up, conv_w, conv_b.reshape(1, -1), w_down, final_g.reshape(1, -1))


def _ffn_sample_kernel(x_ref, hist_ref, g_ref, wup_ref, cw_ref, cb_ref, wdn_ref, gf_ref,
                       y_ref, hist_out_ref, *, final):
    x = x_ref[...]
    hn = _rms(x, g_ref[...]).astype(BF16)
    u = jnp.dot(hn, wup_ref[:, 0:D_FF], preferred_element_type=F32)
    gt = jnp.dot(hn, wup_ref[:, D_FF:], preferred_element_type=F32)
    cw = cw_ref[...]
    conv = hist_ref[0] * cw[0:1] + hist_ref[1] * cw[1:2] + gt * cw[2:3]
    act = _silu(conv + cb_ref[...]) * u
    y = x + _dot(act, wdn_ref[...])
    if final:
        y = _rms(y, gf_ref[...])
    y_ref[...] = y
    hist_out_ref[0] = hist_ref[1]
    hist_out_ref[1] = gt


def _ffn_sample(x, hist_t, norm_g, w_up, conv_w, conv_b, w_down, final_g, final):
    bs = x.shape[0]
    return pl.pallas_call(
        functools.partial(_ffn_sample_kernel, final=final),
        out_shape=[jax.ShapeDtypeStruct((bs, D_MODEL), F32),
                   jax.ShapeDtypeStruct((FFN_CONV - 1, bs, D_FF), F32)],
        compiler_params=pltpu.CompilerParams(vmem_limit_bytes=VMEM_LIMIT),
        name="ffn_sample",
    )(x, hist_t, norm_g.reshape(1, -1), w_up, conv_w, conv_b.reshape(1, -1), w_down, final_g.reshape(1, -1))


def _row_to_col(row, eye):
    n = eye.shape[0]
    return jnp.sum(jnp.where(eye, jnp.broadcast_to(row, (n, n)), 0.0), axis=1, keepdims=True)


def _rec_sample_kernel(aqkv_ref, hist_ref, az_ref, small_ref, convw_ref, alog_ref, dtb_ref, gng_ref,
                       s0_ref, cqkv_ref, co_ref, bi_ref, bf_ref, mng_ref, c0_ref, n0_ref, m0_ref,
                       oa_ref, hist_out_ref, s_out_ref, oc_ref, c_out_ref, n_out_ref, m_out_ref):
    _, _, eye128 = _tri_masks(LANE)
    lane = lax.broadcasted_iota(jnp.int32, (1, LANE), 1)
    sm = small_ref[0]
    x = aqkv_ref[0]
    hist = hist_ref[0]
    w = convw_ref[...]
    y = _silu(hist[0:1] * w[0:1] + hist[1:2] * w[1:2] + hist[2:3] * w[2:3] + x * w[3:4])
    hist_out_ref[0, 0:1, :] = hist[1:2]
    hist_out_ref[0, 1:2, :] = hist[2:3]
    hist_out_ref[0, 2:3, :] = x
    beta_row = _sigmoid(sm)
    decay_row = jnp.exp(-jnp.exp(alog_ref[...]) * _softplus(sm + dtb_ref[...]))
    gng = gng_ref[...]
    az = az_ref[0]
    for h in range(GDN_HEADS):
        q_row = _l2(y[:, h * GDN_DK:(h + 1) * GDN_DK]) * (GDN_DK ** -0.5)
        k_row = _l2(y[:, GDN_QK + h * GDN_DK:GDN_QK + (h + 1) * GDN_DK])
        v_row = y[:, 2 * GDN_QK + h * GDN_DV:2 * GDN_QK + (h + 1) * GDN_DV]
        beta = beta_row[:, SM_BETA + h:SM_BETA + h + 1]
        a = decay_row[:, SM_ALPHA + h:SM_ALPHA + h + 1]
        k_col = _row_to_col(k_row, eye128)
        q_col = _row_to_col(q_row, eye128)
        s_old = s0_ref[0, h]
        ks = jnp.sum(k_col * s_old, axis=0, keepdims=True)
        v_new = beta * (v_row - a * ks)
        qs = jnp.sum(q_col * s_old, axis=0, keepdims=True)
        qk = jnp.sum(q_row * k_row, axis=-1, keepdims=True)
        o = a * qs + qk * v_new
        s_out_ref[0, h] = a * s_old + k_col * v_new
        z = az[:, h * GDN_DV:(h + 1) * GDN_DV]
        oa_ref[0, :, h * GDN_DV:(h + 1) * GDN_DV] = _rms(o, gng) * _silu(z)

    cq = cqkv_ref[0]
    co = co_ref[0]
    ip_row = sm + bi_ref[...]
    lf_row = -_softplus(-(sm + bf_ref[...]))
    mng = mng_ref[...]
    m_prev = m0_ref[0]
    m_out = jnp.zeros((1, LANE), F32)
    for h in range(ML_HEADS):
        q_row = cq[:, h * LANE:(h + 1) * LANE]
        k_row = cq[:, (ML_HEADS + h) * LANE:(ML_HEADS + h + 1) * LANE] * (ML_DK ** -0.5)
        v_row = cq[:, 2 * ML_HEADS * LANE + h * ML_DV:2 * ML_HEADS * LANE + (h + 1) * ML_DV]
        ip = ip_row[:, SM_I + h:SM_I + h + 1]
        lf = lf_row[:, SM_F + h:SM_F + h + 1]
        m_old = m_prev[:, h:h + 1]
        c_old = c0_ref[0, h]
        n_old = n0_ref[0, h:h + 1, :]
        k_col = _row_to_col(k_row, eye128)[0:ML_DK]
        q_col = _row_to_col(q_row, eye128)[0:ML_DK]
        b_log = lf + m_old
        m_r = jnp.maximum(b_log, ip)
        inter = jnp.exp(b_log - m_r)
        w_key = jnp.exp(ip - m_r)
        qk = jnp.sum(q_row * k_row, axis=-1, keepdims=True)
        p = w_key * qk
        num = inter * jnp.sum(q_col * c_old, axis=0, keepdims=True) + p * v_row
        den = inter * jnp.sum(q_row * n_old, axis=-1, keepdims=True) + p
        h_t = num / jnp.maximum(jnp.abs(den), jnp.exp(-m_r))
        c_out_ref[0, h] = inter * c_old + (w_key * k_col) * v_row
        n_out_ref[0, h:h + 1, :] = inter * n_old + w_key * k_row
        m_out = jnp.where(lane == h, m_r, m_out)
        og = _sigmoid(co[:, h * ML_DV:(h + 1) * ML_DV])
        oc_ref[0, :, h * ML_DV:(h + 1) * ML_DV] = _rms(og * h_t, mng)
    m_out_ref[0] = m_out


def _rec_sample(a_qkv, hist, a_z, small, conv_w, alog_row, dtb_row, gdn_norm_g, s0,
                c_qkv, c_o, bi_row, bf_row, ml_norm_g, c0, n0_pad, m0_pad):
    bs = a_qkv.shape[0]
    r3 = lambda a: a.reshape(bs, 1, a.shape[-1])
    b3 = lambda n: pl.BlockSpec((1, 1, n), lambda b: (b, 0, 0))
    mlw = 2 * ML_HEADS * LANE + ML_V
    return pl.pallas_call(
        _rec_sample_kernel,
        grid=(bs,),
        in_specs=[b3(GDN_QKV),
                  pl.BlockSpec((1, GDN_CONV - 1, GDN_QKV), lambda b: (b, 0, 0)),
                  b3(GDN_V), b3(LANE),
                  _const_spec((GDN_CONV, GDN_QKV)),
                  _const_spec((1, LANE)), _const_spec((1, LANE)), _const_spec((1, GDN_DV)),
                  pl.BlockSpec((1, GDN_HEADS, GDN_DK, GDN_DV), lambda b: (b, 0, 0, 0)),
                  b3(mlw), b3(ML_V),
                  _const_spec((1, LANE)), _const_spec((1, LANE)), _const_spec((1, ML_DV)),
                  pl.BlockSpec((1, ML_HEADS, ML_DK, ML_DV), lambda b: (b, 0, 0, 0)),
                  pl.BlockSpec((1, ML_HEADS, LANE), lambda b: (b, 0, 0)),
                  b3(LANE)],
        out_specs=[b3(GDN_V),
                   pl.BlockSpec((1, GDN_CONV - 1, GDN_QKV), lambda b: (b, 0, 0)),
                   pl.BlockSpec((1, GDN_HEADS, GDN_DK, GDN_DV), lambda b: (b, 0, 0, 0)),
                   b3(ML_V),
                   pl.BlockSpec((1, ML_HEADS, ML_DK, ML_DV), lambda b: (b, 0, 0, 0)),
                   pl.BlockSpec((1, ML_HEADS, LANE), lambda b: (b, 0, 0)),
                   b3(LANE)],
        out_shape=[jax.ShapeDtypeStruct((bs, 1, GDN_V), F32),
                   jax.ShapeDtypeStruct((bs, GDN_CONV - 1, GDN_QKV), F32),
                   jax.ShapeDtypeStruct((bs, GDN_HEADS, GDN_DK, GDN_DV), F32),
                   jax.ShapeDtypeStruct((bs, 1, ML_V), F32),
                   jax.ShapeDtypeStruct((bs, ML_HEADS, ML_DK, ML_DV), F32),
                   jax.ShapeDtypeStruct((bs, ML_HEADS, LANE), F32),
                   jax.ShapeDtypeStruct((bs, 1, LANE), F32)],
        compiler_params=_cparams(("parallel",)),
        name="rec_sample",
    )(r3(a_qkv), hist, r3(a_z), r3(small), conv_w, alog_row, dtb_row, gdn_norm_g.reshape(1, -1), s0,
      r3(c_qkv), r3(c_o), bi_row, bf_row, ml_norm_g.reshape(1, -1), c0, n0_pad, r3(m0_pad))


def _mla_sample_pre_kernel(cq_ref, ckv_ref, small_ref, ctab_ref, stab_ref, qg_ref, kvg_ref,
                           wuq_ref, wukt_ref, qlat_ref, qrope_ref, ckv_out_ref, kr_out_ref):
    ctab = ctab_ref[...]
    stab = stab_ref[...]
    lane = lax.broadcasted_iota(jnp.int32, (1, LANE), 1)
    in_rope = (lane >= MLA_NOPE) & (lane < MLA_QHEAD)
    sm = small_ref[...]
    kr_out_ref[...] = jnp.where(in_rope, sm * ctab + _rope_swap(sm, MLA_NOPE) * stab, 0.0)
    ckv_out_ref[...] = _rms(ckv_ref[...], kvg_ref[...])
    qn = _rms(cq_ref[...], qg_ref[...])
    q = _dot(qn, wuq_ref[...])
    q_sw = _rope_swap(q, MLA_NOPE)
    for h in range(MLA_HEADS):
        sl = slice(h * HEAD_PAD, (h + 1) * HEAD_PAD)
        qh = q[:, sl]
        qrope_ref[:, sl] = jnp.where(in_rope, qh * ctab + q_sw[:, sl] * stab, 0.0)
        qlat_ref[:, sl] = _dot(jnp.where(lane < MLA_NOPE, qh, 0.0), wukt_ref[h])


def _mla_sample_pre(b_cq, b_ckv, small, ctab, stab, q_norm_g, kv_norm_g, w_uq_pad, w_uk_t):
    bs = b_cq.shape[0]
    hp = MLA_HEADS * HEAD_PAD
    return pl.pallas_call(
        _mla_sample_pre_kernel,
        out_shape=[jax.ShapeDtypeStruct((bs, hp), F32),
                   jax.ShapeDtypeStruct((bs, hp), F32),
                   jax.ShapeDtypeStruct((bs, MLA_KV_LORA), F32),
                   jax.ShapeDtypeStruct((bs, LANE), F32)],
        compiler_params=pltpu.CompilerParams(vmem_limit_bytes=VMEM_LIMIT),
        name="mla_sample_pre",
    )(b_cq, b_ckv, small, ctab, stab, q_norm_g.reshape(1, -1), kv_norm_g.reshape(1, -1), w_uq_pad, w_uk_t)


def _mla_sample_kernel(pt_ref, qlat_ref, qrope_ref, ckv_ref, kr_ref, lat_hbm, krt_hbm, ctx_ref,
                       lat_buf, krt_buf, sem, *, layer, n_pages):
    b = pl.program_id(0)
    slot = b % 2

    def page_copies(row, buf_slot, g):
        page = pt_ref[row, g]
        off = pl.multiple_of(g * PAGE_SIZE, PAGE_SIZE)
        return (pltpu.make_async_copy(lat_hbm.at[layer, page], lat_buf.at[buf_slot, pl.ds(off, PAGE_SIZE), :],
                                      sem.at[0, buf_slot]),
                pltpu.make_async_copy(krt_hbm.at[layer, page], krt_buf.at[buf_slot, :, pl.ds(off, PAGE_SIZE)],
                                      sem.at[1, buf_slot]))

    def start_row(row, buf_slot):
        def body(g, carry):
            for cp in page_copies(row, buf_slot, g):
                cp.start()
            return carry
        lax.fori_loop(0, n_pages, body, 0, unroll=8)

    def wait_row(buf_slot):
        pltpu.make_async_copy(lat_buf.at[buf_slot], lat_buf.at[buf_slot], sem.at[0, buf_slot]).wait()
        pltpu.make_async_copy(krt_buf.at[buf_slot], krt_buf.at[buf_slot], sem.at[1, buf_slot]).wait()

    @pl.when(b == 0)
    def _():
        start_row(0, 0)

    @pl.when(b + 1 < pl.num_programs(0))
    def _():
        start_row(b + 1, 1 - slot)

    wait_row(slot)

    q_lat = qlat_ref[0].astype(BF16)
    q_rope = qrope_ref[0][:, MLA_NOPE:MLA_QHEAD].astype(BF16)
    lat = lat_buf[slot].astype(BF16)
    s = (_dot_nt(q_lat, lat) + _dot(q_rope, krt_buf[slot])) * MLA_SCALE
    c_new = ckv_ref[0].astype(BF16).astype(F32)
    kr_new = kr_ref[0][:, MLA_NOPE:MLA_QHEAD].astype(BF16).astype(F32)
    s_new = (jnp.sum(q_lat.astype(F32) * c_new, axis=-1, keepdims=True)
             + jnp.sum(q_rope.astype(F32) * kr_new, axis=-1, keepdims=True)) * MLA_SCALE
    m = jnp.maximum(jnp.max(s, axis=-1, keepdims=True), s_new)
    p = jnp.exp(s - m)
    p_new = jnp.exp(s_new - m)
    denom = jnp.sum(p, axis=-1, keepdims=True) + p_new
    ctx_ref[0] = (_dot(p, lat) + p_new.astype(BF16).astype(F32) * c_new) / denom


def _mla_sample(page_table, q_lat, q_rope, c_kv, kr, cache_lat, cache_krt, layer):
    bs, n_pages = page_table.shape
    past = n_pages * PAGE_SIZE
    hp3 = lambda a: a.reshape(bs, MLA_HEADS, HEAD_PAD)
    per_b = lambda n: pl.BlockSpec((1, n, LANE), lambda b, pt: (b, 0, 0))
    grid_spec = pltpu.PrefetchScalarGridSpec(
        num_scalar_prefetch=1,
        grid=(bs,),
        in_specs=[per_b(MLA_HEADS), per_b(MLA_HEADS), per_b(1), per_b(1),
                  pl.BlockSpec(memory_space=pl.ANY), pl.BlockSpec(memory_space=pl.ANY)],
        out_specs=per_b(MLA_HEADS),
        scratch_shapes=[pltpu.VMEM((2, past, MLA_KV_LORA), F32),
                        pltpu.VMEM((2, MLA_ROPE, past), F32),
                        pltpu.SemaphoreType.DMA((2, 2))],
    )
    return pl.pallas_call(
        functools.partial(_mla_sample_kernel, layer=layer, n_pages=n_pages),
        grid_spec=grid_spec,
        out_shape=jax.ShapeDtypeStruct((bs, MLA_HEADS, LANE), F32),
        compiler_params=_cparams(("arbitrary",)),
        name="mla_sample",
    )(page_table, hp3(q_lat), hp3(q_rope), c_kv.reshape(bs, 1, LANE), kr.reshape(bs, 1, LANE),
      cache_lat, cache_krt)


def _ctx_up_kernel(ctx_ref, wuv_ref, o_ref):
    o_ref[...] = _dot(ctx_ref[...], wuv_ref[...])


def _ctx_up(ctx, w_uv_bd):
    return pl.pallas_call(
        _ctx_up_kernel,
        out_shape=jax.ShapeDtypeStruct((ctx.shape[0], MLA_V), F32),
        name="ctx_up",
    )(ctx, w_uv_bd)


def _lane_row(vals, lane0):
    return jnp.zeros((1, LANE), F32).at[0, lane0:lane0 + vals.shape[0]].set(vals.astype(F32))


def _pick(n, pref):
    t = min(n, pref)
    while n % t:
        t //= 2
    return t


def kernel(x_prompt, x_sample, cache_kv_latent, cache_k_rope, state_gdn_conv, state_gdn_S,
           state_mlstm_C, state_mlstm_n, state_mlstm_m, state_ffn_conv, page_table,
           norm1_g, w_in, gdn_conv_w, gdn_A_log, gdn_dt_bias, gdn_norm_g,
           mla_q_norm_g, mla_w_uq, mla_kv_norm_g, mla_w_uk, mla_w_uv,
           ml_b_i, ml_b_f, ml_norm_g, w_branch_a, w_branch_b, w_branch_c, w_out,
           norm2_g, ffn_w_up, ffn_conv_w, ffn_conv_b, ffn_w_down, final_norm_g):
    bp, seq = x_prompt.shape[:2]
    bs, dec = x_sample.shape[:2]
    assert dec == 1 and seq % REC_CHUNK == 0
    depth = w_in.shape[0]
    n_pages = page_table.shape[1]
    past_len = n_pages * PAGE_SIZE
    mp = bp * seq

    tm = _pick(seq, 512)
    tb = _pick(seq, 512)
    bq = _pick(seq, 256)

    ctab_p, stab_p = _rope_tables(jnp.arange(seq))
    ctab_s, stab_s = _rope_tables(jnp.full((1,), past_len))
    cache_krt = jnp.swapaxes(cache_k_rope, 2, 3)

    xp = x_prompt.reshape(mp, D_MODEL)
    xs = x_sample.reshape(bs, D_MODEL)
    out_p = [[] for _ in range(8)]
    out_s = [[] for _ in range(8)]

    for l in range(depth):
        last = l == depth - 1
        w_p, w_small_t = _pack_w_in(w_in[l], False)
        w_s, _ = _pack_w_in(w_in[l], True)
        w_uq_pad = _pad_heads(mla_w_uq[l], MLA_QHEAD).astype(BF16)
        w_uk_pad = _pad_heads(mla_w_uk[l], MLA_NOPE).astype(BF16)
        w_uv = mla_w_uv[l].astype(BF16)
        w_a, w_b, w_c, w_o = (w_branch_a[l].astype(BF16), w_branch_b[l].astype(BF16),
                              w_branch_c[l].astype(BF16), w_out[l].astype(BF16))
        w_up, w_dn = ffn_w_up[l].astype(BF16), ffn_w_down[l].astype(BF16)

        a_qkv, a_z, b_cq, b_ckv, c_qkv, c_o, gates, small, small_t = _in_proj(
            xp, norm1_g[l], w_p, PROMPT_GROUPS, tm, w_small_t)
        o_a, p_gconv, p_gs = _gdn_prompt(a_qkv, a_z, small, small_t, gdn_conv_w[l], gdn_A_log[l], gdn_dt_bias[l],
                                         gdn_norm_g[l], bp, seq, tb)
        q_pad, k_pad, v_all, p_ckv, p_kr = _mla_pre(b_cq, b_ckv, small, ctab_p, stab_p, mla_q_norm_g[l],
                                                    mla_kv_norm_g[l], w_uq_pad, w_uk_pad, w_uv, seq, tm)
        o_b = _flash(q_pad, k_pad, v_all, bp, seq, bq, bq)
        o_c, p_c, p_n, p_m = _mlstm_prompt(c_qkv, c_o, small, small_t, ml_b_i[l], ml_b_f[l], ml_norm_g[l], bp, seq, tb)
        xp = _merge(xp, o_a, o_b, o_c, gates, w_a, w_b, w_c, w_o, tm)
        xp, p_fconv = _ffn_prompt(xp, norm2_g[l], w_up, ffn_conv_w[l], ffn_conv_b[l], w_dn, final_norm_g,
                                  bp, seq, tm, last)
        for lst, a in zip(out_p, (p_ckv.reshape(bp, seq, MLA_KV_LORA), jnp.swapaxes(p_kr, 1, 2),
                                  p_gconv, p_gs, p_c, p_n[..., 0], p_m[..., 0], p_fconv)):
            lst.append(a)

        a_qkv, a_z, b_cq, b_ckv, c_qkv, c_o, gates, small = _in_proj(xs, norm1_g[l], w_s, SAMPLE_GROUPS, bs)
        n0_pad = jnp.pad(state_mlstm_n[l], ((0, 0), (0, 0), (0, LANE - ML_DK)))
        m0_pad = jnp.pad(state_mlstm_m[l], ((0, 0), (0, LANE - ML_HEADS)))
        o_a, s_gconv, s_gs, o_c, s_c, s_n, s_m = _rec_sample(
            a_qkv, state_gdn_conv[l], a_z, small, gdn_conv_w[l],
            _lane_row(gdn_A_log[l], SM_ALPHA), _lane_row(gdn_dt_bias[l], SM_ALPHA), gdn_norm_g[l],
            state_gdn_S[l], c_qkv, c_o, _lane_row(ml_b_i[l], SM_I), _lane_row(ml_b_f[l], SM_F),
            ml_norm_g[l], state_mlstm_C[l], n0_pad, m0_pad)
        w_uk_t = jnp.pad(jnp.transpose(mla_w_uk[l].reshape(MLA_KV_LORA, MLA_HEADS, MLA_NOPE), (1, 2, 0)),
                         ((0, 0), (0, HEAD_PAD - MLA_NOPE), (0, 0))).astype(BF16)
        q_lat, q_rope, s_ckv, kr_s = _mla_sample_pre(b_cq, b_ckv, small, ctab_s, stab_s, mla_q_norm_g[l],
                                                      mla_kv_norm_g[l], w_uq_pad, w_uk_t)
        ctx = _mla_sample(page_table, q_lat, q_rope, s_ckv, kr_s, cache_kv_latent, cache_krt, l)
        w_uv_bd = (jnp.eye(MLA_HEADS, dtype=F32)[:, None, :, None]
                   * jnp.transpose(mla_w_uv[l].reshape(MLA_KV_LORA, MLA_HEADS, MLA_VDIM), (1, 0, 2))[:, :, None, :]
                   ).reshape(MLA_HEADS * MLA_KV_LORA, MLA_V).astype(BF16)
        o_b = _ctx_up(ctx.reshape(bs, MLA_HEADS * LANE), w_uv_bd)
        xs = _merge(xs, o_a.reshape(bs, GDN_V), o_b, o_c.reshape(bs, ML_V), gates, w_a, w_b, w_c, w_o, bs)
        xs, s_fconv_t = _ffn_sample(xs, jnp.swapaxes(state_ffn_conv[l], 0, 1), norm2_g[l], w_up, ffn_conv_w[l],
                                    ffn_conv_b[l], w_dn, final_norm_g, last)
        for lst, a in zip(out_s, (s_ckv.reshape(bs, 1, MLA_KV_LORA),
                                  kr_s[:, MLA_NOPE:MLA_QHEAD].reshape(bs, 1, MLA_ROPE),
                                  s_gconv, s_gs, s_c, s_n[:, :, :ML_DK], s_m.reshape(bs, LANE)[:, :ML_HEADS],
                                  jnp.swapaxes(s_fconv_t, 0, 1))):
            lst.append(a)

    y_prompt = xp.reshape(bp, seq, D_MODEL)
    y_sample = xs.reshape(bs, 1, D_MODEL)
    return (y_prompt, y_sample) + tuple(jnp.stack(a) for a in out_p) + tuple(jnp.stack(a) for a in out_s)
```

```python
import functools
import math

import jax
import jax.numpy as jnp
import numpy as np
from jax import lax
from jax.experimental import pallas as pl
from jax.experimental.pallas import tpu as pltpu

F32 = jnp.float32
BF16 = jnp.bfloat16

D_MODEL = 1024
PAGE_SIZE = 128
GDN_HEADS, GDN_DK, GDN_DV, GDN_CONV = 4, 128, 128, 4
MLA_HEADS, MLA_Q_LORA, MLA_KV_LORA, MLA_NOPE, MLA_ROPE, MLA_VDIM = 8, 256, 128, 64, 32, 64
ROPE_BASE = 10000.0
ML_HEADS, ML_DK, ML_DV = 4, 64, 128
CHUNK = 64
REC_CHUNK = 64
ML_CHUNK = 128
D_FF = 2816
FFN_CONV = 3
NORM_EPS = 1e-6

GDN_QK = GDN_HEADS * GDN_DK
GDN_V = GDN_HEADS * GDN_DV
GDN_QKV = 2 * GDN_QK + GDN_V
MLA_QHEAD = MLA_NOPE + MLA_ROPE
MLA_V = MLA_HEADS * MLA_VDIM
MLA_SCALE = MLA_QHEAD ** -0.5
ML_QK = ML_HEADS * ML_DK
ML_V = ML_HEADS * ML_DV
ML_QKV = 2 * ML_QK + ML_V
IN_SIZES = (GDN_QKV, GDN_HEADS, GDN_HEADS, GDN_V, MLA_Q_LORA, MLA_KV_LORA, MLA_ROPE,
            ML_QKV, ML_HEADS, ML_HEADS, ML_V, 3 * D_MODEL)

LANE = 128
HEAD_PAD = 128
SM_BETA, SM_ALPHA, SM_I, SM_F, SM_KR = 0, 8, 16, 24, 64
VMEM_LIMIT = 56 * 1024 * 1024
LOG2E = math.log2(math.e)


def _cparams(sem):
    return pltpu.CompilerParams(dimension_semantics=sem, vmem_limit_bytes=VMEM_LIMIT)


def _const_spec(shape):
    nd = len(shape)
    return pl.BlockSpec(shape, lambda *_: (0,) * nd, pipeline_mode=pl.Buffered(1))


def _dot(a, b):
    return jnp.dot(a.astype(BF16), b.astype(BF16), preferred_element_type=F32)


def _dot_nt(a, b):
    return lax.dot_general(a.astype(BF16), b.astype(BF16), (((1,), (1,)), ((), ())),
                           preferred_element_type=F32)


def _dot_tn(a, b):
    return lax.dot_general(a.astype(BF16), b.astype(BF16), (((0,), (0,)), ((), ())),
                           preferred_element_type=F32)


def _dot01(mask_bf16, x):
    x1 = x.astype(BF16)
    r1 = x - x1.astype(F32)
    x2 = r1.astype(BF16)
    x3 = (r1 - x2.astype(F32)).astype(BF16)
    d = functools.partial(jnp.dot, preferred_element_type=F32)
    return d(mask_bf16, x1) + d(mask_bf16, x2) + d(mask_bf16, x3)


def _dot01_r(x, mask_bf16):
    x1 = x.astype(BF16)
    r1 = x - x1.astype(F32)
    x2 = r1.astype(BF16)
    x3 = (r1 - x2.astype(F32)).astype(BF16)
    d = functools.partial(jnp.dot, preferred_element_type=F32)
    return d(x1, mask_bf16) + d(x2, mask_bf16) + d(x3, mask_bf16)


def _rms(x, g):
    return x * lax.rsqrt(jnp.mean(x * x, axis=-1, keepdims=True) + NORM_EPS) * g


def _l2(x):
    return x * lax.rsqrt(jnp.sum(x * x, axis=-1, keepdims=True) + NORM_EPS)


def _sigmoid(x):
    return jax.nn.sigmoid(x)


def _silu(x):
    return x * jax.nn.sigmoid(x)


def _softplus(x):
    return jnp.maximum(x, 0.0) + jnp.log1p(jnp.exp(-jnp.abs(x)))


def _tri_masks(n):
    r = lax.broadcasted_iota(jnp.int32, (n, n), 0)
    c = lax.broadcasted_iota(jnp.int32, (n, n), 1)
    return r >= c, r > c, r == c


def _chunk_cumsum_mask(tb, c):
    r = lax.broadcasted_iota(jnp.int32, (tb, tb), 0)
    q = lax.broadcasted_iota(jnp.int32, (tb, tb), 1)
    return jnp.where((r // c == q // c) & (r >= q), 1.0, 0.0).astype(BF16)


def _neumann_inverse_many(a_list, eye_f):
    n = eye_f.shape[0]
    ps = [-a for a in a_list]
    ts = [eye_f + p for p in ps]
    qs = [_dot(p, p) for p in ps]
    covered = 2
    while 2 * covered < n:
        rs = [_dot(q, jnp.concatenate([t, q], axis=1)) for t, q in zip(ts, qs)]
        ts = [t + r[:, :n] for t, r in zip(ts, rs)]
        qs = [r[:, n:] for r in rs]
        covered *= 2
    return [t + _dot(q, t) for t, q in zip(ts, qs)]


def _rope_swap(x, lane0):
    w = x.shape[-1]
    half = MLA_ROPE // 2
    lane = lax.broadcasted_iota(jnp.int32, x.shape, x.ndim - 1) % LANE
    left = pltpu.roll(x, w - half, x.ndim - 1)
    right = pltpu.roll(x, half, x.ndim - 1)
    first = (lane >= lane0) & (lane < lane0 + half)
    second = (lane >= lane0 + half) & (lane < lane0 + 2 * half)
    return jnp.where(first, left, jnp.where(second, right, 0.0))


def _in_proj_kernel(x_ref, g_ref, w_ref, *rest, with_small_t):
    xn = _rms(x_ref[...], g_ref[...]).astype(BF16)
    out_refs = rest
    if with_small_t:
        wst_ref, out_refs, small_t_ref = rest[0], rest[1:-1], rest[-1]
        small_t_ref[...] = lax.dot_general(wst_ref[...], xn, (((1,), (1,)), ((), ())), preferred_element_type=F32)
    off = 0
    for ref in out_refs:
        n = ref.shape[-1]
        ref[...] = jnp.dot(xn, w_ref[:, off:off + n], preferred_element_type=F32).astype(ref.dtype)
        off += n


def _in_proj(x, g, w_packed, groups, tm, w_small_t=None):
    m = x.shape[0]
    width = w_packed.shape[1]
    assert sum(n for n, _ in groups) == width and m % tm == 0
    with_t = w_small_t is not None
    in_specs = [pl.BlockSpec((tm, D_MODEL), lambda i: (i, 0)),
                _const_spec((1, D_MODEL)),
                _const_spec((D_MODEL, width))]
    out_specs = [pl.BlockSpec((tm, n), lambda i: (i, 0)) for n, _ in groups]
    out_shape = [jax.ShapeDtypeStruct((m, n), dt) for n, dt in groups]
    args = [x, g.reshape(1, D_MODEL), w_packed]
    if with_t:
        in_specs.append(_const_spec((LANE, D_MODEL)))
        out_specs.append(pl.BlockSpec((LANE, tm), lambda i: (0, i)))
        out_shape.append(jax.ShapeDtypeStruct((LANE, m), F32))
        args.append(w_small_t)
    return pl.pallas_call(
        functools.partial(_in_proj_kernel, with_small_t=with_t),
        grid=(m // tm,),
        in_specs=in_specs,
        out_specs=out_specs,
        out_shape=out_shape,
        compiler_params=_cparams(("parallel",)),
        name="in_proj",
    )(*args)


def _small_cols(a_beta, a_alpha, b_kr, c_i, c_f):
    small = jnp.zeros((D_MODEL, LANE), F32)
    small = small.at[:, SM_BETA:SM_BETA + GDN_HEADS].set(a_beta)
    small = small.at[:, SM_ALPHA:SM_ALPHA + GDN_HEADS].set(a_alpha)
    small = small.at[:, SM_I:SM_I + ML_HEADS].set(c_i)
    small = small.at[:, SM_F:SM_F + ML_HEADS].set(c_f)
    small = small.at[:, SM_KR:SM_KR + MLA_ROPE].set(b_kr)
    return small


PROMPT_GROUPS = ((GDN_QKV, F32), (GDN_V, BF16), (MLA_Q_LORA, BF16), (MLA_KV_LORA, F32),
                 (ML_QKV, BF16), (ML_V, BF16), (3 * D_MODEL, BF16), (LANE, F32))
SAMPLE_GROUPS = ((GDN_QKV, F32), (GDN_V, F32), (MLA_Q_LORA, F32), (MLA_KV_LORA, F32),
                 (2 * ML_HEADS * LANE + ML_V, F32), (ML_V, F32), (3 * D_MODEL, F32), (LANE, F32))


def _pack_w_in(w_in, pad_ml_heads):
    (a_qkv, a_beta, a_alpha, a_z, b_cq, b_ckv, b_kr,
     c_qkv, c_i, c_f, c_o, gates) = jnp.split(w_in, np.cumsum(IN_SIZES)[:-1].tolist(), axis=-1)
    if pad_ml_heads:
        qk = c_qkv[:, :2 * ML_QK].reshape(D_MODEL, 2 * ML_HEADS, ML_DK)
        qk = jnp.pad(qk, ((0, 0), (0, 0), (0, LANE - ML_DK))).reshape(D_MODEL, 2 * ML_HEADS * LANE)
        c_qkv = jnp.concatenate([qk, c_qkv[:, 2 * ML_QK:]], axis=1)
    small = _small_cols(a_beta, a_alpha, b_kr, c_i, c_f)
    packed = jnp.concatenate([a_qkv, a_z, b_cq, b_ckv, c_qkv, c_o, gates, small], axis=1).astype(BF16)
    return packed, small.T.astype(BF16)


def _gdn_prompt_kernel(aqkv_ref, az_ref, small_ref, small_t_ref, convw_ref, alog_ref, dtb_ref,
                       alog_c_ref, dtb_c_ref, ng_ref,
                       o_ref, conv_out_ref, s_out_ref, xp_scr, s_scr, *, tb):
    t = pl.program_id(1)
    c = REC_CHUNK
    nc = tb // c
    heads = range(GDN_HEADS)
    pairs = [(ci, h) for ci in range(nc) for h in heads]

    @pl.when(t == 0)
    def _():
        s_scr[...] = jnp.zeros_like(s_scr)
        xp_scr[0:8, :] = jnp.zeros((8, GDN_QKV), F32)

    x = aqkv_ref[...]
    xp_scr[8:8 + tb, :] = x
    w = convw_ref[...]
    y = (x * w[3:4] + xp_scr[7:7 + tb, :] * w[2:3]
         + xp_scr[6:6 + tb, :] * w[1:2] + xp_scr[5:5 + tb, :] * w[0:1])
    xp_scr[0:8, :] = x[tb - 8:tb, :]
    y = _silu(y)
    sm = small_ref[...]
    beta = _sigmoid(sm[:, SM_BETA:SM_BETA + GDN_HEADS])
    g_c = -jnp.exp(alog_ref[...]) * _softplus(sm[:, SM_ALPHA:SM_ALPHA + GDN_HEADS] + dtb_ref[...])
    g_r = -jnp.exp(alog_c_ref[...]) * _softplus(small_t_ref[SM_ALPHA:SM_ALPHA + GDN_HEADS, :] + dtb_c_ref[...])
    cmask = _chunk_cumsum_mask(tb, c)
    big_c = _dot01(cmask, g_c)
    big_r = _dot01_r(g_r, cmask.T)

    incl, strict, eye = _tri_masks(c)
    eye_f = jnp.where(eye, 1.0, 0.0).astype(F32)
    ng = ng_ref[...]
    rows = lambda ci: slice(ci * c, (ci + 1) * c)

    qs = [_l2(y[rows(ci), h * GDN_DK:(h + 1) * GDN_DK]) * (GDN_DK ** -0.5) for ci, h in pairs]
    ks = [_l2(y[rows(ci), GDN_QK + h * GDN_DK:GDN_QK + (h + 1) * GDN_DK]) for ci, h in pairs]
    vs = [y[rows(ci), 2 * GDN_QK + h * GDN_DV:2 * GDN_QK + (h + 1) * GDN_DV] for ci, h in pairs]
    bcols = [beta[rows(ci), h:h + 1] for ci, h in pairs]
    gcols = [big_c[rows(ci), h:h + 1] for ci, h in pairs]
    decs = [jnp.exp(jnp.where(incl, big_c[rows(ci), h:h + 1] - big_r[h:h + 1, rows(ci)], -jnp.inf))
            for ci, h in pairs]
    qkk = [_dot_nt(jnp.concatenate([q, k], axis=0), k) for q, k in zip(qs, ks)]
    a_list = [b * m[c:] * jnp.where(strict, d, 0.0) for b, m, d in zip(bcols, qkk, decs)]
    tinvs = _neumann_inverse_many(a_list, eye_f)
    e_gs = [jnp.exp(g) for g in gcols]
    uws = [_dot(ti, jnp.concatenate([b * v, (b * e) * k], axis=1))
           for ti, b, v, e, k in zip(tinvs, bcols, vs, e_gs, ks)]
    wqs = [jnp.concatenate([uw[:, GDN_DV:], q * e], axis=0).astype(BF16) for uw, q, e in zip(uws, qs, e_gs)]
    qkd = [(m[:c] * d).astype(BF16) for m, d in zip(qkk, decs)]
    kds = [(k * jnp.exp(g[c - 1:c, :] - g)).astype(BF16) for k, g in zip(ks, gcols)]
    gls = [jnp.exp(jnp.broadcast_to(g[c - 1:c, :], (1, GDN_DV))) for g in gcols]

    s_st = [s_scr[h] for h in heads]
    for ci in range(nc):
        idx = [ci * GDN_HEADS + h for h in heads]
        ws_qs = [jnp.dot(wqs[i], s_st[h].astype(BF16), preferred_element_type=F32)
                 for h, i in zip(heads, idx)]
        v_new = [(uws[i][:, :GDN_DV] - r[:c]).astype(BF16) for i, r in zip(idx, ws_qs)]
        o2 = [jnp.dot(qkd[i], vn, preferred_element_type=F32) for i, vn in zip(idx, v_new)]
        ds = [_dot_tn(kds[i], vn) for i, vn in zip(idx, v_new)]
        for h, i in zip(heads, idx):
            s_st[h] = gls[i] * s_st[h] + ds[h]
            o = ws_qs[h][c:] + o2[h]
            z = az_ref[rows(ci), h * GDN_DV:(h + 1) * GDN_DV].astype(F32)
            o_ref[rows(ci), h * GDN_DV:(h + 1) * GDN_DV] = (_rms(o, ng) * _silu(z)).astype(o_ref.dtype)
    for h in heads:
        s_scr[h] = s_st[h]

    @pl.when(t == pl.num_programs(1) - 1)
    def _():
        conv_out_ref[0] = x[tb - (GDN_CONV - 1):tb, :]
        s_out_ref[0] = s_scr[...]


def _gdn_prompt(a_qkv, a_z, small, small_t, conv_w, a_log, dt_bias, norm_g, bsz, seq, tb):
    nt = seq // tb
    row = lambda b, t: (b * nt + t, 0)
    col = lambda b, t: (0, b * nt + t)
    return pl.pallas_call(
        functools.partial(_gdn_prompt_kernel, tb=tb),
        grid=(bsz, nt),
        in_specs=[pl.BlockSpec((tb, GDN_QKV), row),
                  pl.BlockSpec((tb, GDN_V), row),
                  pl.BlockSpec((tb, LANE), row),
                  pl.BlockSpec((LANE, tb), col),
                  _const_spec((GDN_CONV, GDN_QKV)),
                  _const_spec((1, GDN_HEADS)),
                  _const_spec((1, GDN_HEADS)),
                  _const_spec((GDN_HEADS, 1)),
                  _const_spec((GDN_HEADS, 1)),
                  _const_spec((1, GDN_DV))],
        out_specs=[pl.BlockSpec((tb, GDN_V), row),
                   pl.BlockSpec((1, GDN_CONV - 1, GDN_QKV), lambda b, t: (b, 0, 0)),
                   pl.BlockSpec((1, GDN_HEADS, GDN_DK, GDN_DV), lambda b, t: (b, 0, 0, 0))],
        out_shape=[jax.ShapeDtypeStruct((bsz * seq, GDN_V), BF16),
                   jax.ShapeDtypeStruct((bsz, GDN_CONV - 1, GDN_QKV), F32),
                   jax.ShapeDtypeStruct((bsz, GDN_HEADS, GDN_DK, GDN_DV), F32)],
        scratch_shapes=[pltpu.VMEM((tb + 8, GDN_QKV), F32),
                        pltpu.VMEM((GDN_HEADS, GDN_DK, GDN_DV), F32)],
        compiler_params=_cparams(("parallel", "arbitrary")),
        name="gdn_prompt",
    )(a_qkv, a_z, small, small_t, conv_w, a_log.reshape(1, GDN_HEADS), dt_bias.reshape(1, GDN_HEADS),
      a_log.reshape(GDN_HEADS, 1), dt_bias.reshape(GDN_HEADS, 1), norm_g.reshape(1, GDN_DV))


def _mlstm_prompt_kernel(cqkv_ref, co_ref, small_ref, small_t_ref, bi_ref, bf_ref, bi_c_ref, bf_c_ref, ng_ref,
                         o_ref, c_out_ref, n_out_ref, m_out_ref, cn_scr, m_scr, *, tb):
    t = pl.program_id(1)
    c = ML_CHUNK
    nc = tb // c
    heads = range(ML_HEADS)
    pairs = [(ci, h) for ci in range(nc) for h in heads]

    @pl.when(t == 0)
    def _():
        cn_scr[...] = jnp.zeros_like(cn_scr)
        m_scr[...] = jnp.zeros_like(m_scr)

    sm = small_ref[...]
    ip_c = sm[:, SM_I:SM_I + ML_HEADS] + bi_ref[...]
    lf_c = -_softplus(-(sm[:, SM_F:SM_F + ML_HEADS] + bf_ref[...]))
    ip_r = small_t_ref[SM_I:SM_I + ML_HEADS, :] + bi_c_ref[...]
    lf_r = -_softplus(-(small_t_ref[SM_F:SM_F + ML_HEADS, :] + bf_c_ref[...]))
    cmask = _chunk_cumsum_mask(tb, c)
    f_c = _dot01(cmask, lf_c)
    f_r = _dot01_r(lf_r, cmask.T)

    incl, _, _ = _tri_masks(c)
    ng = ng_ref[...]
    rows = lambda ci: slice(ci * c, (ci + 1) * c)
    ones = jnp.ones((c, ML_DV), BF16)
    rep = lambda col: jnp.broadcast_to(col, (c, ML_DV))

    qs = [cqkv_ref[rows(ci), h * ML_DK:(h + 1) * ML_DK] for ci, h in pairs]
    ks = [cqkv_ref[rows(ci), ML_QK + h * ML_DK:ML_QK + (h + 1) * ML_DK].astype(F32) * (ML_DK ** -0.5)
          for ci, h in pairs]
    v1s = [jnp.concatenate([cqkv_ref[rows(ci), 2 * ML_QK + h * ML_DV:2 * ML_QK + (h + 1) * ML_DV], ones], axis=1)
           for ci, h in pairs]
    fcols = [f_c[rows(ci), h:h + 1] for ci, h in pairs]
    ds = [jnp.where(incl, f_c[rows(ci), h:h + 1] - f_r[h:h + 1, rows(ci)] + ip_r[h:h + 1, rows(ci)], -jnp.inf)
          for ci, h in pairs]
    dmaxs = [jnp.max(d, axis=-1, keepdims=True) for d in ds]
    qks = [_dot_nt(q, k) for q, k in zip(qs, ks)]
    p_locs = [jnp.exp(d - dm) * qk for d, dm, qk in zip(ds, dmaxs, qks)]
    pvs = [_dot(p, v1) for p, v1 in zip(p_locs, v1s)]
    wks = [jnp.exp(f[c - 1:c, :] - f + ip_c[rows(ci), h:h + 1] - dm[c - 1:c, :]) * k
           for (ci, h), f, dm, k in zip(pairs, fcols, dmaxs, ks)]
    kvs = [_dot_tn(wk, v1) for wk, v1 in zip(wks, v1s)]
    fbs = [rep(f) for f in fcols]
    dmbs = [rep(dm) for dm in dmaxs]

    cn_st = [cn_scr[h] for h in heads]
    m_st = [m_scr[h:h + 1, :] for h in heads]
    for ci in range(nc):
        idx = [ci * ML_HEADS + h for h in heads]
        qcs = [_dot(qs[i], cn_st[h]) for h, i in zip(heads, idx)]
        for h, i in zip(heads, idx):
            b_log = fbs[i] + m_st[h]
            m_r = jnp.maximum(b_log, dmbs[i])
            inter = jnp.exp(b_log - m_r)
            sc = jnp.exp(dmbs[i] - m_r)
            num = inter * qcs[h][:, :ML_DV] + sc * pvs[i][:, :ML_DV]
            den = inter * qcs[h][:, ML_DV:] + sc * pvs[i][:, ML_DV:]
            h_t = num / jnp.maximum(jnp.abs(den), jnp.exp(-m_r))
            m_new = m_r[c - 1:c, :]
            decay = jnp.exp(b_log[c - 1:c, :] - m_new)
            ksc = jnp.exp(dmbs[i][c - 1:c, :] - m_new)
            cn_st[h] = (jnp.concatenate([decay, decay], axis=1) * cn_st[h]
                        + jnp.concatenate([ksc, ksc], axis=1) * kvs[i])
            m_st[h] = m_new
            og = _sigmoid(co_ref[rows(ci), h * ML_DV:(h + 1) * ML_DV].astype(F32))
            o_ref[rows(ci), h * ML_DV:(h + 1) * ML_DV] = _rms(og * h_t, ng).astype(o_ref.dtype)
    for h in heads:
        cn_scr[h] = cn_st[h]
        m_scr[h:h + 1, :] = m_st[h]

    @pl.when(t == pl.num_programs(1) - 1)
    def _():
        c_out_ref[0] = cn_scr[:, :, 0:ML_DV]
        n_out_ref[0] = cn_scr[:, :, ML_DV:]
        m_out_ref[0] = m_scr[...]


def _mlstm_prompt(c_qkv, c_o, small, small_t, b_i, b_f, norm_g, bsz, seq, tb):
    nt = seq // tb
    row = lambda b, t: (b * nt + t, 0)
    col = lambda b, t: (0, b * nt + t)
    return pl.pallas_call(
        functools.partial(_mlstm_prompt_kernel, tb=tb),
        grid=(bsz, nt),
        in_specs=[pl.BlockSpec((tb, ML_QKV), row),
                  pl.BlockSpec((tb, ML_V), row),
                  pl.BlockSpec((tb, LANE), row),
                  pl.BlockSpec((LANE, tb), col),
                  _const_spec((1, ML_HEADS)),
                  _const_spec((1, ML_HEADS)),
                  _const_spec((ML_HEADS, 1)),
                  _const_spec((ML_HEADS, 1)),
                  _const_spec((1, ML_DV))],
        out_specs=[pl.BlockSpec((tb, ML_V), row),
                   pl.BlockSpec((1, ML_HEADS, ML_DK, ML_DV), lambda b, t: (b, 0, 0, 0)),
                   pl.BlockSpec((1, ML_HEADS, ML_DK, ML_DV), lambda b, t: (b, 0, 0, 0)),
                   pl.BlockSpec((1, ML_HEADS, LANE), lambda b, t: (b, 0, 0))],
        out_shape=[jax.ShapeDtypeStruct((bsz * seq, ML_V), BF16),
                   jax.ShapeDtypeStruct((bsz, ML_HEADS, ML_DK, ML_DV), F32),
                   jax.ShapeDtypeStruct((bsz, ML_HEADS, ML_DK, ML_DV), F32),
                   jax.ShapeDtypeStruct((bsz, ML_HEADS, LANE), F32)],
        scratch_shapes=[pltpu.VMEM((ML_HEADS, ML_DK, 2 * ML_DV), F32),
                        pltpu.VMEM((ML_HEADS, LANE), F32)],
        compiler_params=_cparams(("parallel", "arbitrary")),
        name="mlstm_prompt",
    )(c_qkv, c_o, small, small_t, b_i.reshape(1, ML_HEADS), b_f.reshape(1, ML_HEADS),
      b_i.reshape(ML_HEADS, 1), b_f.reshape(ML_HEADS, 1), norm_g.reshape(1, ML_DV))


def _rope_tables(pos):
    half = MLA_ROPE // 2
    inv_freq = ROPE_BASE ** (-jnp.arange(half, dtype=F32) / half)
    ang = pos.astype(F32)[:, None] * inv_freq[None, :]
    cos, sin = jnp.cos(ang), jnp.sin(ang)
    n = pos.shape[0]
    ones = jnp.ones((n, MLA_NOPE), F32)
    zeros = jnp.zeros((n, MLA_NOPE), F32)
    pad = jnp.zeros((n, LANE - MLA_QHEAD), F32)
    ctab = jnp.concatenate([ones, cos, cos, pad], axis=1)
    stab = jnp.concatenate([zeros, -sin, sin, pad], axis=1)
    return ctab, stab


def _mla_pre_kernel(cq_ref, ckv_ref, small_ref, ctab_ref, stab_ref, qg_ref, kvg_ref,
                    wuq_ref, wuk_ref, wuv_ref,
                    q_ref, k_ref, v_ref, ckv_out_ref, kr_out_ref):
    ctab = ctab_ref[...]
    stab = stab_ref[...]
    lane = lax.broadcasted_iota(jnp.int32, ctab.shape, 1)
    in_rope = (lane >= MLA_NOPE) & (lane < MLA_QHEAD)
    sm = small_ref[...]
    kr = jnp.where(in_rope, sm * ctab + _rope_swap(sm, MLA_NOPE) * stab, 0.0)
    kr_out_ref[0] = kr.T[MLA_NOPE:MLA_QHEAD, :]
    c_kv = _rms(ckv_ref[...], kvg_ref[...])
    ckv_out_ref[...] = c_kv
    c_kv_b = c_kv.astype(BF16)
    k_nope = jnp.dot(c_kv_b, wuk_ref[...], preferred_element_type=F32)
    v = jnp.dot(c_kv_b, wuv_ref[...], preferred_element_type=F32)
    v_t = v.T
    ones = jnp.ones((MLA_VDIM, v_t.shape[1]), v_ref.dtype)
    for h in range(MLA_HEADS):
        v_ref[0, h * HEAD_PAD:h * HEAD_PAD + MLA_VDIM, :] = v_t[h * MLA_VDIM:(h + 1) * MLA_VDIM, :].astype(v_ref.dtype)
        v_ref[0, h * HEAD_PAD + MLA_VDIM:(h + 1) * HEAD_PAD, :] = ones
    qn = _rms(cq_ref[...].astype(F32), qg_ref[...]).astype(BF16)
    q = jnp.dot(qn, wuq_ref[...], preferred_element_type=F32)
    q_sw = _rope_swap(q, MLA_NOPE)
    for h in range(MLA_HEADS):
        sl = slice(h * HEAD_PAD, (h + 1) * HEAD_PAD)
        q_ref[:, sl] = ((q[:, sl] * ctab + q_sw[:, sl] * stab) * (MLA_SCALE * LOG2E)).astype(q_ref.dtype)
        k_ref[:, sl] = (k_nope[:, sl] + kr).astype(k_ref.dtype)


def _pad_heads(w, head_dim):
    rows = w.shape[0]
    w = w.reshape(rows, MLA_HEADS, head_dim)
    return jnp.pad(w, ((0, 0), (0, 0), (0, HEAD_PAD - head_dim))).reshape(rows, MLA_HEADS * HEAD_PAD)


def _mla_pre(b_cq, b_ckv, small, ctab, stab, q_norm_g, kv_norm_g, w_uq_pad, w_uk_pad, w_uv, seq, tm):
    m = b_cq.shape[0]
    nt = seq // tm
    row = lambda i: (i, 0)
    tab = lambda i: (i % nt, 0)
    hp = MLA_HEADS * HEAD_PAD
    return pl.pallas_call(
        _mla_pre_kernel,
        grid=(m // tm,),
        in_specs=[pl.BlockSpec((tm, MLA_Q_LORA), row),
                  pl.BlockSpec((tm, MLA_KV_LORA), row),
                  pl.BlockSpec((tm, LANE), row),
                  pl.BlockSpec((tm, LANE), tab),
                  pl.BlockSpec((tm, LANE), tab),
                  _const_spec((1, MLA_Q_LORA)),
                  _const_spec((1, MLA_KV_LORA)),
                  _const_spec((MLA_Q_LORA, hp)),
                  _const_spec((MLA_KV_LORA, hp)),
                  _const_spec((MLA_KV_LORA, MLA_V))],
        out_specs=[pl.BlockSpec((tm, hp), row),
                   pl.BlockSpec((tm, hp), row),
                   pl.BlockSpec((1, hp, tm), lambda i: (i // nt, 0, i % nt)),
                   pl.BlockSpec((tm, MLA_KV_LORA), row),
                   pl.BlockSpec((1, MLA_ROPE, tm), lambda i: (i // nt, 0, i % nt))],
        out_shape=[jax.ShapeDtypeStruct((m, hp), BF16),
                   jax.ShapeDtypeStruct((m, hp), BF16),
                   jax.ShapeDtypeStruct((m // seq, hp, seq), BF16),
                   jax.ShapeDtypeStruct((m, MLA_KV_LORA), F32),
                   jax.ShapeDtypeStruct((m // seq, MLA_ROPE, seq), F32)],
        compiler_params=_cparams(("parallel",)),
        name="mla_pre",
    )(b_cq, b_ckv, small, ctab, stab, q_norm_g.reshape(1, -1), kv_norm_g.reshape(1, -1),
      w_uq_pad, w_uk_pad, w_uv)


def _flash_kernel(q_ref, k_ref, vt_ref, o_ref, m_scr, acc_scr, *, bq, bk):
    i = pl.program_id(1)
    n_full = (i * bq) // bk
    hsl = lambda h: slice(h * HEAD_PAD, (h + 1) * HEAD_PAD)
    m_scr[...] = jnp.full_like(m_scr, -jnp.inf)
    acc_scr[...] = jnp.zeros_like(acc_scr)

    def step(j, masked, q0):
        k0 = pl.multiple_of(j * bk, bk)
        nq = bq - q0
        if masked:
            kpos = j * bk + lax.broadcasted_iota(jnp.int32, (bk, nq), 0)
            qpos = i * bq + q0 + lax.broadcasted_iota(jnp.int32, (bk, nq), 1)
            keep = kpos <= qpos
        sts = [lax.dot_general(k_ref[pl.ds(k0, bk), hsl(h)], q_ref[q0:bq, hsl(h)], (((1,), (1,)), ((), ())),
                               preferred_element_type=F32) for h in range(MLA_HEADS)]
        for h in range(MLA_HEADS):
            st = jnp.where(keep, sts[h], -jnp.inf) if masked else sts[h]
            m_prev = m_scr[h, :, q0:bq]
            m_next = jnp.maximum(m_prev, jnp.max(st, axis=0, keepdims=True))
            p_t = jnp.exp2(st - m_next[0:1, :])
            alpha = jnp.exp2(m_prev - m_next)[0:1, :]
            acc_scr[h, :, q0:bq] = alpha * acc_scr[h, :, q0:bq] + jnp.dot(
                vt_ref[0, hsl(h), pl.ds(k0, bk)], p_t.astype(BF16), preferred_element_type=F32)
            m_scr[h, :, q0:bq] = m_next

    def body_full(j, carry):
        step(j, False, 0)
        return carry

    lax.fori_loop(0, n_full, body_full, 0)
    for d in range(bq // bk):
        step(n_full + d, True, d * bk)
    outs = []
    for h in range(MLA_HEADS):
        a = acc_scr[h]
        outs.append(a[0:MLA_VDIM, :] / a[MLA_VDIM:MLA_VDIM + 1, :])
    o_ref[...] = jnp.concatenate(outs, axis=0).T.astype(o_ref.dtype)


def _flash(q, k, v_t, bsz, seq, bq, bk):
    nq = seq // bq
    hp = MLA_HEADS * HEAD_PAD
    return pl.pallas_call(
        functools.partial(_flash_kernel, bq=bq, bk=bk),
        grid=(bsz, nq),
        in_specs=[pl.BlockSpec((bq, hp), lambda b, i: (b * nq + i, 0)),
                  pl.BlockSpec((seq, hp), lambda b, i: (b, 0)),
                  pl.BlockSpec((1, hp, seq), lambda b, i: (b, 0, 0))],
        out_specs=pl.BlockSpec((bq, MLA_V), lambda b, i: (b * nq + i, 0)),
        out_shape=jax.ShapeDtypeStruct((bsz * seq, MLA_V), BF16),
        scratch_shapes=[pltpu.VMEM((MLA_HEADS, 8, bq), F32), pltpu.VMEM((MLA_HEADS, HEAD_PAD, bq), F32)],
        compiler_params=_cparams(("parallel", "arbitrary")),
        name="mla_flash",
    )(q, k, v_t)


def _merge_kernel(x_ref, oa_ref, ob_ref, oc_ref, gates_ref, wa_ref, wb_ref, wc_ref, wo_ref, y_ref):
    g = gates_ref
    merged = (_sigmoid(g[:, 0:D_MODEL].astype(F32)) * _dot(oa_ref[...], wa_ref[...])
              + _sigmoid(g[:, D_MODEL:2 * D_MODEL].astype(F32)) * _dot(ob_ref[...], wb_ref[...])
              + _sigmoid(g[:, 2 * D_MODEL:].astype(F32)) * _dot(oc_ref[...], wc_ref[...]))
    y_ref[...] = x_ref[...] + _dot(merged, wo_ref[...])


def _merge(x, o_a, o_b, o_c, gates, w_a, w_b, w_c, w_o, tm):
    m = x.shape[0]
    row = lambda i: (i, 0)
    return pl.pallas_call(
        _merge_kernel,
        grid=(m // tm,),
        in_specs=[pl.BlockSpec((tm, D_MODEL), row),
                  pl.BlockSpec((tm, GDN_V), row),
                  pl.BlockSpec((tm, MLA_V), row),
                  pl.BlockSpec((tm, ML_V), row),
                  pl.BlockSpec((tm, 3 * D_MODEL), row),
                  _const_spec((GDN_V, D_MODEL)),
                  _const_spec((MLA_V, D_MODEL)),
                  _const_spec((ML_V, D_MODEL)),
                  _const_spec((D_MODEL, D_MODEL))],
        out_specs=pl.BlockSpec((tm, D_MODEL), row),
        out_shape=jax.ShapeDtypeStruct((m, D_MODEL), F32),
        compiler_params=_cparams(("parallel",)),
        name="merge",
    )(x, o_a, o_b, o_c, gates, w_a, w_b, w_c, w_o)


def _ffn_prompt_kernel(x_ref, g_ref, wup_ref, cw_ref, cb_ref, wdn_ref, gf_ref,
                       y_ref, conv_out_ref, gp_scr, *, tm, final):
    t = pl.program_id(1)

    @pl.when(t == 0)
    def _():
        gp_scr[0:8, :] = jnp.zeros((8, D_FF), F32)

    x = x_ref[...]
    hn = _rms(x, g_ref[...]).astype(BF16)
    u = jnp.dot(hn, wup_ref[:, 0:D_FF], preferred_element_type=F32)
    gt = jnp.dot(hn, wup_ref[:, D_FF:], preferred_element_type=F32)
    gp_scr[8:8 + tm, :] = gt
    cw = cw_ref[...]
    conv = gt * cw[2:3] + gp_scr[7:7 + tm, :] * cw[1:2] + gp_scr[6:6 + tm, :] * cw[0:1]
    gp_scr[0:8, :] = gt[tm - 8:tm, :]
    act = _silu(conv + cb_ref[...]) * u
    y = x + _dot(act, wdn_ref[...])
    if final:
        y = _rms(y, gf_ref[...])
    y_ref[...] = y

    @pl.when(t == pl.num_programs(1) - 1)
    def _():
        conv_out_ref[0] = gt[tm - (FFN_CONV - 1):tm, :]


def _ffn_prompt(x, norm_g, w_up, conv_w, conv_b, w_down, final_g, bsz, seq, tm, final):
    nt = seq // tm
    row = lambda b, t: (b * nt + t, 0)
    return pl.pallas_call(
        functools.partial(_ffn_prompt_kernel, tm=tm, final=final),
        grid=(bsz, nt),
        in_specs=[pl.BlockSpec((tm, D_MODEL), row),
                  _const_spec((1, D_MODEL)),
                  _const_spec((D_MODEL, 2 * D_FF)),
                  _const_spec((FFN_CONV, D_FF)),
                  _const_spec((1, D_FF)),
                  _const_spec((D_FF, D_MODEL)),
                  _const_spec((1, D_MODEL))],
        out_specs=[pl.BlockSpec((tm, D_MODEL), row),
                   pl.BlockSpec((1, FFN_CONV - 1, D_FF), lambda b, t: (b, 0, 0))],
        out_shape=[jax.ShapeDtypeStruct((bsz * seq, D_MODEL), F32),
                   jax.ShapeDtypeStruct((bsz, FFN_CONV - 1, D_FF), F32)],
        scratch_shapes=[pltpu.VMEM((tm + 8, D_FF), F32)],
        compiler_params=_cparams(("parallel", "arbitrary")),
        name="ffn_prompt",
    )(x, norm_g.reshape(1, -1), w_up, conv_w, conv_b.reshape(1, -1), w_down, final_g.reshape(1, -1))


def _ffn_sample_kernel(x_ref, hist_ref, g_ref, wup_ref, cw_ref, cb_ref, wdn_ref, gf_ref,
                       y_ref, hist_out_ref, *, final):
    x = x_ref[...]
    hn = _rms(x, g_ref[...]).astype(BF16)
    u = jnp.dot(hn, wup_ref[:, 0:D_FF], preferred_element_type=F32)
    gt = jnp.dot(hn, wup_ref[:, D_FF:], preferred_element_type=F32)
    cw = cw_ref[...]
    conv = hist_ref[0] * cw[0:1] + hist_ref[1] * cw[1:2] + gt * cw[2:3]
    act = _silu(conv + cb_ref[...]) * u
    y = x + _dot(act, wdn_ref[...])
    if final:
        y = _rms(y, gf_ref[...])
    y_ref[...] = y
    hist_out_ref[0] = hist_ref[1]
    hist_out_ref[1] = gt


def _ffn_sample(x, hist_t, norm_g, w_up, conv_w, conv_b, w_down, final_g, final):
    bs = x.shape[0]
    return pl.pallas_call(
        functools.partial(_ffn_sample_kernel, final=final),
        out_shape=[jax.ShapeDtypeStruct((bs, D_MODEL), F32),
                   jax.ShapeDtypeStruct((FFN_CONV - 1, bs, D_FF), F32)],
        compiler_params=pltpu.CompilerParams(vmem_limit_bytes=VMEM_LIMIT),
        name="ffn_sample",
    )(x, hist_t, norm_g.reshape(1, -1), w_up, conv_w, conv_b.reshape(1, -1), w_down, final_g.reshape(1, -1))


def _row_to_col(row, eye):
    n = eye.shape[0]
    return jnp.sum(jnp.where(eye, jnp.broadcast_to(row, (n, n)), 0.0), axis=1, keepdims=True)


def _rec_sample_kernel(aqkv_ref, hist_ref, az_ref, small_ref, convw_ref, alog_ref, dtb_ref, gng_ref,
                       s0_ref, cqkv_ref, co_ref, bi_ref, bf_ref, mng_ref, c0_ref, n0_ref, m0_ref,
                       oa_ref, hist_out_ref, s_out_ref, oc_ref, c_out_ref, n_out_ref, m_out_ref):
    _, _, eye128 = _tri_masks(LANE)
    lane = lax.broadcasted_iota(jnp.int32, (1, LANE), 1)
    sm = small_ref[0]
    x = aqkv_ref[0]
    hist = hist_ref[0]
    w = convw_ref[...]
    y = _silu(hist[0:1] * w[0:1] + hist[1:2] * w[1:2] + hist[2:3] * w[2:3] + x * w[3:4])
    hist_out_ref[0, 0:1, :] = hist[1:2]
    hist_out_ref[0, 1:2, :] = hist[2:3]
    hist_out_ref[0, 2:3, :] = x
    beta_row = _sigmoid(sm)
    decay_row = jnp.exp(-jnp.exp(alog_ref[...]) * _softplus(sm + dtb_ref[...]))
    gng = gng_ref[...]
    az = az_ref[0]
    for h in range(GDN_HEADS):
        q_row = _l2(y[:, h * GDN_DK:(h + 1) * GDN_DK]) * (GDN_DK ** -0.5)
        k_row = _l2(y[:, GDN_QK + h * GDN_DK:GDN_QK + (h + 1) * GDN_DK])
        v_row = y[:, 2 * GDN_QK + h * GDN_DV:2 * GDN_QK + (h + 1) * GDN_DV]
        beta = beta_row[:, SM_BETA + h:SM_BETA + h + 1]
        a = decay_row[:, SM_ALPHA + h:SM_ALPHA + h + 1]
        k_col = _row_to_col(k_row, eye128)
        q_col = _row_to_col(q_row, eye128)
        s_old = s0_ref[0, h]
        ks = jnp.sum(k_col * s_old, axis=0, keepdims=True)
        v_new = beta * (v_row - a * ks)
        qs = jnp.sum(q_col * s_old, axis=0, keepdims=True)
        qk = jnp.sum(q_row * k_row, axis=-1, keepdims=True)
        o = a * qs + qk * v_new
        s_out_ref[0, h] = a * s_old + k_col * v_new
        z = az[:, h * GDN_DV:(h + 1) * GDN_DV]
        oa_ref[0, :, h * GDN_DV:(h + 1) * GDN_DV] = _rms(o, gng) * _silu(z)

    cq = cqkv_ref[0]
    co = co_ref[0]
    ip_row = sm + bi_ref[...]
    lf_row = -_softplus(-(sm + bf_ref[...]))
    mng = mng_ref[...]
    m_prev = m0_ref[0]
    m_out = jnp.zeros((1, LANE), F32)
    for h in range(ML_HEADS):
        q_row = cq[:, h * LANE:(h + 1) * LANE]
        k_row = cq[:, (ML_HEADS + h) * LANE:(ML_HEADS + h + 1) * LANE] * (ML_DK ** -0.5)
        v_row = cq[:, 2 * ML_HEADS * LANE + h * ML_DV:2 * ML_HEADS * LANE + (h + 1) * ML_DV]
        ip = ip_row[:, SM_I + h:SM_I + h + 1]
        lf = lf_row[:, SM_F + h:SM_F + h + 1]
        m_old = m_prev[:, h:h + 1]
        c_old = c0_ref[0, h]
        n_old = n0_ref[0, h:h + 1, :]
        k_col = _row_to_col(k_row, eye128)[0:ML_DK]
        q_col = _row_to_col(q_row, eye128)[0:ML_DK]
        b_log = lf + m_old
        m_r = jnp.maximum(b_log, ip)
        inter = jnp.exp(b_log - m_r)
        w_key = jnp.exp(ip - m_r)
        qk = jnp.sum(q_row * k_row, axis=-1, keepdims=True)
        p = w_key * qk
        num = inter * jnp.sum(q_col * c_old, axis=0, keepdims=True) + p * v_row
        den = inter * jnp.sum(q_row * n_old, axis=-1, keepdims=True) + p
        h_t = num / jnp.maximum(jnp.abs(den), jnp.exp(-m_r))
        c_out_ref[0, h] = inter * c_old + (w_key * k_col) * v_row
        n_out_ref[0, h:h + 1, :] = inter * n_old + w_key * k_row
        m_out = jnp.where(lane == h, m_r, m_out)
        og = _sigmoid(co[:, h * ML_DV:(h + 1) * ML_DV])
        oc_ref[0, :, h * ML_DV:(h + 1) * ML_DV] = _rms(og * h_t, mng)
    m_out_ref[0] = m_out


def _rec_sample(a_qkv, hist, a_z, small, conv_w, alog_row, dtb_row, gdn_norm_g, s0,
                c_qkv, c_o, bi_row, bf_row, ml_norm_g, c0, n0_pad, m0_pad):
    bs = a_qkv.shape[0]
    r3 = lambda a: a.reshape(bs, 1, a.shape[-1])
    b3 = lambda n: pl.BlockSpec((1, 1, n), lambda b: (b, 0, 0))
    mlw = 2 * ML_HEADS * LANE + ML_V
    return pl.pallas_call(
        _rec_sample_kernel,
        grid=(bs,),
        in_specs=[b3(GDN_QKV),
                  pl.BlockSpec((1, GDN_CONV - 1, GDN_QKV), lambda b: (b, 0, 0)),
                  b3(GDN_V), b3(LANE),
                  _const_spec((GDN_CONV, GDN_QKV)),
                  _const_spec((1, LANE)), _const_spec((1, LANE)), _const_spec((1, GDN_DV)),
                  pl.BlockSpec((1, GDN_HEADS, GDN_DK, GDN_DV), lambda b: (b, 0, 0, 0)),
                  b3(mlw), b3(ML_V),
                  _const_spec((1, LANE)), _const_spec((1, LANE)), _const_spec((1, ML_DV)),
                  pl.BlockSpec((1, ML_HEADS, ML_DK, ML_DV), lambda b: (b, 0, 0, 0)),
                  pl.BlockSpec((1, ML_HEADS, LANE), lambda b: (b, 0, 0)),
                  b3(LANE)],
        out_specs=[b3(GDN_V),
                   pl.BlockSpec((1, GDN_CONV - 1, GDN_QKV), lambda b: (b, 0, 0)),
                   pl.BlockSpec((1, GDN_HEADS, GDN_DK, GDN_DV), lambda b: (b, 0, 0, 0)),
                   b3(ML_V),
                   pl.BlockSpec((1, ML_HEADS, ML_DK, ML_DV), lambda b: (b, 0, 0, 0)),
                   pl.BlockSpec((1, ML_HEADS, LANE), lambda b: (b, 0, 0)),
                   b3(LANE)],
        out_shape=[jax.ShapeDtypeStruct((bs, 1, GDN_V), F32),
                   jax.ShapeDtypeStruct((bs, GDN_CONV - 1, GDN_QKV), F32),
                   jax.ShapeDtypeStruct((bs, GDN_HEADS, GDN_DK, GDN_DV), F32),
                   jax.ShapeDtypeStruct((bs, 1, ML_V), F32),
                   jax.ShapeDtypeStruct((bs, ML_HEADS, ML_DK, ML_DV), F32),
                   jax.ShapeDtypeStruct((bs, ML_HEADS, LANE), F32),
                   jax.ShapeDtypeStruct((bs, 1, LANE), F32)],
        compiler_params=_cparams(("parallel",)),
        name="rec_sample",
    )(r3(a_qkv), hist, r3(a_z), r3(small), conv_w, alog_row, dtb_row, gdn_norm_g.reshape(1, -1), s0,
      r3(c_qkv), r3(c_o), bi_row, bf_row, ml_norm_g.reshape(1, -1), c0, n0_pad, r3(m0_pad))


def _mla_sample_pre_kernel(cq_ref, ckv_ref, small_ref, ctab_ref, stab_ref, qg_ref, kvg_ref,
                           wuq_ref, wukt_ref, qlat_ref, qrope_ref, ckv_out_ref, kr_out_ref):
    ctab = ctab_ref[...]
    stab = stab_ref[...]
    lane = lax.broadcasted_iota(jnp.int32, (1, LANE), 1)
    in_rope = (lane >= MLA_NOPE) & (lane < MLA_QHEAD)
    sm = small_ref[...]
    kr_out_ref[...] = jnp.where(in_rope, sm * ctab + _rope_swap(sm, MLA_NOPE) * stab, 0.0)
    ckv_out_ref[...] = _rms(ckv_ref[...], kvg_ref[...])
    qn = _rms(cq_ref[...], qg_ref[...])
    q = _dot(qn, wuq_ref[...])
    q_sw = _rope_swap(q, MLA_NOPE)
    for h in range(MLA_HEADS):
        sl = slice(h * HEAD_PAD, (h + 1) * HEAD_PAD)
        qh = q[:, sl]
        qrope_ref[:, sl] = jnp.where(in_rope, qh * ctab + q_sw[:, sl] * stab, 0.0)
        qlat_ref[:, sl] = _dot(jnp.where(lane < MLA_NOPE, qh, 0.0), wukt_ref[h])


def _mla_sample_pre(b_cq, b_ckv, small, ctab, stab, q_norm_g, kv_norm_g, w_uq_pad, w_uk_t):
    bs = b_cq.shape[0]
    hp = MLA_HEADS * HEAD_PAD
    return pl.pallas_call(
        _mla_sample_pre_kernel,
        out_shape=[jax.ShapeDtypeStruct((bs, hp), F32),
                   jax.ShapeDtypeStruct((bs, hp), F32),
                   jax.ShapeDtypeStruct((bs, MLA_KV_LORA), F32),
                   jax.ShapeDtypeStruct((bs, LANE), F32)],
        compiler_params=pltpu.CompilerParams(vmem_limit_bytes=VMEM_LIMIT),
        name="mla_sample_pre",
    )(b_cq, b_ckv, small, ctab, stab, q_norm_g.reshape(1, -1), kv_norm_g.reshape(1, -1), w_uq_pad, w_uk_t)


def _mla_sample_kernel(pt_ref, qlat_ref, qrope_ref, ckv_ref, kr_ref, lat_hbm, krt_hbm, ctx_ref,
                       lat_buf, krt_buf, sem, *, layer, n_pages):
    b = pl.program_id(0)
    slot = b % 2

    def page_copies(row, buf_slot, g):
        page = pt_ref[row, g]
        off = pl.multiple_of(g * PAGE_SIZE, PAGE_SIZE)
        return (pltpu.make_async_copy(lat_hbm.at[layer, page], lat_buf.at[buf_slot, pl.ds(off, PAGE_SIZE), :],
                                      sem.at[0, buf_slot]),
                pltpu.make_async_copy(krt_hbm.at[layer, page], krt_buf.at[buf_slot, :, pl.ds(off, PAGE_SIZE)],
                                      sem.at[1, buf_slot]))

    def start_row(row, buf_slot):
        def body(g, carry):
            for cp in page_copies(row, buf_slot, g):
                cp.start()
            return carry
        lax.fori_loop(0, n_pages, body, 0, unroll=8)

    def wait_row(buf_slot):
        pltpu.make_async_copy(lat_buf.at[buf_slot], lat_buf.at[buf_slot], sem.at[0, buf_slot]).wait()
        pltpu.make_async_copy(krt_buf.at[buf_slot], krt_buf.at[buf_slot], sem.at[1, buf_slot]).wait()

    @pl.when(b == 0)
    def _():
        start_row(0, 0)

    @pl.when(b + 1 < pl.num_programs(0))
    def _():
        start_row(b + 1, 1 - slot)

    wait_row(slot)

    q_lat = qlat_ref[0].astype(BF16)
    q_rope = qrope_ref[0][:, MLA_NOPE:MLA_QHEAD].astype(BF16)
    lat = lat_buf[slot].astype(BF16)
    s = (_dot_nt(q_lat, lat) + _dot(q_rope, krt_buf[slot])) * MLA_SCALE
    c_new = ckv_ref[0].astype(BF16).astype(F32)
    kr_new = kr_ref[0][:, MLA_NOPE:MLA_QHEAD].astype(BF16).astype(F32)
    s_new = (jnp.sum(q_lat.astype(F32) * c_new, axis=-1, keepdims=True)
             + jnp.sum(q_rope.astype(F32) * kr_new, axis=-1, keepdims=True)) * MLA_SCALE
    m = jnp.maximum(jnp.max(s, axis=-1, keepdims=True), s_new)
    p = jnp.exp(s - m)
    p_new = jnp.exp(s_new - m)
    denom = jnp.sum(p, axis=-1, keepdims=True) + p_new
    ctx_ref[0] = (_dot(p, lat) + p_new.astype(BF16).astype(F32) * c_new) / denom


def _mla_sample(page_table, q_lat, q_rope, c_kv, kr, cache_lat, cache_krt, layer):
    bs, n_pages = page_table.shape
    past = n_pages * PAGE_SIZE
    hp3 = lambda a: a.reshape(bs, MLA_HEADS, HEAD_PAD)
    per_b = lambda n: pl.BlockSpec((1, n, LANE), lambda b, pt: (b, 0, 0))
    grid_spec = pltpu.PrefetchScalarGridSpec(
        num_scalar_prefetch=1,
        grid=(bs,),
        in_specs=[per_b(MLA_HEADS), per_b(MLA_HEADS), per_b(1), per_b(1),
                  pl.BlockSpec(memory_space=pl.ANY), pl.BlockSpec(memory_space=pl.ANY)],
        out_specs=per_b(MLA_HEADS),
        scratch_shapes=[pltpu.VMEM((2, past, MLA_KV_LORA), F32),
                        pltpu.VMEM((2, MLA_ROPE, past), F32),
                        pltpu.SemaphoreType.DMA((2, 2))],
    )
    return pl.pallas_call(
        functools.partial(_mla_sample_kernel, layer=layer, n_pages=n_pages),
        grid_spec=grid_spec,
        out_shape=jax.ShapeDtypeStruct((bs, MLA_HEADS, LANE), F32),
        compiler_params=_cparams(("arbitrary",)),
        name="mla_sample",
    )(page_table, hp3(q_lat), hp3(q_rope), c_kv.reshape(bs, 1, LANE), kr.reshape(bs, 1, LANE),
      cache_lat, cache_krt)


def _ctx_up_kernel(ctx_ref, wuv_ref, o_ref):
    o_ref[...] = _dot(ctx_ref[...], wuv_ref[...])


def _ctx_up(ctx, w_uv_bd):
    return pl.pallas_call(
        _ctx_up_kernel,
        out_shape=jax.ShapeDtypeStruct((ctx.shape[0], MLA_V), F32),
        name="ctx_up",
    )(ctx, w_uv_bd)


def _lane_row(vals, lane0):
    return jnp.zeros((1, LANE), F32).at[0, lane0:lane0 + vals.shape[0]].set(vals.astype(F32))


def _pick(n, pref):
    t = min(n, pref)
    while n % t:
        t //= 2
    return t


def kernel(x_prompt, x_sample, cache_kv_latent, cache_k_rope, state_gdn_conv, state_gdn_S,
           state_mlstm_C, state_mlstm_n, state_mlstm_m, state_ffn_conv, page_table,
           norm1_g, w_in, gdn_conv_w, gdn_A_log, gdn_dt_bias, gdn_norm_g,
           mla_q_norm_g, mla_w_uq, mla_kv_norm_g, mla_w_uk, mla_w_uv,
           ml_b_i, ml_b_f, ml_norm_g, w_branch_a, w_branch_b, w_branch_c, w_out,
           norm2_g, ffn_w_up, ffn_conv_w, ffn_conv_b, ffn_w_down, final_norm_g):
    bp, seq = x_prompt.shape[:2]
    bs, dec = x_sample.shape[:2]
    assert dec == 1 and seq % max(REC_CHUNK, ML_CHUNK) == 0
    depth = w_in.shape[0]
    n_pages = page_table.shape[1]
    past_len = n_pages * PAGE_SIZE
    mp = bp * seq

    tm = _pick(seq, 512)
    tb = _pick(seq, 512)
    bq = _pick(seq, 512)
    bk = _pick(seq, 256)

    ctab_p, stab_p = _rope_tables(jnp.arange(seq))
    ctab_s, stab_s = _rope_tables(jnp.full((1,), past_len))
    cache_krt = jnp.swapaxes(cache_k_rope, 2, 3)

    xp = x_prompt.reshape(mp, D_MODEL)
    xs = x_sample.reshape(bs, D_MODEL)
    out_p = [[] for _ in range(8)]
    out_s = [[] for _ in range(8)]

    for l in range(depth):
        last = l == depth - 1
        w_p, w_small_t = _pack_w_in(w_in[l], False)
        w_s, _ = _pack_w_in(w_in[l], True)
        w_uq_pad = _pad_heads(mla_w_uq[l], MLA_QHEAD).astype(BF16)
        w_uk_pad = _pad_heads(mla_w_uk[l], MLA_NOPE).astype(BF16)
        w_uv = mla_w_uv[l].astype(BF16)
        w_a, w_b, w_c, w_o = (w_branch_a[l].astype(BF16), w_branch_b[l].astype(BF16),
                              w_branch_c[l].astype(BF16), w_out[l].astype(BF16))
        w_up, w_dn = ffn_w_up[l].astype(BF16), ffn_w_down[l].astype(BF16)

        a_qkv, a_z, b_cq, b_ckv, c_qkv, c_o, gates, small, small_t = _in_proj(
            xp, norm1_g[l], w_p, PROMPT_GROUPS, tm, w_small_t)
        o_a, p_gconv, p_gs = _gdn_prompt(a_qkv, a_z, small, small_t, gdn_conv_w[l], gdn_A_log[l], gdn_dt_bias[l],
                                         gdn_norm_g[l], bp, seq, tb)
        q_pad, k_pad, v_all, p_ckv, p_kr = _mla_pre(b_cq, b_ckv, small, ctab_p, stab_p, mla_q_norm_g[l],
                                                    mla_kv_norm_g[l], w_uq_pad, w_uk_pad, w_uv, seq, tm)
        o_b = _flash(q_pad, k_pad, v_all, bp, seq, bq, bk)
        o_c, p_c, p_n, p_m = _mlstm_prompt(c_qkv, c_o, small, small_t, ml_b_i[l], ml_b_f[l], ml_norm_g[l], bp, seq, tb)
        xp = _merge(xp, o_a, o_b, o_c, gates, w_a, w_b, w_c, w_o, tm)
        xp, p_fconv = _ffn_prompt(xp, norm2_g[l], w_up, ffn_conv_w[l], ffn_conv_b[l], w_dn, final_norm_g,
                                  bp, seq, tm, last)
        for lst, a in zip(out_p, (p_ckv.reshape(bp, seq, MLA_KV_LORA), jnp.swapaxes(p_kr, 1, 2),
                                  p_gconv, p_gs, p_c, p_n[..., 0], p_m[..., 0], p_fconv)):
            lst.append(a)

        a_qkv, a_z, b_cq, b_ckv, c_qkv, c_o, gates, small = _in_proj(xs, norm1_g[l], w_s, SAMPLE_GROUPS, bs)
        n0_pad = jnp.pad(state_mlstm_n[l], ((0, 0), (0, 0), (0, LANE - ML_DK)))
        m0_pad = jnp.pad(state_mlstm_m[l], ((0, 0), (0, LANE - ML_HEADS)))
        o_a, s_gconv, s_gs, o_c, s_c, s_n, s_m = _rec_sample(
            a_qkv, state_gdn_conv[l], a_z, small, gdn_conv_w[l],
            _lane_row(gdn_A_log[l], SM_ALPHA), _lane_row(gdn_dt_bias[l], SM_ALPHA), gdn_norm_g[l],
            state_gdn_S[l], c_qkv, c_o, _lane_row(ml_b_i[l], SM_I), _lane_row(ml_b_f[l], SM_F),
            ml_norm_g[l], state_mlstm_C[l], n0_pad, m0_pad)
        w_uk_t = jnp.pad(jnp.transpose(mla_w_uk[l].reshape(MLA_KV_LORA, MLA_HEADS, MLA_NOPE), (1, 2, 0)),
                         ((0, 0), (0, HEAD_PAD - MLA_NOPE), (0, 0))).astype(BF16)
        q_lat, q_rope, s_ckv, kr_s = _mla_sample_pre(b_cq, b_ckv, small, ctab_s, stab_s, mla_q_norm_g[l],
                                                      mla_kv_norm_g[l], w_uq_pad, w_uk_t)
        ctx = _mla_sample(page_table, q_lat, q_rope, s_ckv, kr_s, cache_kv_latent, cache_krt, l)
        w_uv_bd = (jnp.eye(MLA_HEADS, dtype=F32)[:, None, :, None]
                   * jnp.transpose(mla_w_uv[l].reshape(MLA_KV_LORA, MLA_HEADS, MLA_VDIM), (1, 0, 2))[:, :, None, :]
                   ).reshape(MLA_HEADS * MLA_KV_LORA, MLA_V).astype(BF16)
        o_b = _ctx_up(ctx.reshape(bs, MLA_HEADS * LANE), w_uv_bd)
        xs = _merge(xs, o_a.reshape(bs, GDN_V), o_b, o_c.reshape(bs, ML_V), gates, w_a, w_b, w_c, w_o, bs)
        xs, s_fconv_t = _ffn_sample(xs, jnp.swapaxes(state_ffn_conv[l], 0, 1), norm2_g[l], w_up, ffn_conv_w[l],
                                    ffn_conv_b[l], w_dn, final_norm_g, last)
        for lst, a in zip(out_s, (s_ckv.reshape(bs, 1, MLA_KV_LORA),
                                  kr_s[:, MLA_NOPE:MLA_QHEAD].reshape(bs, 1, MLA_ROPE),
                                  s_gconv, s_gs, s_c, s_n[:, :, :ML_DK], s_m.reshape(bs, LANE)[:, :ML_HEADS],
                                  jnp.swapaxes(s_fconv_t, 0, 1))):
            lst.append(a)

    y_prompt = xp.reshape(bp, seq, D_MODEL)
    y_sample = xs.reshape(bs, 1, D_MODEL)
    return (y_prompt, y_sample) + tuple(jnp.stack(a) for a in out_p) + tuple(jnp.stack(a) for a in out_s)
```

```python
import functools
import math

import jax
import jax.numpy as jnp
import numpy as np
from jax import lax
from jax.experimental import pallas as pl
from jax.experimental.pallas import tpu as pltpu

F32 = jnp.float32
BF16 = jnp.bfloat16

D_MODEL = 1024
PAGE_SIZE = 128
GDN_HEADS, GDN_DK, GDN_DV, GDN_CONV = 4, 128, 128, 4
MLA_HEADS, MLA_Q_LORA, MLA_KV_LORA, MLA_NOPE, MLA_ROPE, MLA_VDIM = 8, 256, 128, 64, 32, 64
ROPE_BASE = 10000.0
ML_HEADS, ML_DK, ML_DV = 4, 64, 128
CHUNK = 64
REC_CHUNK = 64
ML_CHUNK = 128
D_FF = 2816
FFN_CONV = 3
NORM_EPS = 1e-6

GDN_QK = GDN_HEADS * GDN_DK
GDN_V = GDN_HEADS * GDN_DV
GDN_QKV = 2 * GDN_QK + GDN_V
MLA_QHEAD = MLA_NOPE + MLA_ROPE
MLA_V = MLA_HEADS * MLA_VDIM
MLA_SCALE = MLA_QHEAD ** -0.5
ML_QK = ML_HEADS * ML_DK
ML_V = ML_HEADS * ML_DV
ML_QKV = 2 * ML_QK + ML_V
IN_SIZES = (GDN_QKV, GDN_HEADS, GDN_HEADS, GDN_V, MLA_Q_LORA, MLA_KV_LORA, MLA_ROPE,
            ML_QKV, ML_HEADS, ML_HEADS, ML_V, 3 * D_MODEL)

LANE = 128
HEAD_PAD = 128
SM_BETA, SM_ALPHA, SM_I, SM_F, SM_KR = 0, 8, 16, 24, 64
VMEM_LIMIT = 56 * 1024 * 1024
LOG2E = math.log2(math.e)


def _cparams(sem):
    return pltpu.CompilerParams(dimension_semantics=sem, vmem_limit_bytes=VMEM_LIMIT)


def _const_spec(shape):
    nd = len(shape)
    return pl.BlockSpec(shape, lambda *_: (0,) * nd, pipeline_mode=pl.Buffered(1))


def _dot(a, b):
    return jnp.dot(a.astype(BF16), b.astype(BF16), preferred_element_type=F32)


def _dot_nt(a, b):
    return lax.dot_general(a.astype(BF16), b.astype(BF16), (((1,), (1,)), ((), ())),
                           preferred_element_type=F32)


def _dot_tn(a, b):
    return lax.dot_general(a.astype(BF16), b.astype(BF16), (((0,), (0,)), ((), ())),
                           preferred_element_type=F32)


def _dot01(mask_bf16, x):
    x1 = x.astype(BF16)
    r1 = x - x1.astype(F32)
    x2 = r1.astype(BF16)
    x3 = (r1 - x2.astype(F32)).astype(BF16)
    d = functools.partial(jnp.dot, preferred_element_type=F32)
    return d(mask_bf16, x1) + d(mask_bf16, x2) + d(mask_bf16, x3)


def _dot01_r(x, mask_bf16):
    x1 = x.astype(BF16)
    r1 = x - x1.astype(F32)
    x2 = r1.astype(BF16)
    x3 = (r1 - x2.astype(F32)).astype(BF16)
    d = functools.partial(jnp.dot, preferred_element_type=F32)
    return d(x1, mask_bf16) + d(x2, mask_bf16) + d(x3, mask_bf16)


def _rms(x, g):
    return x * lax.rsqrt(jnp.mean(x * x, axis=-1, keepdims=True) + NORM_EPS) * g


def _l2(x):
    return x * lax.rsqrt(jnp.sum(x * x, axis=-1, keepdims=True) + NORM_EPS)


def _sigmoid(x):
    return 0.5 * jnp.tanh(0.5 * x) + 0.5


def _silu(x):
    return x * _sigmoid(x)


def _softplus(x):
    return jnp.maximum(x, 0.0) + jnp.log1p(jnp.exp(-jnp.abs(x)))


def _tri_masks(n):
    r = lax.broadcasted_iota(jnp.int32, (n, n), 0)
    c = lax.broadcasted_iota(jnp.int32, (n, n), 1)
    return r >= c, r > c, r == c


def _chunk_cumsum_mask(tb, c):
    r = lax.broadcasted_iota(jnp.int32, (tb, tb), 0)
    q = lax.broadcasted_iota(jnp.int32, (tb, tb), 1)
    return jnp.where((r // c == q // c) & (r >= q), 1.0, 0.0).astype(BF16)


def _neumann_inverse_many(a_list, eye_f):
    n = eye_f.shape[0]
    ps = [-a for a in a_list]
    ts = [eye_f + p for p in ps]
    qs = [_dot(p, p) for p in ps]
    covered = 2
    while 2 * covered < n:
        rs = [_dot(q, jnp.concatenate([t, q], axis=1)) for t, q in zip(ts, qs)]
        ts = [t + r[:, :n] for t, r in zip(ts, rs)]
        qs = [r[:, n:] for r in rs]
        covered *= 2
    return [t + _dot(q, t) for t, q in zip(ts, qs)]


def _rope_swap(x, lane0):
    w = x.shape[-1]
    half = MLA_ROPE // 2
    lane = lax.broadcasted_iota(jnp.int32, x.shape, x.ndim - 1) % LANE
    left = pltpu.roll(x, w - half, x.ndim - 1)
    right = pltpu.roll(x, half, x.ndim - 1)
    first = (lane >= lane0) & (lane < lane0 + half)
    second = (lane >= lane0 + half) & (lane < lane0 + 2 * half)
    return jnp.where(first, left, jnp.where(second, right, 0.0))


def _in_proj_kernel(x_ref, g_ref, w_ref, *rest, with_small_t):
    xn = _rms(x_ref[...], g_ref[...]).astype(BF16)
    out_refs = rest
    if with_small_t:
        wst_ref, out_refs, small_t_ref = rest[0], rest[1:-1], rest[-1]
        small_t_ref[...] = lax.dot_general(wst_ref[...], xn, (((1,), (1,)), ((), ())), preferred_element_type=F32)
    off = 0
    for ref in out_refs:
        n = ref.shape[-1]
        ref[...] = jnp.dot(xn, w_ref[:, off:off + n], preferred_element_type=F32).astype(ref.dtype)
        off += n


def _in_proj(x, g, w_packed, groups, tm, w_small_t=None):
    m = x.shape[0]
    width = w_packed.shape[1]
    assert sum(n for n, _ in groups) == width and m % tm == 0
    with_t = w_small_t is not None
    in_specs = [pl.BlockSpec((tm, D_MODEL), lambda i: (i, 0)),
                _const_spec((1, D_MODEL)),
                _const_spec((D_MODEL, width))]
    out_specs = [pl.BlockSpec((tm, n), lambda i: (i, 0)) for n, _ in groups]
    out_shape = [jax.ShapeDtypeStruct((m, n), dt) for n, dt in groups]
    args = [x, g.reshape(1, D_MODEL), w_packed]
    if with_t:
        in_specs.append(_const_spec((LANE, D_MODEL)))
        out_specs.append(pl.BlockSpec((LANE, tm), lambda i: (0, i)))
        out_shape.append(jax.ShapeDtypeStruct((LANE, m), F32))
        args.append(w_small_t)
    return pl.pallas_call(
        functools.partial(_in_proj_kernel, with_small_t=with_t),
        grid=(m // tm,),
        in_specs=in_specs,
        out_specs=out_specs,
        out_shape=out_shape,
        compiler_params=_cparams(("parallel",)),
        name="in_proj",
    )(*args)


def _small_cols(a_beta, a_alpha, b_kr, c_i, c_f):
    small = jnp.zeros((D_MODEL, LANE), F32)
    small = small.at[:, SM_BETA:SM_BETA + GDN_HEADS].set(a_beta)
    small = small.at[:, SM_ALPHA:SM_ALPHA + GDN_HEADS].set(a_alpha)
    small = small.at[:, SM_I:SM_I + ML_HEADS].set(c_i)
    small = small.at[:, SM_F:SM_F + ML_HEADS].set(c_f)
    small = small.at[:, SM_KR:SM_KR + MLA_ROPE].set(b_kr)
    return small


PROMPT_GROUPS = ((GDN_QKV, F32), (GDN_V, BF16), (MLA_Q_LORA, BF16), (MLA_KV_LORA, F32),
                 (ML_QKV, BF16), (ML_V, BF16), (3 * D_MODEL, BF16), (LANE, F32))
SAMPLE_GROUPS = ((GDN_QKV, F32), (GDN_V, F32), (MLA_Q_LORA, F32), (MLA_KV_LORA, F32),
                 (2 * ML_HEADS * LANE + ML_V, F32), (ML_V, F32), (3 * D_MODEL, F32), (LANE, F32))


def _pack_w_in(w_in, pad_ml_heads):
    (a_qkv, a_beta, a_alpha, a_z, b_cq, b_ckv, b_kr,
     c_qkv, c_i, c_f, c_o, gates) = jnp.split(w_in, np.cumsum(IN_SIZES)[:-1].tolist(), axis=-1)
    if pad_ml_heads:
        qk = c_qkv[:, :2 * ML_QK].reshape(D_MODEL, 2 * ML_HEADS, ML_DK)
        qk = jnp.pad(qk, ((0, 0), (0, 0), (0, LANE - ML_DK))).reshape(D_MODEL, 2 * ML_HEADS * LANE)
        c_qkv = jnp.concatenate([qk, c_qkv[:, 2 * ML_QK:]], axis=1)
    small = _small_cols(a_beta, a_alpha, b_kr, c_i, c_f)
    packed = jnp.concatenate([a_qkv, a_z, b_cq, b_ckv, c_qkv, c_o, gates, small], axis=1).astype(BF16)
    return packed, small.T.astype(BF16)


def _gdn_prompt_kernel(aqkv_ref, az_ref, small_ref, small_t_ref, convw_ref, alog_ref, dtb_ref,
                       alog_c_ref, dtb_c_ref, ng_ref,
                       o_ref, conv_out_ref, s_out_ref, xp_scr, s_scr, *, tb):
    t = pl.program_id(1)
    c = REC_CHUNK
    nc = tb // c
    heads = range(GDN_HEADS)
    pairs = [(ci, h) for ci in range(nc) for h in heads]

    @pl.when(t == 0)
    def _():
        s_scr[...] = jnp.zeros_like(s_scr)
        xp_scr[0:8, :] = jnp.zeros((8, GDN_QKV), F32)

    x = aqkv_ref[...]
    xp_scr[8:8 + tb, :] = x
    w = convw_ref[...]
    y = (x * w[3:4] + xp_scr[7:7 + tb, :] * w[2:3]
         + xp_scr[6:6 + tb, :] * w[1:2] + xp_scr[5:5 + tb, :] * w[0:1])
    xp_scr[0:8, :] = x[tb - 8:tb, :]
    y = _silu(y)
    sm = small_ref[...]
    beta = _sigmoid(sm[:, SM_BETA:SM_BETA + GDN_HEADS])
    g_c = -jnp.exp(alog_ref[...]) * _softplus(sm[:, SM_ALPHA:SM_ALPHA + GDN_HEADS] + dtb_ref[...])
    g_r = -jnp.exp(alog_c_ref[...]) * _softplus(small_t_ref[SM_ALPHA:SM_ALPHA + GDN_HEADS, :] + dtb_c_ref[...])
    cmask = _chunk_cumsum_mask(tb, c)
    big_c = _dot01(cmask, g_c)
    big_r = _dot01_r(g_r, cmask.T)

    incl, strict, eye = _tri_masks(c)
    eye_f = jnp.where(eye, 1.0, 0.0).astype(F32)
    ng = ng_ref[...]
    rows = lambda ci: slice(ci * c, (ci + 1) * c)

    qs = [_l2(y[rows(ci), h * GDN_DK:(h + 1) * GDN_DK]) * (GDN_DK ** -0.5) for ci, h in pairs]
    ks = [_l2(y[rows(ci), GDN_QK + h * GDN_DK:GDN_QK + (h + 1) * GDN_DK]) for ci, h in pairs]
    vs = [y[rows(ci), 2 * GDN_QK + h * GDN_DV:2 * GDN_QK + (h + 1) * GDN_DV] for ci, h in pairs]
    bcols = [beta[rows(ci), h:h + 1] for ci, h in pairs]
    gcols = [big_c[rows(ci), h:h + 1] for ci, h in pairs]
    decs = [jnp.exp(jnp.where(incl, big_c[rows(ci), h:h + 1] - big_r[h:h + 1, rows(ci)], -jnp.inf))
            for ci, h in pairs]
    qkk = [_dot_nt(jnp.concatenate([q, k], axis=0), k) for q, k in zip(qs, ks)]
    a_list = [b * m[c:] * jnp.where(strict, d, 0.0) for b, m, d in zip(bcols, qkk, decs)]
    tinvs = _neumann_inverse_many(a_list, eye_f)
    e_gs = [jnp.exp(g) for g in gcols]
    uws = [_dot(ti, jnp.concatenate([b * v, (b * e) * k], axis=1))
           for ti, b, v, e, k in zip(tinvs, bcols, vs, e_gs, ks)]
    wqs = [jnp.concatenate([uw[:, GDN_DV:], q * e], axis=0).astype(BF16) for uw, q, e in zip(uws, qs, e_gs)]
    qkd = [(m[:c] * d).astype(BF16) for m, d in zip(qkk, decs)]
    kds = [(k * jnp.exp(g[c - 1:c, :] - g)).astype(BF16) for k, g in zip(ks, gcols)]
    gls = [jnp.exp(jnp.broadcast_to(g[c - 1:c, :], (1, GDN_DV))) for g in gcols]

    s_st = [s_scr[h] for h in heads]
    for ci in range(nc):
        idx = [ci * GDN_HEADS + h for h in heads]
        ws_qs = [jnp.dot(wqs[i], s_st[h].astype(BF16), preferred_element_type=F32)
                 for h, i in zip(heads, idx)]
        v_new = [(uws[i][:, :GDN_DV] - r[:c]).astype(BF16) for i, r in zip(idx, ws_qs)]
        o2 = [jnp.dot(qkd[i], vn, preferred_element_type=F32) for i, vn in zip(idx, v_new)]
        ds = [_dot_tn(kds[i], vn) for i, vn in zip(idx, v_new)]
        for h, i in zip(heads, idx):
            s_st[h] = gls[i] * s_st[h] + ds[h]
            o = ws_qs[h][c:] + o2[h]
            z = az_ref[rows(ci), h * GDN_DV:(h + 1) * GDN_DV].astype(F32)
            o_ref[rows(ci), h * GDN_DV:(h + 1) * GDN_DV] = (_rms(o, ng) * _silu(z)).astype(o_ref.dtype)
    for h in heads:
        s_scr[h] = s_st[h]

    @pl.when(t == pl.num_programs(1) - 1)
    def _():
        conv_out_ref[0] = x[tb - (GDN_CONV - 1):tb, :]
        s_out_ref[0] = s_scr[...]


def _gdn_prompt(a_qkv, a_z, small, small_t, conv_w, a_log, dt_bias, norm_g, bsz, seq, tb):
    nt = seq // tb
    row = lambda b, t: (b * nt + t, 0)
    col = lambda b, t: (0, b * nt + t)
    return pl.pallas_call(
        functools.partial(_gdn_prompt_kernel, tb=tb),
        grid=(bsz, nt),
        in_specs=[pl.BlockSpec((tb, GDN_QKV), row),
                  pl.BlockSpec((tb, GDN_V), row),
                  pl.BlockSpec((tb, LANE), row),
                  pl.BlockSpec((LANE, tb), col),
                  _const_spec((GDN_CONV, GDN_QKV)),
                  _const_spec((1, GDN_HEADS)),
                  _const_spec((1, GDN_HEADS)),
                  _const_spec((GDN_HEADS, 1)),
                  _const_spec((GDN_HEADS, 1)),
                  _const_spec((1, GDN_DV))],
        out_specs=[pl.BlockSpec((tb, GDN_V), row),
                   pl.BlockSpec((1, GDN_CONV - 1, GDN_QKV), lambda b, t: (b, 0, 0)),
                   pl.BlockSpec((1, GDN_HEADS, GDN_DK, GDN_DV), lambda b, t: (b, 0, 0, 0))],
        out_shape=[jax.ShapeDtypeStruct((bsz * seq, GDN_V), BF16),
                   jax.ShapeDtypeStruct((bsz, GDN_CONV - 1, GDN_QKV), F32),
                   jax.ShapeDtypeStruct((bsz, GDN_HEADS, GDN_DK, GDN_DV), F32)],
        scratch_shapes=[pltpu.VMEM((tb + 8, GDN_QKV), F32),
                        pltpu.VMEM((GDN_HEADS, GDN_DK, GDN_DV), F32)],
        compiler_params=_cparams(("parallel", "arbitrary")),
        name="gdn_prompt",
    )(a_qkv, a_z, small, small_t, conv_w, a_log.reshape(1, GDN_HEADS), dt_bias.reshape(1, GDN_HEADS),
      a_log.reshape(GDN_HEADS, 1), dt_bias.reshape(GDN_HEADS, 1), norm_g.reshape(1, GDN_DV))


def _mlstm_prompt_kernel(cqkv_ref, co_ref, small_ref, small_t_ref, bi_ref, bf_ref, bi_c_ref, bf_c_ref, ng_ref,
                         o_ref, c_out_ref, n_out_ref, m_out_ref, cn_scr, m_scr, *, tb):
    t = pl.program_id(1)
    c = ML_CHUNK
    nc = tb // c
    heads = range(ML_HEADS)
    pairs = [(ci, h) for ci in range(nc) for h in heads]

    @pl.when(t == 0)
    def _():
        cn_scr[...] = jnp.zeros_like(cn_scr)
        m_scr[...] = jnp.zeros_like(m_scr)

    sm = small_ref[...]
    ip_c = sm[:, SM_I:SM_I + ML_HEADS] + bi_ref[...]
    lf_c = -_softplus(-(sm[:, SM_F:SM_F + ML_HEADS] + bf_ref[...]))
    ip_r = small_t_ref[SM_I:SM_I + ML_HEADS, :] + bi_c_ref[...]
    lf_r = -_softplus(-(small_t_ref[SM_F:SM_F + ML_HEADS, :] + bf_c_ref[...]))
    cmask = _chunk_cumsum_mask(tb, c)
    f_c = _dot01(cmask, lf_c)
    f_r = _dot01_r(lf_r, cmask.T)

    incl, _, _ = _tri_masks(c)
    ng = ng_ref[...]
    rows = lambda ci: slice(ci * c, (ci + 1) * c)
    ones = jnp.ones((c, ML_DV), BF16)
    rep = lambda col: jnp.broadcast_to(col, (c, ML_DV))

    qs = [cqkv_ref[rows(ci), h * ML_DK:(h + 1) * ML_DK] for ci, h in pairs]
    ks = [cqkv_ref[rows(ci), ML_QK + h * ML_DK:ML_QK + (h + 1) * ML_DK].astype(F32) * (ML_DK ** -0.5)
          for ci, h in pairs]
    v1s = [jnp.concatenate([cqkv_ref[rows(ci), 2 * ML_QK + h * ML_DV:2 * ML_QK + (h + 1) * ML_DV], ones], axis=1)
           for ci, h in pairs]
    fcols = [f_c[rows(ci), h:h + 1] for ci, h in pairs]
    ds = [jnp.where(incl, f_c[rows(ci), h:h + 1] - f_r[h:h + 1, rows(ci)] + ip_r[h:h + 1, rows(ci)], -jnp.inf)
          for ci, h in pairs]
    dmaxs = [jnp.max(d, axis=-1, keepdims=True) for d in ds]
    qks = [_dot_nt(q, k) for q, k in zip(qs, ks)]
    p_locs = [jnp.exp(d - dm) * qk for d, dm, qk in zip(ds, dmaxs, qks)]
    pvs = [_dot(p, v1) for p, v1 in zip(p_locs, v1s)]
    wks = [jnp.exp(f[c - 1:c, :] - f + ip_c[rows(ci), h:h + 1] - dm[c - 1:c, :]) * k
           for (ci, h), f, dm, k in zip(pairs, fcols, dmaxs, ks)]
    kvs = [_dot_tn(wk, v1) for wk, v1 in zip(wks, v1s)]
    fbs = [rep(f) for f in fcols]
    dmbs = [rep(dm) for dm in dmaxs]

    cn_st = [cn_scr[h] for h in heads]
    m_st = [m_scr[h:h + 1, :] for h in heads]
    for ci in range(nc):
        idx = [ci * ML_HEADS + h for h in heads]
        qcs = [_dot(qs[i], cn_st[h]) for h, i in zip(heads, idx)]
        for h, i in zip(heads, idx):
            b_log = fbs[i] + m_st[h]
            m_r = jnp.maximum(b_log, dmbs[i])
            inter = jnp.exp(b_log - m_r)
            sc = jnp.exp(dmbs[i] - m_r)
            num = inter * qcs[h][:, :ML_DV] + sc * pvs[i][:, :ML_DV]
            den = inter * qcs[h][:, ML_DV:] + sc * pvs[i][:, ML_DV:]
            h_t = num / jnp.maximum(jnp.abs(den), jnp.exp(-m_r))
            m_new = m_r[c - 1:c, :]
            decay = jnp.exp(b_log[c - 1:c, :] - m_new)
            ksc = jnp.exp(dmbs[i][c - 1:c, :] - m_new)
            cn_st[h] = (jnp.concatenate([decay, decay], axis=1) * cn_st[h]
                        + jnp.concatenate([ksc, ksc], axis=1) * kvs[i])
            m_st[h] = m_new
            og = _sigmoid(co_ref[rows(ci), h * ML_DV:(h + 1) * ML_DV].astype(F32))
            o_ref[rows(ci), h * ML_DV:(h + 1) * ML_DV] = _rms(og * h_t, ng).astype(o_ref.dtype)
    for h in heads:
        cn_scr[h] = cn_st[h]
        m_scr[h:h + 1, :] = m_st[h]

    @pl.when(t == pl.num_programs(1) - 1)
    def _():
        c_out_ref[0] = cn_scr[:, :, 0:ML_DV]
        n_out_ref[0] = cn_scr[:, :, ML_DV:]
        m_out_ref[0] = m_scr[...]


def _mlstm_prompt(c_qkv, c_o, small, small_t, b_i, b_f, norm_g, bsz, seq, tb):
    nt = seq // tb
    row = lambda b, t: (b * nt + t, 0)
    col = lambda b, t: (0, b * nt + t)
    return pl.pallas_call(
        functools.partial(_mlstm_prompt_kernel, tb=tb),
        grid=(bsz, nt),
        in_specs=[pl.BlockSpec((tb, ML_QKV), row),
                  pl.BlockSpec((tb, ML_V), row),
                  pl.BlockSpec((tb, LANE), row),
                  pl.BlockSpec((LANE, tb), col),
                  _const_spec((1, ML_HEADS)),
                  _const_spec((1, ML_HEADS)),
                  _const_spec((ML_HEADS, 1)),
                  _const_spec((ML_HEADS, 1)),
                  _const_spec((1, ML_DV))],
        out_specs=[pl.BlockSpec((tb, ML_V), row),
                   pl.BlockSpec((1, ML_HEADS, ML_DK, ML_DV), lambda b, t: (b, 0, 0, 0)),
                   pl.BlockSpec((1, ML_HEADS, ML_DK, ML_DV), lambda b, t: (b, 0, 0, 0)),
                   pl.BlockSpec((1, ML_HEADS, LANE), lambda b, t: (b, 0, 0))],
        out_shape=[jax.ShapeDtypeStruct((bsz * seq, ML_V), BF16),
                   jax.ShapeDtypeStruct((bsz, ML_HEADS, ML_DK, ML_DV), F32),
                   jax.ShapeDtypeStruct((bsz, ML_HEADS, ML_DK, ML_DV), F32),
                   jax.ShapeDtypeStruct((bsz, ML_HEADS, LANE), F32)],
        scratch_shapes=[pltpu.VMEM((ML_HEADS, ML_DK, 2 * ML_DV), F32),
                        pltpu.VMEM((ML_HEADS, LANE), F32)],
        compiler_params=_cparams(("parallel", "arbitrary")),
        name="mlstm_prompt",
    )(c_qkv, c_o, small, small_t, b_i.reshape(1, ML_HEADS), b_f.reshape(1, ML_HEADS),
      b_i.reshape(ML_HEADS, 1), b_f.reshape(ML_HEADS, 1), norm_g.reshape(1, ML_DV))


def _rope_tables(pos):
    half = MLA_ROPE // 2
    inv_freq = ROPE_BASE ** (-jnp.arange(half, dtype=F32) / half)
    ang = pos.astype(F32)[:, None] * inv_freq[None, :]
    cos, sin = jnp.cos(ang), jnp.sin(ang)
    n = pos.shape[0]
    ones = jnp.ones((n, MLA_NOPE), F32)
    zeros = jnp.zeros((n, MLA_NOPE), F32)
    pad = jnp.zeros((n, LANE - MLA_QHEAD), F32)
    ctab = jnp.concatenate([ones, cos, cos, pad], axis=1)
    stab = jnp.concatenate([zeros, -sin, sin, pad], axis=1)
    return ctab, stab


def _mla_pre_kernel(cq_ref, ckv_ref, small_ref, ctab_ref, stab_ref, qg_ref, kvg_ref,
                    wuq_ref, wuk_ref, wuv_ref,
                    q_ref, k_ref, v_ref, ckv_out_ref, kr_out_ref):
    ctab = ctab_ref[...]
    stab = stab_ref[...]
    lane = lax.broadcasted_iota(jnp.int32, ctab.shape, 1)
    in_rope = (lane >= MLA_NOPE) & (lane < MLA_QHEAD)
    sm = small_ref[...]
    kr = jnp.where(in_rope, sm * ctab + _rope_swap(sm, MLA_NOPE) * stab, 0.0)
    kr_out_ref[0] = kr.T[MLA_NOPE:MLA_QHEAD, :]
    c_kv = _rms(ckv_ref[...], kvg_ref[...])
    ckv_out_ref[...] = c_kv
    c_kv_b = c_kv.astype(BF16)
    k_nope = jnp.dot(c_kv_b, wuk_ref[...], preferred_element_type=F32)
    v = jnp.dot(c_kv_b, wuv_ref[...], preferred_element_type=F32)
    v_t = v.T
    ones = jnp.ones((MLA_VDIM, v_t.shape[1]), v_ref.dtype)
    for h in range(MLA_HEADS):
        v_ref[0, h * HEAD_PAD:h * HEAD_PAD + MLA_VDIM, :] = v_t[h * MLA_VDIM:(h + 1) * MLA_VDIM, :].astype(v_ref.dtype)
        v_ref[0, h * HEAD_PAD + MLA_VDIM:(h + 1) * HEAD_PAD, :] = ones
    qn = _rms(cq_ref[...].astype(F32), qg_ref[...]).astype(BF16)
    q = jnp.dot(qn, wuq_ref[...], preferred_element_type=F32)
    q_sw = _rope_swap(q, MLA_NOPE)
    for h in range(MLA_HEADS):
        sl = slice(h * HEAD_PAD, (h + 1) * HEAD_PAD)
        q_ref[:, sl] = ((q[:, sl] * ctab + q_sw[:, sl] * stab) * (MLA_SCALE * LOG2E)).astype(q_ref.dtype)
        k_ref[:, sl] = (k_nope[:, sl] + kr).astype(k_ref.dtype)


def _pad_heads(w, head_dim):
    rows = w.shape[0]
    w = w.reshape(rows, MLA_HEADS, head_dim)
    return jnp.pad(w, ((0, 0), (0, 0), (0, HEAD_PAD - head_dim))).reshape(rows, MLA_HEADS * HEAD_PAD)


def _mla_pre(b_cq, b_ckv, small, ctab, stab, q_norm_g, kv_norm_g, w_uq_pad, w_uk_pad, w_uv, seq, tm):
    m = b_cq.shape[0]
    nt = seq // tm
    row = lambda i: (i, 0)
    tab = lambda i: (i % nt, 0)
    hp = MLA_HEADS * HEAD_PAD
    return pl.pallas_call(
        _mla_pre_kernel,
        grid=(m // tm,),
        in_specs=[pl.BlockSpec((tm, MLA_Q_LORA), row),
                  pl.BlockSpec((tm, MLA_KV_LORA), row),
                  pl.BlockSpec((tm, LANE), row),
                  pl.BlockSpec((tm, LANE), tab),
                  pl.BlockSpec((tm, LANE), tab),
                  _const_spec((1, MLA_Q_LORA)),
                  _const_spec((1, MLA_KV_LORA)),
                  _const_spec((MLA_Q_LORA, hp)),
                  _const_spec((MLA_KV_LORA, hp)),
                  _const_spec((MLA_KV_LORA, MLA_V))],
        out_specs=[pl.BlockSpec((tm, hp), row),
                   pl.BlockSpec((tm, hp), row),
                   pl.BlockSpec((1, hp, tm), lambda i: (i // nt, 0, i % nt)),
                   pl.BlockSpec((tm, MLA_KV_LORA), row),
                   pl.BlockSpec((1, MLA_ROPE, tm), lambda i: (i // nt, 0, i % nt))],
        out_shape=[jax.ShapeDtypeStruct((m, hp), BF16),
                   jax.ShapeDtypeStruct((m, hp), BF16),
                   jax.ShapeDtypeStruct((m // seq, hp, seq), BF16),
                   jax.ShapeDtypeStruct((m, MLA_KV_LORA), F32),
                   jax.ShapeDtypeStruct((m // seq, MLA_ROPE, seq), F32)],
        compiler_params=_cparams(("parallel",)),
        name="mla_pre",
    )(b_cq, b_ckv, small, ctab, stab, q_norm_g.reshape(1, -1), kv_norm_g.reshape(1, -1),
      w_uq_pad, w_uk_pad, w_uv)


def _flash_kernel(q_ref, k_ref, vt_ref, o_ref, m_scr, acc_scr, *, bq, bk):
    i = pl.program_id(1)
    n_full = (i * bq) // bk
    hsl = lambda h: slice(h * HEAD_PAD, (h + 1) * HEAD_PAD)
    m_scr[...] = jnp.full_like(m_scr, -jnp.inf)
    acc_scr[...] = jnp.zeros_like(acc_scr)

    def step(j, masked, q0):
        k0 = pl.multiple_of(j * bk, bk)
        nq = bq - q0
        if masked:
            kpos = j * bk + lax.broadcasted_iota(jnp.int32, (bk, nq), 0)
            qpos = i * bq + q0 + lax.broadcasted_iota(jnp.int32, (bk, nq), 1)
            keep = kpos <= qpos
        sts = [lax.dot_general(k_ref[pl.ds(k0, bk), hsl(h)], q_ref[q0:bq, hsl(h)], (((1,), (1,)), ((), ())),
                               preferred_element_type=F32) for h in range(MLA_HEADS)]
        for h in range(MLA_HEADS):
            st = jnp.where(keep, sts[h], -jnp.inf) if masked else sts[h]
            m_prev = m_scr[h, :, q0:bq]
            m_next = jnp.maximum(m_prev, jnp.max(st, axis=0, keepdims=True))
            p_t = jnp.exp2(st - m_next[0:1, :])
            alpha = jnp.exp2(m_prev - m_next)[0:1, :]
            acc_scr[h, :, q0:bq] = alpha * acc_scr[h, :, q0:bq] + jnp.dot(
                vt_ref[0, hsl(h), pl.ds(k0, bk)], p_t.astype(BF16), preferred_element_type=F32)
            m_scr[h, :, q0:bq] = m_next

    def body_full(j, carry):
        step(j, False, 0)
        return carry

    lax.fori_loop(0, n_full, body_full, 0)
    for d in range(bq // bk):
        step(n_full + d, True, d * bk)
    outs = []
    for h in range(MLA_HEADS):
        a = acc_scr[h]
        outs.append(a[0:MLA_VDIM, :] / a[MLA_VDIM:MLA_VDIM + 1, :])
    o_ref[...] = jnp.concatenate(outs, axis=0).T.astype(o_ref.dtype)


def _flash(q, k, v_t, bsz, seq, bq, bk):
    nq = seq // bq
    hp = MLA_HEADS * HEAD_PAD
    return pl.pallas_call(
        functools.partial(_flash_kernel, bq=bq, bk=bk),
        grid=(bsz, nq),
        in_specs=[pl.BlockSpec((bq, hp), lambda b, i: (b * nq + i, 0)),
                  pl.BlockSpec((seq, hp), lambda b, i: (b, 0)),
                  pl.BlockSpec((1, hp, seq), lambda b, i: (b, 0, 0))],
        out_specs=pl.BlockSpec((bq, MLA_V), lambda b, i: (b * nq + i, 0)),
        out_shape=jax.ShapeDtypeStruct((bsz * seq, MLA_V), BF16),
        scratch_shapes=[pltpu.VMEM((MLA_HEADS, 8, bq), F32), pltpu.VMEM((MLA_HEADS, HEAD_PAD, bq), F32)],
        compiler_params=_cparams(("parallel", "arbitrary")),
        name="mla_flash",
    )(q, k, v_t)


def _merge_kernel(x_ref, oa_ref, ob_ref, oc_ref, gates_ref, wa_ref, wb_ref, wc_ref, wo_ref, y_ref):
    g = gates_ref
    merged = (_sigmoid(g[:, 0:D_MODEL].astype(F32)) * _dot(oa_ref[...], wa_ref[...])
              + _sigmoid(g[:, D_MODEL:2 * D_MODEL].astype(F32)) * _dot(ob_ref[...], wb_ref[...])
              + _sigmoid(g[:, 2 * D_MODEL:].astype(F32)) * _dot(oc_ref[...], wc_ref[...]))
    y_ref[...] = x_ref[...] + _dot(merged, wo_ref[...])


def _merge(x, o_a, o_b, o_c, gates, w_a, w_b, w_c, w_o, tm):
    m = x.shape[0]
    row = lambda i: (i, 0)
    return pl.pallas_call(
        _merge_kernel,
        grid=(m // tm,),
        in_specs=[pl.BlockSpec((tm, D_MODEL), row),
                  pl.BlockSpec((tm, GDN_V), row),
                  pl.BlockSpec((tm, MLA_V), row),
                  pl.BlockSpec((tm, ML_V), row),
                  pl.BlockSpec((tm, 3 * D_MODEL), row),
                  _const_spec((GDN_V, D_MODEL)),
                  _const_spec((MLA_V, D_MODEL)),
                  _const_spec((ML_V, D_MODEL)),
                  _const_spec((D_MODEL, D_MODEL))],
        out_specs=pl.BlockSpec((tm, D_MODEL), row),
        out_shape=jax.ShapeDtypeStruct((m, D_MODEL), F32),
        compiler_params=_cparams(("parallel",)),
        name="merge",
    )(x, o_a, o_b, o_c, gates, w_a, w_b, w_c, w_o)


def _ffn_prompt_kernel(x_ref, g_ref, wup_ref, cw_ref, cb_ref, wdn_ref, gf_ref,
                       y_ref, conv_out_ref, gp_scr, *, tm, final):
    t = pl.program_id(1)

    @pl.when(t == 0)
    def _():
        gp_scr[0:8, :] = jnp.zeros((8, D_FF), F32)

    x = x_ref[...]
    hn = _rms(x, g_ref[...]).astype(BF16)
    u = jnp.dot(hn, wup_ref[:, 0:D_FF], preferred_element_type=F32)
    gt = jnp.dot(hn, wup_ref[:, D_FF:], preferred_element_type=F32)
    gp_scr[8:8 + tm, :] = gt
    cw = cw_ref[...]
    conv = gt * cw[2:3] + gp_scr[7:7 + tm, :] * cw[1:2] + gp_scr[6:6 + tm, :] * cw[0:1]
    gp_scr[0:8, :] = gt[tm - 8:tm, :]
    act = _silu(conv + cb_ref[...]) * u
    y = x + _dot(act, wdn_ref[...])
    if final:
        y = _rms(y, gf_ref[...])
    y_ref[...] = y

    @pl.when(t == pl.num_programs(1) - 1)
    def _():
        conv_out_ref[0] = gt[tm - (FFN_CONV - 1):tm, :]


def _ffn_prompt(x, norm_g, w_up, conv_w, conv_b, w_down, final_g, bsz, seq, tm, final):
    nt = seq // tm
    row = lambda b, t: (b * nt + t, 0)
    return pl.pallas_call(
        functools.partial(_ffn_prompt_kernel, tm=tm, final=final),
        grid=(bsz, nt),
        in_specs=[pl.BlockSpec((tm, D_MODEL), row),
                  _const_spec((1, D_MODEL)),
                  _const_spec((D_MODEL, 2 * D_FF)),
                  _const_spec((FFN_CONV, D_FF)),
                  _const_spec((1, D_FF)),
                  _const_spec((D_FF, D_MODEL)),
                  _const_spec((1, D_MODEL))],
        out_specs=[pl.BlockSpec((tm, D_MODEL), row),
                   pl.BlockSpec((1, FFN_CONV - 1, D_FF), lambda b, t: (b, 0, 0))],
        out_shape=[jax.ShapeDtypeStruct((bsz * seq, D_MODEL), F32),
                   jax.ShapeDtypeStruct((bsz, FFN_CONV - 1, D_FF), F32)],
        scratch_shapes=[pltpu.VMEM((tm + 8, D_FF), F32)],
        compiler_params=_cparams(("parallel", "arbitrary")),
        name="ffn_prompt",
    )(x, norm_g.reshape(1, -1), w_up, conv_w, conv_b.reshape(1, -1), w_down, final_g.reshape(1, -1))


def _ffn_sample_kernel(x_ref, hist_ref, g_ref, wup_ref, cw_ref, cb_ref, wdn_ref, gf_ref,
                       y_ref, hist_out_ref, *, final):
    x = x_ref[...]
    hn = _rms(x, g_ref[...]).astype(BF16)
    u = jnp.dot(hn, wup_ref[:, 0:D_FF], preferred_element_type=F32)
    gt = jnp.dot(hn, wup_ref[:, D_FF:], preferred_element_type=F32)
    cw = cw_ref[...]
    conv = hist_ref[0] * cw[0:1] + hist_ref[1] * cw[1:2] + gt * cw[2:3]
    act = _silu(conv + cb_ref[...]) * u
    y = x + _dot(act, wdn_ref[...])
    if final:
        y = _rms(y, gf_ref[...])
    y_ref[...] = y
    hist_out_ref[0] = hist_ref[1]
    hist_out_ref[1] = gt


def _ffn_sample(x, hist_t, norm_g, w_up, conv_w, conv_b, w_down, final_g, final):
    bs = x.shape[0]
    return pl.pallas_call(
        functools.partial(_ffn_sample_kernel, final=final),
        out_shape=[jax.ShapeDtypeStruct((bs, D_MODEL), F32),
                   jax.ShapeDtypeStruct((FFN_CONV - 1, bs, D_FF), F32)],
        compiler_params=pltpu.CompilerParams(vmem_limit_bytes=VMEM_LIMIT),
        name="ffn_sample",
    )(x, hist_t, norm_g.reshape(1, -1), w_up, conv_w, conv_b.reshape(1, -1), w_down, final_g.reshape(1, -1))


def _row_to_col(row, eye):
    n = eye.shape[0]
    return jnp.sum(jnp.where(eye, jnp.broadcast_to(row, (n, n)), 0.0), axis=1, keepdims=True)


def _rec_sample_kernel(aqkv_ref, hist_ref, az_ref, small_ref, convw_ref, alog_ref, dtb_ref, gng_ref,
                       s0_ref, cqkv_ref, co_ref, bi_ref, bf_ref, mng_ref, c0_ref, n0_ref, m0_ref,
                       oa_ref, hist_out_ref, s_out_ref, oc_ref, c_out_ref, n_out_ref, m_out_ref):
    _, _, eye128 = _tri_masks(LANE)
    lane = lax.broadcasted_iota(jnp.int32, (1, LANE), 1)
    sm = small_ref[0]
    x = aqkv_ref[0]
    hist = hist_ref[0]
    w = convw_ref[...]
    y = _silu(hist[0:1] * w[0:1] + hist[1:2] * w[1:2] + hist[2:3] * w[2:3] + x * w[3:4])
    hist_out_ref[0, 0:1, :] = hist[1:2]
    hist_out_ref[0, 1:2, :] = hist[2:3]
    hist_out_ref[0, 2:3, :] = x
    beta_row = _sigmoid(sm)
    decay_row = jnp.exp(-jnp.exp(alog_ref[...]) * _softplus(sm + dtb_ref[...]))
    gng = gng_ref[...]
    az = az_ref[0]
    cq = cqkv_ref[0]
    co = co_ref[0]
    ip_row = sm + bi_ref[...]
    lf_row = -_softplus(-(sm + bf_ref[...]))
    mng = mng_ref[...]
    m_prev = m0_ref[0]
    gh, mh = range(GDN_HEADS), range(ML_HEADS)

    g_q = [_l2(y[:, h * GDN_DK:(h + 1) * GDN_DK]) * (GDN_DK ** -0.5) for h in gh]
    g_k = [_l2(y[:, GDN_QK + h * GDN_DK:GDN_QK + (h + 1) * GDN_DK]) for h in gh]
    g_v = [y[:, 2 * GDN_QK + h * GDN_DV:2 * GDN_QK + (h + 1) * GDN_DV] for h in gh]
    m_q = [cq[:, h * LANE:(h + 1) * LANE] for h in mh]
    m_k = [cq[:, (ML_HEADS + h) * LANE:(ML_HEADS + h + 1) * LANE] * (ML_DK ** -0.5) for h in mh]
    m_v = [cq[:, 2 * ML_HEADS * LANE + h * ML_DV:2 * ML_HEADS * LANE + (h + 1) * ML_DV] for h in mh]
    g_kc = [_row_to_col(k, eye128) for k in g_k]
    g_qc = [_row_to_col(q, eye128) for q in g_q]
    m_kc = [_row_to_col(k, eye128)[0:ML_DK] for k in m_k]
    m_qc = [_row_to_col(q, eye128)[0:ML_DK] for q in m_q]
    g_qk = [jnp.sum(q * k, axis=-1, keepdims=True) for q, k in zip(g_q, g_k)]
    m_qk = [jnp.sum(q * k, axis=-1, keepdims=True) for q, k in zip(m_q, m_k)]
    s_old = [s0_ref[0, h] for h in gh]
    c_old = [c0_ref[0, h] for h in mh]
    n_old = [n0_ref[0, h:h + 1, :] for h in mh]
    g_ks = [jnp.sum(kc * s, axis=0, keepdims=True) for kc, s in zip(g_kc, s_old)]
    g_qs = [jnp.sum(qc * s, axis=0, keepdims=True) for qc, s in zip(g_qc, s_old)]
    m_qc_c = [jnp.sum(qc * c, axis=0, keepdims=True) for qc, c in zip(m_qc, c_old)]
    m_qn = [jnp.sum(q * n, axis=-1, keepdims=True) for q, n in zip(m_q, n_old)]

    for h in gh:
        beta = beta_row[:, SM_BETA + h:SM_BETA + h + 1]
        a = decay_row[:, SM_ALPHA + h:SM_ALPHA + h + 1]
        v_new = beta * (g_v[h] - a * g_ks[h])
        o = a * g_qs[h] + g_qk[h] * v_new
        s_out_ref[0, h] = a * s_old[h] + g_kc[h] * v_new
        z = az[:, h * GDN_DV:(h + 1) * GDN_DV]
        oa_ref[0, :, h * GDN_DV:(h + 1) * GDN_DV] = _rms(o, gng) * _silu(z)

    m_out = jnp.zeros((1, LANE), F32)
    for h in mh:
        ip = ip_row[:, SM_I + h:SM_I + h + 1]
        lf = lf_row[:, SM_F + h:SM_F + h + 1]
        m_old = m_prev[:, h:h + 1]
        b_log = lf + m_old
        m_r = jnp.maximum(b_log, ip)
        inter = jnp.exp(b_log - m_r)
        w_key = jnp.exp(ip - m_r)
        p = w_key * m_qk[h]
        num = inter * m_qc_c[h] + p * m_v[h]
        den = inter * m_qn[h] + p
        h_t = num / jnp.maximum(jnp.abs(den), jnp.exp(-m_r))
        c_out_ref[0, h] = inter * c_old[h] + (w_key * m_kc[h]) * m_v[h]
        n_out_ref[0, h:h + 1, :] = inter * n_old[h] + w_key * m_k[h]
        m_out = jnp.where(lane == h, m_r, m_out)
        og = _sigmoid(co[:, h * ML_DV:(h + 1) * ML_DV])
        oc_ref[0, :, h * ML_DV:(h + 1) * ML_DV] = _rms(og * h_t, mng)
    m_out_ref[0] = m_out


def _rec_sample(a_qkv, hist, a_z, small, conv_w, alog_row, dtb_row, gdn_norm_g, s0_all,
                c_qkv, c_o, bi_row, bf_row, ml_norm_g, c0_all, n0_pad, m0_pad, layer):
    bs = a_qkv.shape[0]
    r3 = lambda a: a.reshape(bs, 1, a.shape[-1])
    b3 = lambda n: pl.BlockSpec((1, 1, n), lambda b: (b, 0, 0))
    mlw = 2 * ML_HEADS * LANE + ML_V
    return pl.pallas_call(
        _rec_sample_kernel,
        grid=(bs,),
        in_specs=[b3(GDN_QKV),
                  pl.BlockSpec((1, GDN_CONV - 1, GDN_QKV), lambda b: (b, 0, 0)),
                  b3(GDN_V), b3(LANE),
                  _const_spec((GDN_CONV, GDN_QKV)),
                  _const_spec((1, LANE)), _const_spec((1, LANE)), _const_spec((1, GDN_DV)),
                  pl.BlockSpec((None, 1, GDN_HEADS, GDN_DK, GDN_DV), lambda b: (layer, b, 0, 0, 0)),
                  b3(mlw), b3(ML_V),
                  _const_spec((1, LANE)), _const_spec((1, LANE)), _const_spec((1, ML_DV)),
                  pl.BlockSpec((None, 1, ML_HEADS, ML_DK, ML_DV), lambda b: (layer, b, 0, 0, 0)),
                  pl.BlockSpec((1, ML_HEADS, LANE), lambda b: (b, 0, 0)),
                  b3(LANE)],
        out_specs=[b3(GDN_V),
                   pl.BlockSpec((1, GDN_CONV - 1, GDN_QKV), lambda b: (b, 0, 0)),
                   pl.BlockSpec((1, GDN_HEADS, GDN_DK, GDN_DV), lambda b: (b, 0, 0, 0)),
                   b3(ML_V),
                   pl.BlockSpec((1, ML_HEADS, ML_DK, ML_DV), lambda b: (b, 0, 0, 0)),
                   pl.BlockSpec((1, ML_HEADS, LANE), lambda b: (b, 0, 0)),
                   b3(LANE)],
        out_shape=[jax.ShapeDtypeStruct((bs, 1, GDN_V), F32),
                   jax.ShapeDtypeStruct((bs, GDN_CONV - 1, GDN_QKV), F32),
                   jax.ShapeDtypeStruct((bs, GDN_HEADS, GDN_DK, GDN_DV), F32),
                   jax.ShapeDtypeStruct((bs, 1, ML_V), F32),
                   jax.ShapeDtypeStruct((bs, ML_HEADS, ML_DK, ML_DV), F32),
                   jax.ShapeDtypeStruct((bs, ML_HEADS, LANE), F32),
                   jax.ShapeDtypeStruct((bs, 1, LANE), F32)],
        compiler_params=_cparams(("parallel",)),
        name="rec_sample",
    )(r3(a_qkv), hist, r3(a_z), r3(small), conv_w, alog_row, dtb_row, gdn_norm_g.reshape(1, -1), s0_all,
      r3(c_qkv), r3(c_o), bi_row, bf_row, ml_norm_g.reshape(1, -1), c0_all, n0_pad, r3(m0_pad))


def _mla_sample_pre_kernel(cq_ref, ckv_ref, small_ref, ctab_ref, stab_ref, qg_ref, kvg_ref,
                           wuq_ref, wukt_ref, qlat_ref, qrope_ref, ckv_out_ref, kr_out_ref):
    ctab = ctab_ref[...]
    stab = stab_ref[...]
    lane = lax.broadcasted_iota(jnp.int32, (1, LANE), 1)
    in_rope = (lane >= MLA_NOPE) & (lane < MLA_QHEAD)
    sm = small_ref[...]
    kr_out_ref[...] = jnp.where(in_rope, sm * ctab + _rope_swap(sm, MLA_NOPE) * stab, 0.0)
    ckv_out_ref[...] = _rms(ckv_ref[...], kvg_ref[...])
    qn = _rms(cq_ref[...], qg_ref[...])
    q = _dot(qn, wuq_ref[...])
    q_sw = _rope_swap(q, MLA_NOPE)
    for h in range(MLA_HEADS):
        sl = slice(h * HEAD_PAD, (h + 1) * HEAD_PAD)
        qh = q[:, sl]
        qrope_ref[:, sl] = jnp.where(in_rope, qh * ctab + q_sw[:, sl] * stab, 0.0)
        qlat_ref[:, sl] = _dot(jnp.where(lane < MLA_NOPE, qh, 0.0), wukt_ref[h])


def _mla_sample_pre(b_cq, b_ckv, small, ctab, stab, q_norm_g, kv_norm_g, w_uq_pad, w_uk_t):
    bs = b_cq.shape[0]
    hp = MLA_HEADS * HEAD_PAD
    return pl.pallas_call(
        _mla_sample_pre_kernel,
        out_shape=[jax.ShapeDtypeStruct((bs, hp), F32),
                   jax.ShapeDtypeStruct((bs, hp), F32),
                   jax.ShapeDtypeStruct((bs, MLA_KV_LORA), F32),
                   jax.ShapeDtypeStruct((bs, LANE), F32)],
        compiler_params=pltpu.CompilerParams(vmem_limit_bytes=VMEM_LIMIT),
        name="mla_sample_pre",
    )(b_cq, b_ckv, small, ctab, stab, q_norm_g.reshape(1, -1), kv_norm_g.reshape(1, -1), w_uq_pad, w_uk_t)


def _mla_sample_kernel(pt_ref, qlat_ref, qrope_ref, ckv_ref, kr_ref, lat_hbm, krt_hbm, ctx_ref,
                       lat_buf, krt_buf, sem, *, layer, n_pages):
    b = pl.program_id(0)
    slot = b % 2

    def page_copies(row, buf_slot, g):
        page = pt_ref[row, g]
        off = pl.multiple_of(g * PAGE_SIZE, PAGE_SIZE)
        return (pltpu.make_async_copy(lat_hbm.at[layer, page], lat_buf.at[buf_slot, pl.ds(off, PAGE_SIZE), :],
                                      sem.at[0, buf_slot]),
                pltpu.make_async_copy(krt_hbm.at[layer, page], krt_buf.at[buf_slot, :, pl.ds(off, PAGE_SIZE)],
                                      sem.at[1, buf_slot]))

    def start_row(row, buf_slot):
        def body(g, carry):
            for cp in page_copies(row, buf_slot, g):
                cp.start()
            return carry
        lax.fori_loop(0, n_pages, body, 0, unroll=8)

    def wait_row(buf_slot):
        pltpu.make_async_copy(lat_buf.at[buf_slot], lat_buf.at[buf_slot], sem.at[0, buf_slot]).wait()
        pltpu.make_async_copy(krt_buf.at[buf_slot], krt_buf.at[buf_slot], sem.at[1, buf_slot]).wait()

    @pl.when(b == 0)
    def _():
        start_row(0, 0)

    @pl.when(b + 1 < pl.num_programs(0))
    def _():
        start_row(b + 1, 1 - slot)

    wait_row(slot)

    q_lat = qlat_ref[0].astype(BF16)
    q_rope = qrope_ref[0][:, MLA_NOPE:MLA_QHEAD].astype(BF16)
    lat = lat_buf[slot].astype(BF16)
    s = (_dot_nt(q_lat, lat) + _dot(q_rope, krt_buf[slot])) * MLA_SCALE
    c_new = ckv_ref[0].astype(BF16).astype(F32)
    kr_new = kr_ref[0][:, MLA_NOPE:MLA_QHEAD].astype(BF16).astype(F32)
    s_new = (jnp.sum(q_lat.astype(F32) * c_new, axis=-1, keepdims=True)
             + jnp.sum(q_rope.astype(F32) * kr_new, axis=-1, keepdims=True)) * MLA_SCALE
    m = jnp.maximum(jnp.max(s, axis=-1, keepdims=True), s_new)
    p = jnp.exp(s - m)
    p_new = jnp.exp(s_new - m)
    denom = jnp.sum(p, axis=-1, keepdims=True) + p_new
    ctx_ref[0] = (_dot(p, lat) + p_new.astype(BF16).astype(F32) * c_new) / denom


def _mla_sample(page_table, q_lat, q_rope, c_kv, kr, cache_lat, cache_krt, layer):
    bs, n_pages = page_table.shape
    past = n_pages * PAGE_SIZE
    hp3 = lambda a: a.reshape(bs, MLA_HEADS, HEAD_PAD)
    per_b = lambda n: pl.BlockSpec((1, n, LANE), lambda b, pt: (b, 0, 0))
    grid_spec = pltpu.PrefetchScalarGridSpec(
        num_scalar_prefetch=1,
        grid=(bs,),
        in_specs=[per_b(MLA_HEADS), per_b(MLA_HEADS), per_b(1), per_b(1),
                  pl.BlockSpec(memory_space=pl.ANY), pl.BlockSpec(memory_space=pl.ANY)],
        out_specs=per_b(MLA_HEADS),
        scratch_shapes=[pltpu.VMEM((2, past, MLA_KV_LORA), F32),
                        pltpu.VMEM((2, MLA_ROPE, past), F32),
                        pltpu.SemaphoreType.DMA((2, 2))],
    )
    return pl.pallas_call(
        functools.partial(_mla_sample_kernel, layer=layer, n_pages=n_pages),
        grid_spec=grid_spec,
        out_shape=jax.ShapeDtypeStruct((bs, MLA_HEADS, LANE), F32),
        compiler_params=_cparams(("arbitrary",)),
        name="mla_sample",
    )(page_table, hp3(q_lat), hp3(q_rope), c_kv.reshape(bs, 1, LANE), kr.reshape(bs, 1, LANE),
      cache_lat, cache_krt)


def _ctx_up_kernel(ctx_ref, wuv_ref, o_ref):
    o_ref[...] = _dot(ctx_ref[...], wuv_ref[...])


def _ctx_up(ctx, w_uv_bd):
    return pl.pallas_call(
        _ctx_up_kernel,
        out_shape=jax.ShapeDtypeStruct((ctx.shape[0], MLA_V), F32),
        name="ctx_up",
    )(ctx, w_uv_bd)


def _lane_row(vals, lane0):
    return jnp.zeros((1, LANE), F32).at[0, lane0:lane0 + vals.shape[0]].set(vals.astype(F32))


def _pick(n, pref):
    t = min(n, pref)
    while n % t:
        t //= 2
    return t


def kernel(x_prompt, x_sample, cache_kv_latent, cache_k_rope, state_gdn_conv, state_gdn_S,
           state_mlstm_C, state_mlstm_n, state_mlstm_m, state_ffn_conv, page_table,
           norm1_g, w_in, gdn_conv_w, gdn_A_log, gdn_dt_bias, gdn_norm_g,
           mla_q_norm_g, mla_w_uq, mla_kv_norm_g, mla_w_uk, mla_w_uv,
           ml_b_i, ml_b_f, ml_norm_g, w_branch_a, w_branch_b, w_branch_c, w_out,
           norm2_g, ffn_w_up, ffn_conv_w, ffn_conv_b, ffn_w_down, final_norm_g):
    bp, seq = x_prompt.shape[:2]
    bs, dec = x_sample.shape[:2]
    assert dec == 1 and seq % max(REC_CHUNK, ML_CHUNK) == 0
    depth = w_in.shape[0]
    n_pages = page_table.shape[1]
    past_len = n_pages * PAGE_SIZE
    mp = bp * seq

    tm = _pick(seq, 512)
    tb = _pick(seq, 512)
    bq = _pick(seq, 512)
    bk = _pick(seq, 256)

    ctab_p, stab_p = _rope_tables(jnp.arange(seq))
    ctab_s, stab_s = _rope_tables(jnp.full((1,), past_len))
    cache_krt = jnp.swapaxes(cache_k_rope, 2, 3)

    xp = x_prompt.reshape(mp, D_MODEL)
    xs = x_sample.reshape(bs, D_MODEL)
    out_p = [[] for _ in range(8)]
    out_s = [[] for _ in range(8)]

    for l in range(depth):
        last = l == depth - 1
        w_p, w_small_t = _pack_w_in(w_in[l], False)
        w_s, _ = _pack_w_in(w_in[l], True)
        w_uq_pad = _pad_heads(mla_w_uq[l], MLA_QHEAD).astype(BF16)
        w_uk_pad = _pad_heads(mla_w_uk[l], MLA_NOPE).astype(BF16)
        w_uv = mla_w_uv[l].astype(BF16)
        w_a, w_b, w_c, w_o = (w_branch_a[l].astype(BF16), w_branch_b[l].astype(BF16),
                              w_branch_c[l].astype(BF16), w_out[l].astype(BF16))
        w_up, w_dn = ffn_w_up[l].astype(BF16), ffn_w_down[l].astype(BF16)

        a_qkv, a_z, b_cq, b_ckv, c_qkv, c_o, gates, small, small_t = _in_proj(
            xp, norm1_g[l], w_p, PROMPT_GROUPS, tm, w_small_t)
        o_a, p_gconv, p_gs = _gdn_prompt(a_qkv, a_z, small, small_t, gdn_conv_w[l], gdn_A_log[l], gdn_dt_bias[l],
                                         gdn_norm_g[l], bp, seq, tb)
        q_pad, k_pad, v_all, p_ckv, p_kr = _mla_pre(b_cq, b_ckv, small, ctab_p, stab_p, mla_q_norm_g[l],
                                                    mla_kv_norm_g[l], w_uq_pad, w_uk_pad, w_uv, seq, tm)
        o_b = _flash(q_pad, k_pad, v_all, bp, seq, bq, bk)
        o_c, p_c, p_n, p_m = _mlstm_prompt(c_qkv, c_o, small, small_t, ml_b_i[l], ml_b_f[l], ml_norm_g[l], bp, seq, tb)
        xp = _merge(xp, o_a, o_b, o_c, gates, w_a, w_b, w_c, w_o, tm)
        xp, p_fconv = _ffn_prompt(xp, norm2_g[l], w_up, ffn_conv_w[l], ffn_conv_b[l], w_dn, final_norm_g,
                                  bp, seq, tm, last)
        for lst, a in zip(out_p, (p_ckv.reshape(bp, seq, MLA_KV_LORA), jnp.swapaxes(p_kr, 1, 2),
                                  p_gconv, p_gs, p_c, p_n[..., 0], p_m[..., 0], p_fconv)):
            lst.append(a)

        a_qkv, a_z, b_cq, b_ckv, c_qkv, c_o, gates, small = _in_proj(xs, norm1_g[l], w_s, SAMPLE_GROUPS, bs)
        n0_pad = jnp.pad(state_mlstm_n[l], ((0, 0), (0, 0), (0, LANE - ML_DK)))
        m0_pad = jnp.pad(state_mlstm_m[l], ((0, 0), (0, LANE - ML_HEADS)))
        o_a, s_gconv, s_gs, o_c, s_c, s_n, s_m = _rec_sample(
            a_qkv, state_gdn_conv[l], a_z, small, gdn_conv_w[l],
            _lane_row(gdn_A_log[l], SM_ALPHA), _lane_row(gdn_dt_bias[l], SM_ALPHA), gdn_norm_g[l],
            state_gdn_S, c_qkv, c_o, _lane_row(ml_b_i[l], SM_I), _lane_row(ml_b_f[l], SM_F),
            ml_norm_g[l], state_mlstm_C, n0_pad, m0_pad, l)
        w_uk_t = jnp.pad(jnp.transpose(mla_w_uk[l].reshape(MLA_KV_LORA, MLA_HEADS, MLA_NOPE), (1, 2, 0)),
                         ((0, 0), (0, HEAD_PAD - MLA_NOPE), (0, 0))).astype(BF16)
        q_lat, q_rope, s_ckv, kr_s = _mla_sample_pre(b_cq, b_ckv, small, ctab_s, stab_s, mla_q_norm_g[l],
                                                      mla_kv_norm_g[l], w_uq_pad, w_uk_t)
        ctx = _mla_sample(page_table, q_lat, q_rope, s_ckv, kr_s, cache_kv_latent, cache_krt, l)
        w_uv_bd = (jnp.eye(MLA_HEADS, dtype=F32)[:, None, :, None]
                   * jnp.transpose(mla_w_uv[l].reshape(MLA_KV_LORA, MLA_HEADS, MLA_VDIM), (1, 0, 2))[:, :, None, :]
                   ).reshape(MLA_HEADS * MLA_KV_LORA, MLA_V).astype(BF16)
        o_b = _ctx_up(ctx.reshape(bs, MLA_HEADS * LANE), w_uv_bd)
        xs = _merge(xs, o_a.reshape(bs, GDN_V), o_b, o_c.reshape(bs, ML_V), gates, w_a, w_b, w_c, w_o, bs)
        xs, s_fconv_t = _ffn_sample(xs, jnp.swapaxes(state_ffn_conv[l], 0, 1), norm2_g[l], w_up, ffn_conv_w[l],
                                    ffn_conv_b[l], w_dn, final_norm_g, last)
        for lst, a in zip(out_s, (s_ckv.reshape(bs, 1, MLA_KV_LORA),
                                  kr_s[:, MLA_NOPE:MLA_QHEAD].reshape(bs, 1, MLA_ROPE),
                                  s_gconv, s_gs, s_c, s_n[:, :, :ML_DK], s_m.reshape(bs, LANE)[:, :ML_HEADS],
                                  jnp.swapaxes(s_fconv_t, 0, 1))):
            lst.append(a)

    y_prompt = xp.reshape(bp, seq, D_MODEL)
    y_sample = xs.reshape(bs, 1, D_MODEL)
    return (y_prompt, y_sample) + tuple(jnp.stack(a) for a in out_p) + tuple(jnp.stack(a) for a in out_s)
```

```python
import functools
import math

import jax
import jax.numpy as jnp
import numpy as np
from jax import lax
from jax.experimental import pallas as pl
from jax.experimental.pallas import tpu as pltpu

F32 = jnp.float32
BF16 = jnp.bfloat16

D_MODEL = 1024
PAGE_SIZE = 128
GDN_HEADS, GDN_DK, GDN_DV, GDN_CONV = 4, 128, 128, 4
MLA_HEADS, MLA_Q_LORA, MLA_KV_LORA, MLA_NOPE, MLA_ROPE, MLA_VDIM = 8, 256, 128, 64, 32, 64
ROPE_BASE = 10000.0
ML_HEADS, ML_DK, ML_DV = 4, 64, 128
CHUNK = 64
REC_CHUNK = 64
ML_CHUNK = 128
D_FF = 2816
FFN_CONV = 3
NORM_EPS = 1e-6

GDN_QK = GDN_HEADS * GDN_DK
GDN_V = GDN_HEADS * GDN_DV
GDN_QKV = 2 * GDN_QK + GDN_V
MLA_QHEAD = MLA_NOPE + MLA_ROPE
MLA_V = MLA_HEADS * MLA_VDIM
MLA_SCALE = MLA_QHEAD ** -0.5
ML_QK = ML_HEADS * ML_DK
ML_V = ML_HEADS * ML_DV
ML_QKV = 2 * ML_QK + ML_V
IN_SIZES = (GDN_QKV, GDN_HEADS, GDN_HEADS, GDN_V, MLA_Q_LORA, MLA_KV_LORA, MLA_ROPE,
            ML_QKV, ML_HEADS, ML_HEADS, ML_V, 3 * D_MODEL)

LANE = 128
HEAD_PAD = 128
SM_BETA, SM_ALPHA, SM_I, SM_F, SM_KR = 0, 8, 16, 24, 64
VMEM_LIMIT = 56 * 1024 * 1024
LOG2E = math.log2(math.e)


def _cparams(sem):
    return pltpu.CompilerParams(dimension_semantics=sem, vmem_limit_bytes=VMEM_LIMIT)


def _const_spec(shape):
    nd = len(shape)
    return pl.BlockSpec(shape, lambda *_: (0,) * nd, pipeline_mode=pl.Buffered(1))


def _dot(a, b):
    return jnp.dot(a.astype(BF16), b.astype(BF16), preferred_element_type=F32)


def _dot_nt(a, b):
    return lax.dot_general(a.astype(BF16), b.astype(BF16), (((1,), (1,)), ((), ())),
                           preferred_element_type=F32)


def _dot_tn(a, b):
    return lax.dot_general(a.astype(BF16), b.astype(BF16), (((0,), (0,)), ((), ())),
                           preferred_element_type=F32)


def _dot01(mask_bf16, x):
    x1 = x.astype(BF16)
    r1 = x - x1.astype(F32)
    x2 = r1.astype(BF16)
    x3 = (r1 - x2.astype(F32)).astype(BF16)
    d = functools.partial(jnp.dot, preferred_element_type=F32)
    return d(mask_bf16, x1) + d(mask_bf16, x2) + d(mask_bf16, x3)


def _dot01_r(x, mask_bf16):
    x1 = x.astype(BF16)
    r1 = x - x1.astype(F32)
    x2 = r1.astype(BF16)
    x3 = (r1 - x2.astype(F32)).astype(BF16)
    d = functools.partial(jnp.dot, preferred_element_type=F32)
    return d(x1, mask_bf16) + d(x2, mask_bf16) + d(x3, mask_bf16)


def _rms(x, g):
    return x * lax.rsqrt(jnp.mean(x * x, axis=-1, keepdims=True) + NORM_EPS) * g


def _l2(x):
    return x * lax.rsqrt(jnp.sum(x * x, axis=-1, keepdims=True) + NORM_EPS)


def _sigmoid(x):
    return 0.5 * jnp.tanh(0.5 * x) + 0.5


def _silu(x):
    return x * _sigmoid(x)


def _softplus(x):
    return jnp.maximum(x, 0.0) + jnp.log1p(jnp.exp(-jnp.abs(x)))


def _tri_masks(n):
    r = lax.broadcasted_iota(jnp.int32, (n, n), 0)
    c = lax.broadcasted_iota(jnp.int32, (n, n), 1)
    return r >= c, r > c, r == c


def _chunk_cumsum_mask(tb, c):
    r = lax.broadcasted_iota(jnp.int32, (tb, tb), 0)
    q = lax.broadcasted_iota(jnp.int32, (tb, tb), 1)
    return jnp.where((r // c == q // c) & (r >= q), 1.0, 0.0).astype(BF16)


def _neumann_inverse_many(a_list, eye_f):
    n = eye_f.shape[0]
    ps = [-a for a in a_list]
    ts = [eye_f + p for p in ps]
    qs = [_dot(p, p) for p in ps]
    covered = 2
    while 2 * covered < n:
        rs = [_dot(q, jnp.concatenate([t, q], axis=1)) for t, q in zip(ts, qs)]
        ts = [t + r[:, :n] for t, r in zip(ts, rs)]
        qs = [r[:, n:] for r in rs]
        covered *= 2
    return [t + _dot(q, t) for t, q in zip(ts, qs)]


def _rope_swap(x, lane0):
    w = x.shape[-1]
    half = MLA_ROPE // 2
    lane = lax.broadcasted_iota(jnp.int32, x.shape, x.ndim - 1) % LANE
    left = pltpu.roll(x, w - half, x.ndim - 1)
    right = pltpu.roll(x, half, x.ndim - 1)
    first = (lane >= lane0) & (lane < lane0 + half)
    second = (lane >= lane0 + half) & (lane < lane0 + 2 * half)
    return jnp.where(first, left, jnp.where(second, right, 0.0))


def _in_proj_kernel(x_ref, g_ref, w_ref, *rest, with_small_t):
    xn = _rms(x_ref[...], g_ref[...]).astype(BF16)
    out_refs = rest
    if with_small_t:
        wst_ref, out_refs, small_t_ref = rest[0], rest[1:-1], rest[-1]
        small_t_ref[...] = lax.dot_general(wst_ref[...], xn, (((1,), (1,)), ((), ())), preferred_element_type=F32)
    off = 0
    for ref in out_refs:
        n = ref.shape[-1]
        ref[...] = jnp.dot(xn, w_ref[:, off:off + n], preferred_element_type=F32).astype(ref.dtype)
        off += n


def _in_proj(x, g, w_packed, groups, tm, w_small_t=None):
    m = x.shape[0]
    width = w_packed.shape[1]
    assert sum(n for n, _ in groups) == width and m % tm == 0
    with_t = w_small_t is not None
    in_specs = [pl.BlockSpec((tm, D_MODEL), lambda i: (i, 0)),
                _const_spec((1, D_MODEL)),
                _const_spec((D_MODEL, width))]
    out_specs = [pl.BlockSpec((tm, n), lambda i: (i, 0)) for n, _ in groups]
    out_shape = [jax.ShapeDtypeStruct((m, n), dt) for n, dt in groups]
    args = [x, g.reshape(1, D_MODEL), w_packed]
    if with_t:
        in_specs.append(_const_spec((LANE, D_MODEL)))
        out_specs.append(pl.BlockSpec((LANE, tm), lambda i: (0, i)))
        out_shape.append(jax.ShapeDtypeStruct((LANE, m), F32))
        args.append(w_small_t)
    return pl.pallas_call(
        functools.partial(_in_proj_kernel, with_small_t=with_t),
        grid=(m // tm,),
        in_specs=in_specs,
        out_specs=out_specs,
        out_shape=out_shape,
        compiler_params=_cparams(("parallel",)),
        name="in_proj",
    )(*args)


def _small_cols(a_beta, a_alpha, b_kr, c_i, c_f):
    small = jnp.zeros((D_MODEL, LANE), F32)
    small = small.at[:, SM_BETA:SM_BETA + GDN_HEADS].set(a_beta)
    small = small.at[:, SM_ALPHA:SM_ALPHA + GDN_HEADS].set(a_alpha)
    small = small.at[:, SM_I:SM_I + ML_HEADS].set(c_i)
    small = small.at[:, SM_F:SM_F + ML_HEADS].set(c_f)
    small = small.at[:, SM_KR:SM_KR + MLA_ROPE].set(b_kr)
    return small


PROMPT_GROUPS = ((GDN_QKV, F32), (GDN_V, BF16), (MLA_Q_LORA, BF16), (MLA_KV_LORA, F32),
                 (ML_QKV, BF16), (ML_V, BF16), (3 * D_MODEL, BF16), (LANE, F32))
SAMPLE_GROUPS = ((GDN_QKV, F32), (GDN_V, F32), (MLA_Q_LORA, F32), (MLA_KV_LORA, F32),
                 (2 * ML_HEADS * LANE + ML_V, F32), (ML_V, F32), (3 * D_MODEL, F32), (LANE, F32))


def _pack_w_in(w_in, pad_ml_heads):
    (a_qkv, a_beta, a_alpha, a_z, b_cq, b_ckv, b_kr,
     c_qkv, c_i, c_f, c_o, gates) = jnp.split(w_in, np.cumsum(IN_SIZES)[:-1].tolist(), axis=-1)
    if pad_ml_heads:
        qk = c_qkv[:, :2 * ML_QK].reshape(D_MODEL, 2 * ML_HEADS, ML_DK)
        qk = jnp.pad(qk, ((0, 0), (0, 0), (0, LANE - ML_DK))).reshape(D_MODEL, 2 * ML_HEADS * LANE)
        c_qkv = jnp.concatenate([qk, c_qkv[:, 2 * ML_QK:]], axis=1)
    small = _small_cols(a_beta, a_alpha, b_kr, c_i, c_f)
    packed = jnp.concatenate([a_qkv, a_z, b_cq, b_ckv, c_qkv, c_o, gates, small], axis=1).astype(BF16)
    return packed, small.T.astype(BF16)


def _gdn_prompt_kernel(aqkv_ref, az_ref, small_ref, small_t_ref, convw_ref, alog_ref, dtb_ref,
                       alog_c_ref, dtb_c_ref, ng_ref,
                       o_ref, conv_out_ref, s_out_ref, xp_scr, s_scr, *, tb):
    t = pl.program_id(1)
    c = REC_CHUNK
    nc = tb // c
    heads = range(GDN_HEADS)
    pairs = [(ci, h) for ci in range(nc) for h in heads]

    @pl.when(t == 0)
    def _():
        s_scr[...] = jnp.zeros_like(s_scr)
        xp_scr[0:8, :] = jnp.zeros((8, GDN_QKV), F32)

    x = aqkv_ref[...]
    xp_scr[8:8 + tb, :] = x
    w = convw_ref[...]
    y = (x * w[3:4] + xp_scr[7:7 + tb, :] * w[2:3]
         + xp_scr[6:6 + tb, :] * w[1:2] + xp_scr[5:5 + tb, :] * w[0:1])
    xp_scr[0:8, :] = x[tb - 8:tb, :]
    y = _silu(y)
    sm = small_ref[...]
    beta = _sigmoid(sm[:, SM_BETA:SM_BETA + GDN_HEADS])
    g_c = -jnp.exp(alog_ref[...]) * _softplus(sm[:, SM_ALPHA:SM_ALPHA + GDN_HEADS] + dtb_ref[...])
    g_r = -jnp.exp(alog_c_ref[...]) * _softplus(small_t_ref[SM_ALPHA:SM_ALPHA + GDN_HEADS, :] + dtb_c_ref[...])
    cmask = _chunk_cumsum_mask(tb, c)
    big_c = _dot01(cmask, g_c)
    big_r = _dot01_r(g_r, cmask.T)

    incl, strict, eye = _tri_masks(c)
    eye_f = jnp.where(eye, 1.0, 0.0).astype(F32)
    ng = ng_ref[...]
    rows = lambda ci: slice(ci * c, (ci + 1) * c)

    qs = [_l2(y[rows(ci), h * GDN_DK:(h + 1) * GDN_DK]) * (GDN_DK ** -0.5) for ci, h in pairs]
    ks = [_l2(y[rows(ci), GDN_QK + h * GDN_DK:GDN_QK + (h + 1) * GDN_DK]) for ci, h in pairs]
    vs = [y[rows(ci), 2 * GDN_QK + h * GDN_DV:2 * GDN_QK + (h + 1) * GDN_DV] for ci, h in pairs]
    bcols = [beta[rows(ci), h:h + 1] for ci, h in pairs]
    gcols = [big_c[rows(ci), h:h + 1] for ci, h in pairs]
    decs = [jnp.exp(jnp.where(incl, big_c[rows(ci), h:h + 1] - big_r[h:h + 1, rows(ci)], -jnp.inf))
            for ci, h in pairs]
    qkk = [_dot_nt(jnp.concatenate([q, k], axis=0), k) for q, k in zip(qs, ks)]
    a_list = [b * m[c:] * jnp.where(strict, d, 0.0) for b, m, d in zip(bcols, qkk, decs)]
    tinvs = _neumann_inverse_many(a_list, eye_f)
    e_gs = [jnp.exp(g) for g in gcols]
    uws = [_dot(ti, jnp.concatenate([b * v, (b * e) * k], axis=1))
           for ti, b, v, e, k in zip(tinvs, bcols, vs, e_gs, ks)]
    wqs = [jnp.concatenate([uw[:, GDN_DV:], q * e], axis=0).astype(BF16) for uw, q, e in zip(uws, qs, e_gs)]
    qkd = [(m[:c] * d).astype(BF16) for m, d in zip(qkk, decs)]
    kds = [(k * jnp.exp(g[c - 1:c, :] - g)).astype(BF16) for k, g in zip(ks, gcols)]
    gls = [jnp.exp(jnp.broadcast_to(g[c - 1:c, :], (1, GDN_DV))) for g in gcols]

    s_st = [s_scr[h] for h in heads]
    for ci in range(nc):
        idx = [ci * GDN_HEADS + h for h in heads]
        ws_qs = [jnp.dot(wqs[i], s_st[h].astype(BF16), preferred_element_type=F32)
                 for h, i in zip(heads, idx)]
        v_new = [(uws[i][:, :GDN_DV] - r[:c]).astype(BF16) for i, r in zip(idx, ws_qs)]
        o2 = [jnp.dot(qkd[i], vn, preferred_element_type=F32) for i, vn in zip(idx, v_new)]
        ds = [_dot_tn(kds[i], vn) for i, vn in zip(idx, v_new)]
        for h, i in zip(heads, idx):
            s_st[h] = gls[i] * s_st[h] + ds[h]
            o = ws_qs[h][c:] + o2[h]
            z = az_ref[rows(ci), h * GDN_DV:(h + 1) * GDN_DV].astype(F32)
            o_ref[rows(ci), h * GDN_DV:(h + 1) * GDN_DV] = (_rms(o, ng) * _silu(z)).astype(o_ref.dtype)
    for h in heads:
        s_scr[h] = s_st[h]

    @pl.when(t == pl.num_programs(1) - 1)
    def _():
        conv_out_ref[0] = x[tb - (GDN_CONV - 1):tb, :]
        s_out_ref[0] = s_scr[...]


def _gdn_prompt(a_qkv, a_z, small, small_t, conv_w, a_log, dt_bias, norm_g, bsz, seq, tb):
    nt = seq // tb
    row = lambda b, t: (b * nt + t, 0)
    col = lambda b, t: (0, b * nt + t)
    return pl.pallas_call(
        functools.partial(_gdn_prompt_kernel, tb=tb),
        grid=(bsz, nt),
        in_specs=[pl.BlockSpec((tb, GDN_QKV), row),
                  pl.BlockSpec((tb, GDN_V), row),
                  pl.BlockSpec((tb, LANE), row),
                  pl.BlockSpec((LANE, tb), col),
                  _const_spec((GDN_CONV, GDN_QKV)),
                  _const_spec((1, GDN_HEADS)),
                  _const_spec((1, GDN_HEADS)),
                  _const_spec((GDN_HEADS, 1)),
                  _const_spec((GDN_HEADS, 1)),
                  _const_spec((1, GDN_DV))],
        out_specs=[pl.BlockSpec((tb, GDN_V), row),
                   pl.BlockSpec((1, GDN_CONV - 1, GDN_QKV), lambda b, t: (b, 0, 0)),
                   pl.BlockSpec((1, GDN_HEADS, GDN_DK, GDN_DV), lambda b, t: (b, 0, 0, 0))],
        out_shape=[jax.ShapeDtypeStruct((bsz * seq, GDN_V), BF16),
                   jax.ShapeDtypeStruct((bsz, GDN_CONV - 1, GDN_QKV), F32),
                   jax.ShapeDtypeStruct((bsz, GDN_HEADS, GDN_DK, GDN_DV), F32)],
        scratch_shapes=[pltpu.VMEM((tb + 8, GDN_QKV), F32),
                        pltpu.VMEM((GDN_HEADS, GDN_DK, GDN_DV), F32)],
        compiler_params=_cparams(("parallel", "arbitrary")),
        name="gdn_prompt",
    )(a_qkv, a_z, small, small_t, conv_w, a_log.reshape(1, GDN_HEADS), dt_bias.reshape(1, GDN_HEADS),
      a_log.reshape(GDN_HEADS, 1), dt_bias.reshape(GDN_HEADS, 1), norm_g.reshape(1, GDN_DV))


def _mlstm_prompt_kernel(cqkv_ref, co_ref, small_ref, small_t_ref, bi_ref, bf_ref, bi_c_ref, bf_c_ref, ng_ref,
                         o_ref, c_out_ref, n_out_ref, m_out_ref, cn_scr, m_scr, *, tb):
    t = pl.program_id(1)
    c = ML_CHUNK
    nc = tb // c
    heads = range(ML_HEADS)
    pairs = [(ci, h) for ci in range(nc) for h in heads]

    @pl.when(t == 0)
    def _():
        cn_scr[...] = jnp.zeros_like(cn_scr)
        m_scr[...] = jnp.zeros_like(m_scr)

    sm = small_ref[...]
    ip_c = sm[:, SM_I:SM_I + ML_HEADS] + bi_ref[...]
    lf_c = -_softplus(-(sm[:, SM_F:SM_F + ML_HEADS] + bf_ref[...]))
    ip_r = small_t_ref[SM_I:SM_I + ML_HEADS, :] + bi_c_ref[...]
    lf_r = -_softplus(-(small_t_ref[SM_F:SM_F + ML_HEADS, :] + bf_c_ref[...]))
    cmask = _chunk_cumsum_mask(tb, c)
    f_c = _dot01(cmask, lf_c)
    f_r = _dot01_r(lf_r, cmask.T)

    incl, _, _ = _tri_masks(c)
    ng = ng_ref[...]
    rows = lambda ci: slice(ci * c, (ci + 1) * c)
    ones = jnp.ones((c, ML_DV), BF16)
    rep = lambda col: jnp.broadcast_to(col, (c, ML_DV))

    qs = [cqkv_ref[rows(ci), h * ML_DK:(h + 1) * ML_DK] for ci, h in pairs]
    ks = [cqkv_ref[rows(ci), ML_QK + h * ML_DK:ML_QK + (h + 1) * ML_DK].astype(F32) * (ML_DK ** -0.5)
          for ci, h in pairs]
    v1s = [jnp.concatenate([cqkv_ref[rows(ci), 2 * ML_QK + h * ML_DV:2 * ML_QK + (h + 1) * ML_DV], ones], axis=1)
           for ci, h in pairs]
    fcols = [f_c[rows(ci), h:h + 1] for ci, h in pairs]
    ds = [jnp.where(incl, f_c[rows(ci), h:h + 1] - f_r[h:h + 1, rows(ci)] + ip_r[h:h + 1, rows(ci)], -jnp.inf)
          for ci, h in pairs]
    dmaxs = [jnp.max(d, axis=-1, keepdims=True) for d in ds]
    qks = [_dot_nt(q, k) for q, k in zip(qs, ks)]
    p_locs = [jnp.exp(d - dm) * qk for d, dm, qk in zip(ds, dmaxs, qks)]
    pvs = [_dot(p, v1) for p, v1 in zip(p_locs, v1s)]
    wks = [jnp.exp(f[c - 1:c, :] - f + ip_c[rows(ci), h:h + 1] - dm[c - 1:c, :]) * k
           for (ci, h), f, dm, k in zip(pairs, fcols, dmaxs, ks)]
    kvs = [_dot_tn(wk, v1) for wk, v1 in zip(wks, v1s)]
    fbs = [rep(f) for f in fcols]
    dmbs = [rep(dm) for dm in dmaxs]

    cn_st = [cn_scr[h] for h in heads]
    m_st = [m_scr[h:h + 1, :] for h in heads]
    for ci in range(nc):
        idx = [ci * ML_HEADS + h for h in heads]
        qcs = [_dot(qs[i], cn_st[h]) for h, i in zip(heads, idx)]
        for h, i in zip(heads, idx):
            b_log = fbs[i] + m_st[h]
            m_r = jnp.maximum(b_log, dmbs[i])
            inter = jnp.exp(b_log - m_r)
            sc = jnp.exp(dmbs[i] - m_r)
            num = inter * qcs[h][:, :ML_DV] + sc * pvs[i][:, :ML_DV]
            den = inter * qcs[h][:, ML_DV:] + sc * pvs[i][:, ML_DV:]
            h_t = num / jnp.maximum(jnp.abs(den), jnp.exp(-m_r))
            m_new = m_r[c - 1:c, :]
            decay = jnp.exp(b_log[c - 1:c, :] - m_new)
            ksc = jnp.exp(dmbs[i][c - 1:c, :] - m_new)
            cn_st[h] = (jnp.concatenate([decay, decay], axis=1) * cn_st[h]
                        + jnp.concatenate([ksc, ksc], axis=1) * kvs[i])
            m_st[h] = m_new
            og = _sigmoid(co_ref[rows(ci), h * ML_DV:(h + 1) * ML_DV].astype(F32))
            o_ref[rows(ci), h * ML_DV:(h + 1) * ML_DV] = _rms(og * h_t, ng).astype(o_ref.dtype)
    for h in heads:
        cn_scr[h] = cn_st[h]
        m_scr[h:h + 1, :] = m_st[h]

    @pl.when(t == pl.num_programs(1) - 1)
    def _():
        c_out_ref[0] = cn_scr[:, :, 0:ML_DV]
        n_out_ref[0] = cn_scr[:, :, ML_DV:]
        m_out_ref[0] = m_scr[...]


def _mlstm_prompt(c_qkv, c_o, small, small_t, b_i, b_f, norm_g, bsz, seq, tb):
    nt = seq // tb
    row = lambda b, t: (b * nt + t, 0)
    col = lambda b, t: (0, b * nt + t)
    return pl.pallas_call(
        functools.partial(_mlstm_prompt_kernel, tb=tb),
        grid=(bsz, nt),
        in_specs=[pl.BlockSpec((tb, ML_QKV), row),
                  pl.BlockSpec((tb, ML_V), row),
                  pl.BlockSpec((tb, LANE), row),
                  pl.BlockSpec((LANE, tb), col),
                  _const_spec((1, ML_HEADS)),
                  _const_spec((1, ML_HEADS)),
                  _const_spec((ML_HEADS, 1)),
                  _const_spec((ML_HEADS, 1)),
                  _const_spec((1, ML_DV))],
        out_specs=[pl.BlockSpec((tb, ML_V), row),
                   pl.BlockSpec((1, ML_HEADS, ML_DK, ML_DV), lambda b, t: (b, 0, 0, 0)),
                   pl.BlockSpec((1, ML_HEADS, ML_DK, ML_DV), lambda b, t: (b, 0, 0, 0)),
                   pl.BlockSpec((1, ML_HEADS, LANE), lambda b, t: (b, 0, 0))],
        out_shape=[jax.ShapeDtypeStruct((bsz * seq, ML_V), BF16),
                   jax.ShapeDtypeStruct((bsz, ML_HEADS, ML_DK, ML_DV), F32),
                   jax.ShapeDtypeStruct((bsz, ML_HEADS, ML_DK, ML_DV), F32),
                   jax.ShapeDtypeStruct((bsz, ML_HEADS, LANE), F32)],
        scratch_shapes=[pltpu.VMEM((ML_HEADS, ML_DK, 2 * ML_DV), F32),
                        pltpu.VMEM((ML_HEADS, LANE), F32)],
        compiler_params=_cparams(("parallel", "arbitrary")),
        name="mlstm_prompt",
    )(c_qkv, c_o, small, small_t, b_i.reshape(1, ML_HEADS), b_f.reshape(1, ML_HEADS),
      b_i.reshape(ML_HEADS, 1), b_f.reshape(ML_HEADS, 1), norm_g.reshape(1, ML_DV))


def _rope_tables(pos):
    half = MLA_ROPE // 2
    inv_freq = ROPE_BASE ** (-jnp.arange(half, dtype=F32) / half)
    ang = pos.astype(F32)[:, None] * inv_freq[None, :]
    cos, sin = jnp.cos(ang), jnp.sin(ang)
    n = pos.shape[0]
    ones = jnp.ones((n, MLA_NOPE), F32)
    zeros = jnp.zeros((n, MLA_NOPE), F32)
    pad = jnp.zeros((n, LANE - MLA_QHEAD), F32)
    ctab = jnp.concatenate([ones, cos, cos, pad], axis=1)
    stab = jnp.concatenate([zeros, -sin, sin, pad], axis=1)
    return ctab, stab


def _mla_pre_kernel(cq_ref, ckv_ref, small_ref, ctab_ref, stab_ref, qg_ref, kvg_ref,
                    wuq_ref, wuk_ref, wuv_ref,
                    q_ref, k_ref, v_ref, ckv_out_ref, kr_out_ref):
    ctab = ctab_ref[...]
    stab = stab_ref[...]
    lane = lax.broadcasted_iota(jnp.int32, ctab.shape, 1)
    in_rope = (lane >= MLA_NOPE) & (lane < MLA_QHEAD)
    sm = small_ref[...]
    kr = jnp.where(in_rope, sm * ctab + _rope_swap(sm, MLA_NOPE) * stab, 0.0)
    kr_out_ref[0] = kr.T[MLA_NOPE:MLA_QHEAD, :]
    c_kv = _rms(ckv_ref[...], kvg_ref[...])
    ckv_out_ref[...] = c_kv
    c_kv_b = c_kv.astype(BF16)
    k_nope = jnp.dot(c_kv_b, wuk_ref[...], preferred_element_type=F32)
    v = jnp.dot(c_kv_b, wuv_ref[...], preferred_element_type=F32)
    v_t = v.T
    ones = jnp.ones((MLA_VDIM, v_t.shape[1]), v_ref.dtype)
    for h in range(MLA_HEADS):
        v_ref[0, h * HEAD_PAD:h * HEAD_PAD + MLA_VDIM, :] = v_t[h * MLA_VDIM:(h + 1) * MLA_VDIM, :].astype(v_ref.dtype)
        v_ref[0, h * HEAD_PAD + MLA_VDIM:(h + 1) * HEAD_PAD, :] = ones
    qn = _rms(cq_ref[...].astype(F32), qg_ref[...]).astype(BF16)
    q = jnp.dot(qn, wuq_ref[...], preferred_element_type=F32)
    q_sw = _rope_swap(q, MLA_NOPE)
    for h in range(MLA_HEADS):
        sl = slice(h * HEAD_PAD, (h + 1) * HEAD_PAD)
        q_ref[:, sl] = ((q[:, sl] * ctab + q_sw[:, sl] * stab) * (MLA_SCALE * LOG2E)).astype(q_ref.dtype)
        k_ref[:, sl] = (k_nope[:, sl] + kr).astype(k_ref.dtype)


def _pad_heads(w, head_dim):
    rows = w.shape[0]
    w = w.reshape(rows, MLA_HEADS, head_dim)
    return jnp.pad(w, ((0, 0), (0, 0), (0, HEAD_PAD - head_dim))).reshape(rows, MLA_HEADS * HEAD_PAD)


def _mla_pre(b_cq, b_ckv, small, ctab, stab, q_norm_g, kv_norm_g, w_uq_pad, w_uk_pad, w_uv, seq, tm):
    m = b_cq.shape[0]
    nt = seq // tm
    row = lambda i: (i, 0)
    tab = lambda i: (i % nt, 0)
    hp = MLA_HEADS * HEAD_PAD
    return pl.pallas_call(
        _mla_pre_kernel,
        grid=(m // tm,),
        in_specs=[pl.BlockSpec((tm, MLA_Q_LORA), row),
                  pl.BlockSpec((tm, MLA_KV_LORA), row),
                  pl.BlockSpec((tm, LANE), row),
                  pl.BlockSpec((tm, LANE), tab),
                  pl.BlockSpec((tm, LANE), tab),
                  _const_spec((1, MLA_Q_LORA)),
                  _const_spec((1, MLA_KV_LORA)),
                  _const_spec((MLA_Q_LORA, hp)),
                  _const_spec((MLA_KV_LORA, hp)),
                  _const_spec((MLA_KV_LORA, MLA_V))],
        out_specs=[pl.BlockSpec((tm, hp), row),
                   pl.BlockSpec((tm, hp), row),
                   pl.BlockSpec((1, hp, tm), lambda i: (i // nt, 0, i % nt)),
                   pl.BlockSpec((tm, MLA_KV_LORA), row),
                   pl.BlockSpec((1, MLA_ROPE, tm), lambda i: (i // nt, 0, i % nt))],
        out_shape=[jax.ShapeDtypeStruct((m, hp), BF16),
                   jax.ShapeDtypeStruct((m, hp), BF16),
                   jax.ShapeDtypeStruct((m // seq, hp, seq), BF16),
                   jax.ShapeDtypeStruct((m, MLA_KV_LORA), F32),
                   jax.ShapeDtypeStruct((m // seq, MLA_ROPE, seq), F32)],
        compiler_params=_cparams(("parallel",)),
        name="mla_pre",
    )(b_cq, b_ckv, small, ctab, stab, q_norm_g.reshape(1, -1), kv_norm_g.reshape(1, -1),
      w_uq_pad, w_uk_pad, w_uv)


def _flash_kernel(q_ref, k_ref, vt_ref, o_ref, m_scr, acc_scr, *, bq, bk):
    i = pl.program_id(1)
    n_full = (i * bq) // bk
    hsl = lambda h: slice(h * HEAD_PAD, (h + 1) * HEAD_PAD)
    m_scr[...] = jnp.full_like(m_scr, -jnp.inf)
    acc_scr[...] = jnp.zeros_like(acc_scr)

    def step(j, masked, q0):
        k0 = pl.multiple_of(j * bk, bk)
        nq = bq - q0
        if masked:
            kpos = j * bk + lax.broadcasted_iota(jnp.int32, (bk, nq), 0)
            qpos = i * bq + q0 + lax.broadcasted_iota(jnp.int32, (bk, nq), 1)
            keep = kpos <= qpos
        sts = [lax.dot_general(k_ref[pl.ds(k0, bk), hsl(h)], q_ref[q0:bq, hsl(h)], (((1,), (1,)), ((), ())),
                               preferred_element_type=F32) for h in range(MLA_HEADS)]
        for h in range(MLA_HEADS):
            st = jnp.where(keep, sts[h], -jnp.inf) if masked else sts[h]
            m_prev = m_scr[h, :, q0:bq]
            m_next = jnp.maximum(m_prev, jnp.max(st, axis=0, keepdims=True))
            p_t = jnp.exp2(st - m_next[0:1, :])
            alpha = jnp.exp2(m_prev - m_next)[0:1, :]
            acc_scr[h, :, q0:bq] = alpha * acc_scr[h, :, q0:bq] + jnp.dot(
                vt_ref[0, hsl(h), pl.ds(k0, bk)], p_t.astype(BF16), preferred_element_type=F32)
            m_scr[h, :, q0:bq] = m_next

    def body_full(j, carry):
        step(j, False, 0)
        return carry

    lax.fori_loop(0, n_full, body_full, 0)
    for d in range(bq // bk):
        step(n_full + d, True, d * bk)
    outs = []
    for h in range(MLA_HEADS):
        a = acc_scr[h]
        outs.append(a[0:MLA_VDIM, :] / a[MLA_VDIM:MLA_VDIM + 1, :])
    o_ref[...] = jnp.concatenate(outs, axis=0).T.astype(o_ref.dtype)


def _flash(q, k, v_t, bsz, seq, bq, bk):
    nq = seq // bq
    hp = MLA_HEADS * HEAD_PAD
    return pl.pallas_call(
        functools.partial(_flash_kernel, bq=bq, bk=bk),
        grid=(bsz, nq),
        in_specs=[pl.BlockSpec((bq, hp), lambda b, i: (b * nq + i, 0)),
                  pl.BlockSpec((seq, hp), lambda b, i: (b, 0)),
                  pl.BlockSpec((1, hp, seq), lambda b, i: (b, 0, 0))],
        out_specs=pl.BlockSpec((bq, MLA_V), lambda b, i: (b * nq + i, 0)),
        out_shape=jax.ShapeDtypeStruct((bsz * seq, MLA_V), BF16),
        scratch_shapes=[pltpu.VMEM((MLA_HEADS, 8, bq), F32), pltpu.VMEM((MLA_HEADS, HEAD_PAD, bq), F32)],
        compiler_params=_cparams(("parallel", "arbitrary")),
        name="mla_flash",
    )(q, k, v_t)


def _merge_kernel(x_ref, oa_ref, ob_ref, oc_ref, gates_ref, wa_ref, wb_ref, wc_ref, wo_ref, y_ref):
    g = gates_ref
    merged = (_sigmoid(g[:, 0:D_MODEL].astype(F32)) * _dot(oa_ref[...], wa_ref[...])
              + _sigmoid(g[:, D_MODEL:2 * D_MODEL].astype(F32)) * _dot(ob_ref[...], wb_ref[...])
              + _sigmoid(g[:, 2 * D_MODEL:].astype(F32)) * _dot(oc_ref[...], wc_ref[...]))
    y_ref[...] = x_ref[...] + _dot(merged, wo_ref[...])


def _merge(x, o_a, o_b, o_c, gates, w_a, w_b, w_c, w_o, tm):
    m = x.shape[0]
    row = lambda i: (i, 0)
    return pl.pallas_call(
        _merge_kernel,
        grid=(m // tm,),
        in_specs=[pl.BlockSpec((tm, D_MODEL), row),
                  pl.BlockSpec((tm, GDN_V), row),
                  pl.BlockSpec((tm, MLA_V), row),
                  pl.BlockSpec((tm, ML_V), row),
                  pl.BlockSpec((tm, 3 * D_MODEL), row),
                  _const_spec((GDN_V, D_MODEL)),
                  _const_spec((MLA_V, D_MODEL)),
                  _const_spec((ML_V, D_MODEL)),
                  _const_spec((D_MODEL, D_MODEL))],
        out_specs=pl.BlockSpec((tm, D_MODEL), row),
        out_shape=jax.ShapeDtypeStruct((m, D_MODEL), F32),
        compiler_params=_cparams(("parallel",)),
        name="merge",
    )(x, o_a, o_b, o_c, gates, w_a, w_b, w_c, w_o)


def _ffn_prompt_kernel(x_ref, g_ref, wup_ref, cw_ref, cb_ref, wdn_ref, gf_ref,
                       y_ref, conv_out_ref, gp_scr, *, tm, final):
    t = pl.program_id(1)

    @pl.when(t == 0)
    def _():
        gp_scr[0:8, :] = jnp.zeros((8, D_FF), F32)

    x = x_ref[...]
    hn = _rms(x, g_ref[...]).astype(BF16)
    u = jnp.dot(hn, wup_ref[:, 0:D_FF], preferred_element_type=F32)
    gt = jnp.dot(hn, wup_ref[:, D_FF:], preferred_element_type=F32)
    gp_scr[8:8 + tm, :] = gt
    cw = cw_ref[...]
    conv = gt * cw[2:3] + gp_scr[7:7 + tm, :] * cw[1:2] + gp_scr[6:6 + tm, :] * cw[0:1]
    gp_scr[0:8, :] = gt[tm - 8:tm, :]
    act = _silu(conv + cb_ref[...]) * u
    y = x + _dot(act, wdn_ref[...])
    if final:
        y = _rms(y, gf_ref[...])
    y_ref[...] = y

    @pl.when(t == pl.num_programs(1) - 1)
    def _():
        conv_out_ref[0] = gt[tm - (FFN_CONV - 1):tm, :]


def _ffn_prompt(x, norm_g, w_up, conv_w, conv_b, w_down, final_g, bsz, seq, tm, final):
    nt = seq // tm
    row = lambda b, t: (b * nt + t, 0)
    return pl.pallas_call(
        functools.partial(_ffn_prompt_kernel, tm=tm, final=final),
        grid=(bsz, nt),
        in_specs=[pl.BlockSpec((tm, D_MODEL), row),
                  _const_spec((1, D_MODEL)),
                  _const_spec((D_MODEL, 2 * D_FF)),
                  _const_spec((FFN_CONV, D_FF)),
                  _const_spec((1, D_FF)),
                  _const_spec((D_FF, D_MODEL)),
                  _const_spec((1, D_MODEL))],
        out_specs=[pl.BlockSpec((tm, D_MODEL), row),
                   pl.BlockSpec((1, FFN_CONV - 1, D_FF), lambda b, t: (b, 0, 0))],
        out_shape=[jax.ShapeDtypeStruct((bsz * seq, D_MODEL), F32),
                   jax.ShapeDtypeStruct((bsz, FFN_CONV - 1, D_FF), F32)],
        scratch_shapes=[pltpu.VMEM((tm + 8, D_FF), F32)],
        compiler_params=_cparams(("parallel", "arbitrary")),
        name="ffn_prompt",
    )(x, norm_g.reshape(1, -1), w_up, conv_w, conv_b.reshape(1, -1), w_down, final_g.reshape(1, -1))


def _ffn_sample_kernel(x_ref, hist_ref, g_ref, wup_ref, cw_ref, cb_ref, wdn_ref, gf_ref,
                       y_ref, hist_out_ref, *, final):
    x = x_ref[...]
    hn = _rms(x, g_ref[...]).astype(BF16)
    u = jnp.dot(hn, wup_ref[:, 0:D_FF], preferred_element_type=F32)
    gt = jnp.dot(hn, wup_ref[:, D_FF:], preferred_element_type=F32)
    cw = cw_ref[...]
    conv = hist_ref[0] * cw[0:1] + hist_ref[1] * cw[1:2] + gt * cw[2:3]
    act = _silu(conv + cb_ref[...]) * u
    y = x + _dot(act, wdn_ref[...])
    if final:
        y = _rms(y, gf_ref[...])
    y_ref[...] = y
    hist_out_ref[0] = hist_ref[1]
    hist_out_ref[1] = gt


def _ffn_sample(x, hist_t, norm_g, w_up, conv_w, conv_b, w_down, final_g, final):
    bs = x.shape[0]
    return pl.pallas_call(
        functools.partial(_ffn_sample_kernel, final=final),
        out_shape=[jax.ShapeDtypeStruct((bs, D_MODEL), F32),
                   jax.ShapeDtypeStruct((FFN_CONV - 1, bs, D_FF), F32)],
        compiler_params=pltpu.CompilerParams(vmem_limit_bytes=VMEM_LIMIT),
        name="ffn_sample",
    )(x, hist_t, norm_g.reshape(1, -1), w_up, conv_w, conv_b.reshape(1, -1), w_down, final_g.reshape(1, -1))


def _row_to_col(row, eye):
    n = eye.shape[0]
    return jnp.sum(jnp.where(eye, jnp.broadcast_to(row, (n, n)), 0.0), axis=1, keepdims=True)


def _rec_sample_kernel(aqkv_ref, hist_ref, az_ref, small_ref, convw_ref, alog_ref, dtb_ref, gng_ref,
                       s0_ref, cqkv_ref, co_ref, bi_ref, bf_ref, mng_ref, c0_ref, n0_ref, m0_ref,
                       oa_ref, hist_out_ref, s_out_ref, oc_ref, c_out_ref, n_out_ref, m_out_ref):
    _, _, eye128 = _tri_masks(LANE)
    lane = lax.broadcasted_iota(jnp.int32, (1, LANE), 1)
    sm = small_ref[0]
    x = aqkv_ref[0]
    hist = hist_ref[0]
    w = convw_ref[...]
    y = _silu(hist[0:1] * w[0:1] + hist[1:2] * w[1:2] + hist[2:3] * w[2:3] + x * w[3:4])
    hist_out_ref[0, 0:1, :] = hist[1:2]
    hist_out_ref[0, 1:2, :] = hist[2:3]
    hist_out_ref[0, 2:3, :] = x
    beta_row = _sigmoid(sm)
    decay_row = jnp.exp(-jnp.exp(alog_ref[...]) * _softplus(sm + dtb_ref[...]))
    gng = gng_ref[...]
    az = az_ref[0]
    cq = cqkv_ref[0]
    co = co_ref[0]
    ip_row = sm + bi_ref[...]
    lf_row = -_softplus(-(sm + bf_ref[...]))
    mng = mng_ref[...]
    m_prev = m0_ref[0]
    gh, mh = range(GDN_HEADS), range(ML_HEADS)

    g_q = [_l2(y[:, h * GDN_DK:(h + 1) * GDN_DK]) * (GDN_DK ** -0.5) for h in gh]
    g_k = [_l2(y[:, GDN_QK + h * GDN_DK:GDN_QK + (h + 1) * GDN_DK]) for h in gh]
    g_v = [y[:, 2 * GDN_QK + h * GDN_DV:2 * GDN_QK + (h + 1) * GDN_DV] for h in gh]
    m_q = [cq[:, h * LANE:(h + 1) * LANE] for h in mh]
    m_k = [cq[:, (ML_HEADS + h) * LANE:(ML_HEADS + h + 1) * LANE] * (ML_DK ** -0.5) for h in mh]
    m_v = [cq[:, 2 * ML_HEADS * LANE + h * ML_DV:2 * ML_HEADS * LANE + (h + 1) * ML_DV] for h in mh]
    g_kc = [_row_to_col(k, eye128) for k in g_k]
    g_qc = [_row_to_col(q, eye128) for q in g_q]
    m_kc = [_row_to_col(k, eye128)[0:ML_DK] for k in m_k]
    m_qc = [_row_to_col(q, eye128)[0:ML_DK] for q in m_q]
    g_qk = [jnp.sum(q * k, axis=-1, keepdims=True) for q, k in zip(g_q, g_k)]
    m_qk = [jnp.sum(q * k, axis=-1, keepdims=True) for q, k in zip(m_q, m_k)]
    s_old = [s0_ref[0, h] for h in gh]
    c_old = [c0_ref[0, h] for h in mh]
    n_old = [n0_ref[0, h:h + 1, :] for h in mh]
    g_ks = [jnp.sum(kc * s, axis=0, keepdims=True) for kc, s in zip(g_kc, s_old)]
    g_qs = [jnp.sum(qc * s, axis=0, keepdims=True) for qc, s in zip(g_qc, s_old)]
    m_qc_c = [jnp.sum(qc * c, axis=0, keepdims=True) for qc, c in zip(m_qc, c_old)]
    m_qn = [jnp.sum(q * n, axis=-1, keepdims=True) for q, n in zip(m_q, n_old)]

    for h in gh:
        beta = beta_row[:, SM_BETA + h:SM_BETA + h + 1]
        a = decay_row[:, SM_ALPHA + h:SM_ALPHA + h + 1]
        v_new = beta * (g_v[h] - a * g_ks[h])
        o = a * g_qs[h] + g_qk[h] * v_new
        s_out_ref[0, h] = a * s_old[h] + g_kc[h] * v_new
        z = az[:, h * GDN_DV:(h + 1) * GDN_DV]
        oa_ref[0, :, h * GDN_DV:(h + 1) * GDN_DV] = _rms(o, gng) * _silu(z)

    m_out = jnp.zeros((1, LANE), F32)
    for h in mh:
        ip = ip_row[:, SM_I + h:SM_I + h + 1]
        lf = lf_row[:, SM_F + h:SM_F + h + 1]
        m_old = m_prev[:, h:h + 1]
        b_log = lf + m_old
        m_r = jnp.maximum(b_log, ip)
        inter = jnp.exp(b_log - m_r)
        w_key = jnp.exp(ip - m_r)
        p = w_key * m_qk[h]
        num = inter * m_qc_c[h] + p * m_v[h]
        den = inter * m_qn[h] + p
        h_t = num / jnp.maximum(jnp.abs(den), jnp.exp(-m_r))
        c_out_ref[0, h] = inter * c_old[h] + (w_key * m_kc[h]) * m_v[h]
        n_out_ref[0, h:h + 1, :] = inter * n_old[h] + w_key * m_k[h]
        m_out = jnp.where(lane == h, m_r, m_out)
        og = _sigmoid(co[:, h * ML_DV:(h + 1) * ML_DV])
        oc_ref[0, :, h * ML_DV:(h + 1) * ML_DV] = _rms(og * h_t, mng)
    m_out_ref[0] = m_out


def _rec_sample(a_qkv, hist, a_z, small, conv_w, alog_row, dtb_row, gdn_norm_g, s0_all,
                c_qkv, c_o, bi_row, bf_row, ml_norm_g, c0_all, n0_pad, m0_pad, layer):
    bs = a_qkv.shape[0]
    r3 = lambda a: a.reshape(bs, 1, a.shape[-1])
    b3 = lambda n: pl.BlockSpec((1, 1, n), lambda b: (b, 0, 0))
    mlw = 2 * ML_HEADS * LANE + ML_V
    return pl.pallas_call(
        _rec_sample_kernel,
        grid=(bs,),
        in_specs=[b3(GDN_QKV),
                  pl.BlockSpec((1, GDN_CONV - 1, GDN_QKV), lambda b: (b, 0, 0)),
                  b3(GDN_V), b3(LANE),
                  _const_spec((GDN_CONV, GDN_QKV)),
                  _const_spec((1, LANE)), _const_spec((1, LANE)), _const_spec((1, GDN_DV)),
                  pl.BlockSpec((None, 1, GDN_HEADS, GDN_DK, GDN_DV), lambda b: (layer, b, 0, 0, 0)),
                  b3(mlw), b3(ML_V),
                  _const_spec((1, LANE)), _const_spec((1, LANE)), _const_spec((1, ML_DV)),
                  pl.BlockSpec((None, 1, ML_HEADS, ML_DK, ML_DV), lambda b: (layer, b, 0, 0, 0)),
                  pl.BlockSpec((1, ML_HEADS, LANE), lambda b: (b, 0, 0)),
                  b3(LANE)],
        out_specs=[b3(GDN_V),
                   pl.BlockSpec((1, GDN_CONV - 1, GDN_QKV), lambda b: (b, 0, 0)),
                   pl.BlockSpec((1, GDN_HEADS, GDN_DK, GDN_DV), lambda b: (b, 0, 0, 0)),
                   b3(ML_V),
                   pl.BlockSpec((1, ML_HEADS, ML_DK, ML_DV), lambda b: (b, 0, 0, 0)),
                   pl.BlockSpec((1, ML_HEADS, LANE), lambda b: (b, 0, 0)),
                   b3(LANE)],
        out_shape=[jax.ShapeDtypeStruct((bs, 1, GDN_V), F32),
                   jax.ShapeDtypeStruct((bs, GDN_CONV - 1, GDN_QKV), F32),
                   jax.ShapeDtypeStruct((bs, GDN_HEADS, GDN_DK, GDN_DV), F32),
                   jax.ShapeDtypeStruct((bs, 1, ML_V), F32),
                   jax.ShapeDtypeStruct((bs, ML_HEADS, ML_DK, ML_DV), F32),
                   jax.ShapeDtypeStruct((bs, ML_HEADS, LANE), F32),
                   jax.ShapeDtypeStruct((bs, 1, LANE), F32)],
        compiler_params=_cparams(("parallel",)),
        name="rec_sample",
    )(r3(a_qkv), hist, r3(a_z), r3(small), conv_w, alog_row, dtb_row, gdn_norm_g.reshape(1, -1), s0_all,
      r3(c_qkv), r3(c_o), bi_row, bf_row, ml_norm_g.reshape(1, -1), c0_all, n0_pad, r3(m0_pad))


def _mla_sample_pre_kernel(cq_ref, ckv_ref, small_ref, ctab_ref, stab_ref, qg_ref, kvg_ref,
                           wuq_ref, wukt_ref, qlat_ref, qrope_ref, ckv_out_ref, kr_out_ref):
    ctab = ctab_ref[...]
    stab = stab_ref[...]
    lane = lax.broadcasted_iota(jnp.int32, (1, LANE), 1)
    in_rope = (lane >= MLA_NOPE) & (lane < MLA_QHEAD)
    sm = small_ref[...]
    kr_out_ref[...] = jnp.where(in_rope, sm * ctab + _rope_swap(sm, MLA_NOPE) * stab, 0.0)
    ckv_out_ref[...] = _rms(ckv_ref[...], kvg_ref[...])
    qn = _rms(cq_ref[...], qg_ref[...])
    q = _dot(qn, wuq_ref[...])
    q_sw = _rope_swap(q, MLA_NOPE)
    for h in range(MLA_HEADS):
        sl = slice(h * HEAD_PAD, (h + 1) * HEAD_PAD)
        qh = q[:, sl]
        qrope_ref[:, sl] = jnp.where(in_rope, qh * ctab + q_sw[:, sl] * stab, 0.0)
        qlat_ref[:, sl] = _dot(jnp.where(lane < MLA_NOPE, qh, 0.0), wukt_ref[h])


def _mla_sample_pre(b_cq, b_ckv, small, ctab, stab, q_norm_g, kv_norm_g, w_uq_pad, w_uk_t):
    bs = b_cq.shape[0]
    hp = MLA_HEADS * HEAD_PAD
    return pl.pallas_call(
        _mla_sample_pre_kernel,
        out_shape=[jax.ShapeDtypeStruct((bs, hp), F32),
                   jax.ShapeDtypeStruct((bs, hp), F32),
                   jax.ShapeDtypeStruct((bs, MLA_KV_LORA), F32),
                   jax.ShapeDtypeStruct((bs, LANE), F32)],
        compiler_params=pltpu.CompilerParams(vmem_limit_bytes=VMEM_LIMIT),
        name="mla_sample_pre",
    )(b_cq, b_ckv, small, ctab, stab, q_norm_g.reshape(1, -1), kv_norm_g.reshape(1, -1), w_uq_pad, w_uk_t)


def _mla_sample_kernel(pt_ref, qlat_ref, qrope_ref, ckv_ref, kr_ref, lat_hbm, krt_hbm, ctx_ref,
                       lat_buf, krt_buf, sem, *, layer, n_pages):
    b = pl.program_id(0)
    slot = b % 2

    def page_copies(row, buf_slot, g):
        page = pt_ref[row, g]
        off = pl.multiple_of(g * PAGE_SIZE, PAGE_SIZE)
        return (pltpu.make_async_copy(lat_hbm.at[layer, page], lat_buf.at[buf_slot, pl.ds(off, PAGE_SIZE), :],
                                      sem.at[0, buf_slot]),
                pltpu.make_async_copy(krt_hbm.at[layer, page], krt_buf.at[buf_slot, :, pl.ds(off, PAGE_SIZE)],
                                      sem.at[1, buf_slot]))

    def start_row(row, buf_slot):
        def body(g, carry):
            for cp in page_copies(row, buf_slot, g):
                cp.start()
            return carry
        lax.fori_loop(0, n_pages, body, 0, unroll=8)

    def wait_row(buf_slot):
        pltpu.make_async_copy(lat_buf.at[buf_slot], lat_buf.at[buf_slot], sem.at[0, buf_slot]).wait()
        pltpu.make_async_copy(krt_buf.at[buf_slot], krt_buf.at[buf_slot], sem.at[1, buf_slot]).wait()

    @pl.when(b == 0)
    def _():
        start_row(0, 0)

    @pl.when(b + 1 < pl.num_programs(0))
    def _():
        start_row(b + 1, 1 - slot)

    wait_row(slot)

    q_lat = qlat_ref[0].astype(BF16)
    q_rope = qrope_ref[0][:, MLA_NOPE:MLA_QHEAD].astype(BF16)
    lat = lat_buf[slot].astype(BF16)
    s = (_dot_nt(q_lat, lat) + _dot(q_rope, krt_buf[slot])) * MLA_SCALE
    c_new = ckv_ref[0].astype(BF16).astype(F32)
    kr_new = kr_ref[0][:, MLA_NOPE:MLA_QHEAD].astype(BF16).astype(F32)
    s_new = (jnp.sum(q_lat.astype(F32) * c_new, axis=-1, keepdims=True)
             + jnp.sum(q_rope.astype(F32) * kr_new, axis=-1, keepdims=True)) * MLA_SCALE
    m = jnp.maximum(jnp.max(s, axis=-1, keepdims=True), s_new)
    p = jnp.exp(s - m)
    p_new = jnp.exp(s_new - m)
    denom = jnp.sum(p, axis=-1, keepdims=True) + p_new
    ctx_ref[0] = (_dot(p, lat) + p_new.astype(BF16).astype(F32) * c_new) / denom


def _mla_sample(page_table, q_lat, q_rope, c_kv, kr, cache_lat, cache_krt, layer):
    bs, n_pages = page_table.shape
    past = n_pages * PAGE_SIZE
    hp3 = lambda a: a.reshape(bs, MLA_HEADS, HEAD_PAD)
    per_b = lambda n: pl.BlockSpec((1, n, LANE), lambda b, pt: (b, 0, 0))
    grid_spec = pltpu.PrefetchScalarGridSpec(
        num_scalar_prefetch=1,
        grid=(bs,),
        in_specs=[per_b(MLA_HEADS), per_b(MLA_HEADS), per_b(1), per_b(1),
                  pl.BlockSpec(memory_space=pl.ANY), pl.BlockSpec(memory_space=pl.ANY)],
        out_specs=per_b(MLA_HEADS),
        scratch_shapes=[pltpu.VMEM((2, past, MLA_KV_LORA), F32),
                        pltpu.VMEM((2, MLA_ROPE, past), F32),
                        pltpu.SemaphoreType.DMA((2, 2))],
    )
    return pl.pallas_call(
        functools.partial(_mla_sample_kernel, layer=layer, n_pages=n_pages),
        grid_spec=grid_spec,
        out_shape=jax.ShapeDtypeStruct((bs, MLA_HEADS, LANE), F32),
        compiler_params=_cparams(("arbitrary",)),
        name="mla_sample",
    )(page_table, hp3(q_lat), hp3(q_rope), c_kv.reshape(bs, 1, LANE), kr.reshape(bs, 1, LANE),
      cache_lat, cache_krt)


def _ctx_up_kernel(ctx_ref, wuv_ref, o_ref):
    o_ref[...] = _dot(ctx_ref[...], wuv_ref[...])


def _ctx_up(ctx, w_uv_bd):
    return pl.pallas_call(
        _ctx_up_kernel,
        out_shape=jax.ShapeDtypeStruct((ctx.shape[0], MLA_V), F32),
        name="ctx_up",
    )(ctx, w_uv_bd)


def _lane_row(vals, lane0):
    return jnp.zeros((1, LANE), F32).at[0, lane0:lane0 + vals.shape[0]].set(vals.astype(F32))


def _pick(n, pref):
    t = min(n, pref)
    while n % t:
        t //= 2
    return t


def kernel(x_prompt, x_sample, cache_kv_latent, cache_k_rope, state_gdn_conv, state_gdn_S,
           state_mlstm_C, state_mlstm_n, state_mlstm_m, state_ffn_conv, page_table,
           norm1_g, w_in, gdn_conv_w, gdn_A_log, gdn_dt_bias, gdn_norm_g,
           mla_q_norm_g, mla_w_uq, mla_kv_norm_g, mla_w_uk, mla_w_uv,
           ml_b_i, ml_b_f, ml_norm_g, w_branch_a, w_branch_b, w_branch_c, w_out,
           norm2_g, ffn_w_up, ffn_conv_w, ffn_conv_b, ffn_w_down, final_norm_g):
    bp, seq = x_prompt.shape[:2]
    bs, dec = x_sample.shape[:2]
    assert dec == 1 and seq % max(REC_CHUNK, ML_CHUNK) == 0
    depth = w_in.shape[0]
    n_pages = page_table.shape[1]
    past_len = n_pages * PAGE_SIZE
    mp = bp * seq

    tm = _pick(seq, 512)
    tb = _pick(seq, 512)
    tw = _pick(seq, 1024)
    bq = _pick(seq, 1024)
    bk = _pick(seq, 256)

    ctab_p, stab_p = _rope_tables(jnp.arange(seq))
    ctab_s, stab_s = _rope_tables(jnp.full((1,), past_len))
    cache_krt = jnp.swapaxes(cache_k_rope, 2, 3)

    xp = x_prompt.reshape(mp, D_MODEL)
    xs = x_sample.reshape(bs, D_MODEL)
    out_p = [[] for _ in range(8)]
    out_s = [[] for _ in range(8)]

    for l in range(depth):
        last = l == depth - 1
        w_p, w_small_t = _pack_w_in(w_in[l], False)
        w_s, _ = _pack_w_in(w_in[l], True)
        w_uq_pad = _pad_heads(mla_w_uq[l], MLA_QHEAD).astype(BF16)
        w_uk_pad = _pad_heads(mla_w_uk[l], MLA_NOPE).astype(BF16)
        w_uv = mla_w_uv[l].astype(BF16)
        w_a, w_b, w_c, w_o = (w_branch_a[l].astype(BF16), w_branch_b[l].astype(BF16),
                              w_branch_c[l].astype(BF16), w_out[l].astype(BF16))
        w_up, w_dn = ffn_w_up[l].astype(BF16), ffn_w_down[l].astype(BF16)

        a_qkv, a_z, b_cq, b_ckv, c_qkv, c_o, gates, small, small_t = _in_proj(
            xp, norm1_g[l], w_p, PROMPT_GROUPS, tm, w_small_t)
        o_a, p_gconv, p_gs = _gdn_prompt(a_qkv, a_z, small, small_t, gdn_conv_w[l], gdn_A_log[l], gdn_dt_bias[l],
                                         gdn_norm_g[l], bp, seq, tb)
        q_pad, k_pad, v_all, p_ckv, p_kr = _mla_pre(b_cq, b_ckv, small, ctab_p, stab_p, mla_q_norm_g[l],
                                                    mla_kv_norm_g[l], w_uq_pad, w_uk_pad, w_uv, seq, tw)
        o_b = _flash(q_pad, k_pad, v_all, bp, seq, bq, bk)
        o_c, p_c, p_n, p_m = _mlstm_prompt(c_qkv, c_o, small, small_t, ml_b_i[l], ml_b_f[l], ml_norm_g[l], bp, seq, tb)
        xp = _merge(xp, o_a, o_b, o_c, gates, w_a, w_b, w_c, w_o, tw)
        xp, p_fconv = _ffn_prompt(xp, norm2_g[l], w_up, ffn_conv_w[l], ffn_conv_b[l], w_dn, final_norm_g,
                                  bp, seq, tm, last)
        for lst, a in zip(out_p, (p_ckv.reshape(bp, seq, MLA_KV_LORA), jnp.swapaxes(p_kr, 1, 2),
                                  p_gconv, p_gs, p_c, p_n[..., 0], p_m[..., 0], p_fconv)):
            lst.append(a)

        a_qkv, a_z, b_cq, b_ckv, c_qkv, c_o, gates, small = _in_proj(xs, norm1_g[l], w_s, SAMPLE_GROUPS, bs)
        n0_pad = jnp.pad(state_mlstm_n[l], ((0, 0), (0, 0), (0, LANE - ML_DK)))
        m0_pad = jnp.pad(state_mlstm_m[l], ((0, 0), (0, LANE - ML_HEADS)))
        o_a, s_gconv, s_gs, o_c, s_c, s_n, s_m = _rec_sample(
            a_qkv, state_gdn_conv[l], a_z, small, gdn_conv_w[l],
            _lane_row(gdn_A_log[l], SM_ALPHA), _lane_row(gdn_dt_bias[l], SM_ALPHA), gdn_norm_g[l],
            state_gdn_S, c_qkv, c_o, _lane_row(ml_b_i[l], SM_I), _lane_row(ml_b_f[l], SM_F),
            ml_norm_g[l], state_mlstm_C, n0_pad, m0_pad, l)
        w_uk_t = jnp.pad(jnp.transpose(mla_w_uk[l].reshape(MLA_KV_LORA, MLA_HEADS, MLA_NOPE), (1, 2, 0)),
                         ((0, 0), (0, HEAD_PAD - MLA_NOPE), (0, 0))).astype(BF16)
        q_lat, q_rope, s_ckv, kr_s = _mla_sample_pre(b_cq, b_ckv, small, ctab_s, stab_s, mla_q_norm_g[l],
                                                      mla_kv_norm_g[l], w_uq_pad, w_uk_t)
        ctx = _mla_sample(page_table, q_lat, q_rope, s_ckv, kr_s, cache_kv_latent, cache_krt, l)
        w_uv_bd = (jnp.eye(MLA_HEADS, dtype=F32)[:, None, :, None]
                   * jnp.transpose(mla_w_uv[l].reshape(MLA_KV_LORA, MLA_HEADS, MLA_VDIM), (1, 0, 2))[:, :, None, :]
                   ).reshape(MLA_HEADS * MLA_KV_LORA, MLA_V).astype(BF16)
        o_b = _ctx_up(ctx.reshape(bs, MLA_HEADS * LANE), w_uv_bd)
        xs = _merge(xs, o_a.reshape(bs, GDN_V), o_b, o_c.reshape(bs, ML_V), gates, w_a, w_b, w_c, w_o, bs)
        xs, s_fconv_t = _ffn_sample(xs, jnp.swapaxes(state_ffn_conv[l], 0, 1), norm2_g[l], w_up, ffn_conv_w[l],
                                    ffn_conv_b[l], w_dn, final_norm_g, last)
        for lst, a in zip(out_s, (s_ckv.reshape(bs, 1, MLA_KV_LORA),
                                  kr_s[:, MLA_NOPE:MLA_QHEAD].reshape(bs, 1, MLA_ROPE),
                                  s_gconv, s_gs, s_c, s_n[:, :, :ML_DK], s_m.reshape(bs, LANE)[:, :ML_HEADS],
                                  jnp.swapaxes(s_fconv_t, 0, 1))):
            lst.append(a)

    y_prompt = xp.reshape(bp, seq, D_MODEL)
    y_sample = xs.reshape(bs, 1, D_MODEL)
    return (y_prompt, y_sample) + tuple(jnp.stack(a) for a in out_p) + tuple(jnp.stack(a) for a in out_s)
```
